```python
import math
import jax
import jax.numpy as jnp
from jax import lax
import numpy as np

D_MODEL = 1024
BATCH = 2
SEQ = 8192
DEPTH = 2
DEC_BATCH = 128
DEC_SEQ = 4
PAST_LEN = 2048
PAGE_SIZE = 128

SSM_WIDTH = D_MODEL // 2
SSM_GROUP_CH = 16
N_SSM_GROUPS = SSM_WIDTH // SSM_GROUP_CH
SSM_STATE = 64
HEAD_DIM = 64
HEADS_PER_GROUP = D_MODEL // 256
WINDOWS = (128, 512, 2048)
DILATIONS = (1, 4, 16)
N_ATTN_GROUPS = len(WINDOWS)
ATTN_QKV_WIDTH = N_ATTN_GROUPS * HEADS_PER_GROUP * HEAD_DIM
ATTN_OUT_WIDTH = HEADS_PER_GROUP * HEAD_DIM
IN_PROJ_WIDTH = SSM_WIDTH + 3 * ATTN_QKV_WIDTH
N_EXPERT_GROUPS = 4
EXPERTS_PER_GROUP = 8
N_EXPERTS = N_EXPERT_GROUPS * EXPERTS_PER_GROUP
TOP_K_IN_GROUP = 2
EXPERT_FF = D_MODEL // 4
RMS_EPS = 1e-6
NEG_INF = -1e30

kernel_name = "hybrid_s5_dilated_attn_hmoe_step"


def rmsnorm(x, g):
    xf = x.astype(jnp.float32)
    xf = xf * lax.rsqrt(jnp.mean(xf * xf, axis=-1, keepdims=True) + RMS_EPS)
    return (xf * g.astype(jnp.float32)).astype(x.dtype)


def s5_scan(u, h0_re, h0_im, a_re, a_im, b_re, b_im, c_re, c_im, log_step, d_skip):
    n, t, _ = u.shape
    f32 = jnp.float32
    uf = u.astype(f32).reshape(n, t, N_SSM_GROUPS, SSM_GROUP_CH)
    lr, li = a_re.astype(f32), a_im.astype(f32)
    dt = jnp.exp(log_step.astype(f32))[:, None]
    mag = jnp.exp(lr * dt)
    ab_re, ab_im = mag * jnp.cos(li * dt), mag * jnp.sin(li * dt)
    den = lr * lr + li * li
    er, ei = ab_re - 1.0, ab_im
    f_re = (er * lr + ei * li) / den
    f_im = (ei * lr - er * li) / den
    br, bi = b_re.astype(f32), b_im.astype(f32)
    bb_re = f_re[..., None] * br - f_im[..., None] * bi
    bb_im = f_re[..., None] * bi + f_im[..., None] * br
    bu_re = jnp.einsum('ntgc,gpc->ntgp', uf, bb_re)
    bu_im = jnp.einsum('ntgc,gpc->ntgp', uf, bb_im)
    h0r, h0i = h0_re.astype(f32), h0_im.astype(f32)
    bu_re = bu_re.at[:, 0].add(ab_re * h0r - ab_im * h0i)
    bu_im = bu_im.at[:, 0].add(ab_re * h0i + ab_im * h0r)
    a_t_re = jnp.broadcast_to(ab_re, bu_re.shape)
    a_t_im = jnp.broadcast_to(ab_im, bu_im.shape)

    def combine(e1, e2):
        a1r, a1i, b1r, b1i = e1
        a2r, a2i, b2r, b2i = e2
        return (a2r * a1r - a2i * a1i, a2r * a1i + a2i * a1r,
                a2r * b1r - a2i * b1i + b2r, a2r * b1i + a2i * b1r + b2i)

    _, _, hr, hi = lax.associative_scan(combine, (a_t_re, a_t_im, bu_re, bu_im), axis=1)
    y = (jnp.einsum('ntgp,gcp->ntgc', hr, c_re.astype(f32))
         - jnp.einsum('ntgp,gcp->ntgc', hi, c_im.astype(f32)))
    y = y.reshape(n, t, SSM_WIDTH) + d_skip.astype(f32) * uf.reshape(n, t, SSM_WIDTH)
    return y, hr[:, -1], hi[:, -1]


def dilated_attn_prompt(q, k, v, dil, n_back):
    n, s, h, dh = q.shape
    f32 = jnp.float32
    length = s // dil
    nb = -(-length // n_back)
    pad = nb * n_back - length

    def by_residue(t_):
        return t_.reshape(n, length, dil, h, dh).transpose(0, 2, 1, 3, 4).astype(f32)

    qr, kr, vr = by_residue(q), by_residue(k), by_residue(v)
    qr = jnp.pad(qr, ((0, 0), (0, 0), (0, pad), (0, 0), (0, 0)))
    kr = jnp.pad(kr, ((0, 0), (0, 0), (n_back, pad), (0, 0), (0, 0)))
    vr = jnp.pad(vr, ((0, 0), (0, 0), (n_back, pad), (0, 0), (0, 0)))
    qb = qr.reshape(n, dil, nb, n_back, h, dh)
    kb = kr.reshape(n, dil, nb + 1, n_back, h, dh)
    vb = vr.reshape(n, dil, nb + 1, n_back, h, dh)
    kk = jnp.concatenate([kb[:, :, :-1], kb[:, :, 1:]], axis=3)
    vv = jnp.concatenate([vb[:, :, :-1], vb[:, :, 1:]], axis=3)
    sc = jnp.einsum('nrbqhd,nrbkhd->nrbhqk', qb, kk) * (dh ** -0.5)
    qi = jnp.arange(n_back)[:, None]
    kj = jnp.arange(2 * n_back)[None, :]
    band = (kj >= qi) & (kj <= qi + n_back)
    real = (jnp.arange(nb)[:, None, None] * n_back + kj[None]) >= n_back
    mask = band[None] & real
    sc = jnp.where(mask[:, None], sc, NEG_INF)
    m = jnp.max(sc, axis=-1, keepdims=True)
    p = jnp.exp(sc - m)
    den = jnp.sum(p, axis=-1)
    o = jnp.einsum('nrbhqk,nrbkhd->nrbqhd', p, vv) / den.transpose(0, 1, 2, 4, 3)[..., None]
    lse = m[..., 0] + jnp.log(den)
    o = o.reshape(n, dil, nb * n_back, h, dh)[:, :, :length].transpose(0, 2, 1, 3, 4).reshape(n, s, h, dh)
    lse = (lse.transpose(0, 1, 2, 4, 3).reshape(n, dil, nb * n_back, h)[:, :, :length]
           .transpose(0, 2, 1, 3).reshape(n, s, h))
    return o, lse


def dilated_attn_step(q, k_new, v_new, k_buf, v_buf, dil, n_back):
    n, t, h, dh = q.shape
    f32 = jnp.float32
    wb = k_buf.shape[1]
    kk = jnp.concatenate([k_buf, k_new], axis=1).astype(f32)
    vv = jnp.concatenate([v_buf, v_new], axis=1).astype(f32)
    idx = wb + jnp.arange(t)[:, None] - dil * jnp.arange(n_back + 1)[None, :]
    valid = idx >= 0
    idx = jnp.maximum(idx, 0)
    kg = kk[:, idx]
    vg = vv[:, idx]
    sc = jnp.einsum('nthd,ntmhd->nthm', q.astype(f32), kg) * (dh ** -0.5)
    sc = jnp.where(valid[None, :, None, :], sc, NEG_INF)
    m = jnp.max(sc, axis=-1, keepdims=True)
    p = jnp.exp(sc - m)
    den = jnp.sum(p, axis=-1)
    o = jnp.einsum('nthm,ntmhd->nthd', p, vg) / den[..., None]
    lse = m[..., 0] + jnp.log(den)
    return o, lse


def token_mixer(h, l, p, h0_re, h0_im, bufs):
    n, t, _ = h.shape
    proj = h @ p['w_in'][l]
    u = proj[..., :SSM_WIDTH]
    qkv = proj[..., SSM_WIDTH:].reshape(n, t, 3, N_ATTN_GROUPS, HEADS_PER_GROUP, HEAD_DIM)
    q, k, v = qkv[:, :, 0], qkv[:, :, 1], qkv[:, :, 2]
    y, hr, hi = s5_scan(u, h0_re, h0_im, p['ssm_a_re'][l], p['ssm_a_im'][l], p['ssm_b_re'][l],
                        p['ssm_b_im'][l], p['ssm_c_re'][l], p['ssm_c_im'][l], p['ssm_log_step'][l], p['ssm_d'][l])
    y = jax.nn.gelu(y).astype(h.dtype)
    y = y * jax.nn.sigmoid(y @ p['w_glu'][l])
    ssm_branch = y @ p['w_br_ssm'][l]
    outs, lses, new_kv = [], [], []
    for g in range(N_ATTN_GROUPS):
        dil = DILATIONS[g]
        n_back = WINDOWS[g] // dil
        qg, kg, vg = q[:, :, g], k[:, :, g], v[:, :, g]
        if bufs is None:
            o, lse = dilated_attn_prompt(qg, kg, vg, dil, n_back)
            keep = min(WINDOWS[g], t)
            new_kv += [kg[:, t - keep:], vg[:, t - keep:]]
        else:
            o, lse = dilated_attn_step(qg, kg, vg, bufs[2 * g], bufs[2 * g + 1], dil, n_back)
            new_kv += [kg, vg]
        outs.append(o)
        lses.append(lse)
    wgt = jax.nn.softmax(jnp.stack(lses, axis=0), axis=0)
    attn = jnp.sum(wgt[..., None] * jnp.stack(outs, axis=0), axis=0).reshape(n, t, ATTN_OUT_WIDTH)
    attn_branch = attn.astype(h.dtype) @ p['w_br_attn'][l]
    gates = jax.nn.sigmoid(h @ p['w_mgate'][l] + p['b_mgate'][l])
    g_ssm, g_attn = jnp.split(gates, 2, axis=-1)
    merged = g_ssm * ssm_branch + g_attn * attn_branch
    return merged @ p['w_out'][l], hr, hi, new_kv


def hier_moe(h, l, p):
    n, t, d = h.shape
    x = h.reshape(n * t, d)
    f32 = jnp.float32
    lg = (x @ p['w_router_grp'][l] + p['b_router_grp'][l]).astype(f32)
    pg = jax.nn.softmax(lg, axis=-1)
    gsel = jnp.argmax(lg, axis=-1)
    pg_sel = jnp.take_along_axis(pg, gsel[:, None], axis=-1)
    le = (x @ p['w_router_exp'][l] + p['b_router_exp'][l]).astype(f32).reshape(-1, N_EXPERT_GROUPS, EXPERTS_PER_GROUP)
    le_sel = jnp.take_along_axis(le, gsel[:, None, None], axis=1)[:, 0]
    pe = jax.nn.softmax(le_sel, axis=-1)
    w_top, e_top = lax.top_k(pe, TOP_K_IN_GROUP)
    w_top = w_top / jnp.sum(w_top, axis=-1, keepdims=True)
    eid = gsel[:, None] * EXPERTS_PER_GROUP + e_top
    combine = jnp.einsum('tk,tke->te', pg_sel * w_top, jax.nn.one_hot(eid, N_EXPERTS, dtype=f32))
    combine = combine.astype(x.dtype)
    out = jnp.zeros_like(x)
    for e in range(N_EXPERTS):
        hid = jax.nn.silu(x @ p['w_exp_gate'][l, e]) * (x @ p['w_exp_up'][l, e])
        out = out + combine[:, e:e + 1] * (hid @ p['w_exp_down'][l, e])
    return out.reshape(n, t, d)


def run_trunk(x, c, ssm_re0, ssm_im0, bufs, p):
    layer_states = []
    for l in range(DEPTH):
        mod = jax.nn.silu(c) @ p['w_ada'][l] + p['b_ada'][l]
        sh1, sc1, gt1, sh2, sc2, gt2 = jnp.split(mod[:, None, :], 6, axis=-1)
        h = rmsnorm(x, p['g_norm_mix'][l]) * (1.0 + sc1) + sh1
        lb = None if bufs is None else [b[l] for b in bufs]
        out, hr, hi, kv = token_mixer(h, l, p, ssm_re0[l], ssm_im0[l], lb)
        x = x + gt1 * out
        h = rmsnorm(x, p['g_norm_ffn'][l]) * (1.0 + sc2) + sh2
        x = x + gt2 * hier_moe(h, l, p)
        layer_states.append(kv + [hr, hi])
    y = rmsnorm(x, p['g_final'])
    n_entries = len(layer_states[0])
    new_state = [jnp.stack([layer_states[l][i] for l in range(DEPTH)], axis=0) for i in range(n_entries)]
    return y, new_state


def setup_inputs(seed: int = 0) -> dict:
    key = jax.random.key(seed)
    ks = iter(jax.random.split(key, 64))
    f32 = jnp.float32

    def nrm(shape, scale):
        return jax.random.normal(next(ks), shape, f32) * scale

    D = D_MODEL
    bufs = [min(w, PAST_LEN) for w in WINDOWS]
    kv_shape = lambda b: (DEPTH, DEC_BATCH, b, HEADS_PER_GROUP, HEAD_DIM)
    a_im0 = jnp.broadcast_to(jnp.arange(SSM_STATE, dtype=f32) * math.pi, (DEPTH, N_SSM_GROUPS, SSM_STATE))
    return {
        "x_prompt": nrm((BATCH, SEQ, D), 1.0),
        "x_sample": nrm((DEC_BATCH, DEC_SEQ, D), 1.0),
        "cache_k_w128": nrm(kv_shape(bufs[0]), 1.0),
        "cache_v_w128": nrm(kv_shape(bufs[0]), 1.0),
        "cache_k_w512": nrm(kv_shape(bufs[1]), 1.0),
        "cache_v_w512": nrm(kv_shape(bufs[1]), 1.0),
        "cache_k_w2048": nrm(kv_shape(bufs[2]), 1.0),
        "cache_v_w2048": nrm(kv_shape(bufs[2]), 1.0),
        "state_ssm_re": nrm((DEPTH, DEC_BATCH, N_SSM_GROUPS, SSM_STATE), 0.5),
        "state_ssm_im": nrm((DEPTH, DEC_BATCH, N_SSM_GROUPS, SSM_STATE), 0.5),
        "c_prompt": nrm((BATCH, D), 1.0),
        "c_sample": nrm((DEC_BATCH, D), 1.0),
        "w_ada": nrm((DEPTH, D, 6 * D), 0.5 * D ** -0.5),
        "b_ada": nrm((DEPTH, 6 * D), 0.02),
        "g_norm_mix": 1.0 + nrm((DEPTH, D), 0.01),
        "g_norm_ffn": 1.0 + nrm((DEPTH, D), 0.01),
        "g_final": 1.0 + nrm((D,), 0.01),
        "w_in": nrm((DEPTH, D, IN_PROJ_WIDTH), D ** -0.5),
        "ssm_a_re": -0.5 + nrm((DEPTH, N_SSM_GROUPS, SSM_STATE), 0.01),
        "ssm_a_im": a_im0 + nrm((DEPTH, N_SSM_GROUPS, SSM_STATE), 0.01),
        "ssm_b_re": nrm((DEPTH, N_SSM_GROUPS, SSM_STATE, SSM_GROUP_CH), (2 * SSM_GROUP_CH) ** -0.5),
        "ssm_b_im": nrm((DEPTH, N_SSM_GROUPS, SSM_STATE, SSM_GROUP_CH), (2 * SSM_GROUP_CH) ** -0.5),
        "ssm_c_re": nrm((DEPTH, N_SSM_GROUPS, SSM_GROUP_CH, SSM_STATE), (2 * SSM_STATE) ** -0.5),
        "ssm_c_im": nrm((DEPTH, N_SSM_GROUPS, SSM_GROUP_CH, SSM_STATE), (2 * SSM_STATE) ** -0.5),
        "ssm_log_step": jax.random.uniform(next(ks), (DEPTH, N_SSM_GROUPS), f32, math.log(0.001), math.log(0.1)),
        "ssm_d": nrm((DEPTH, SSM_WIDTH), 1.0),
        "w_glu": nrm((DEPTH, SSM_WIDTH, SSM_WIDTH), SSM_WIDTH ** -0.5),
        "w_br_ssm": nrm((DEPTH, SSM_WIDTH, D), SSM_WIDTH ** -0.5),
        "w_br_attn": nrm((DEPTH, ATTN_OUT_WIDTH, D), ATTN_OUT_WIDTH ** -0.5),
        "w_mgate": nrm((DEPTH, D, 2 * D), D ** -0.5),
        "b_mgate": nrm((DEPTH, 2 * D), 0.02),
        "w_out": nrm((DEPTH, D, D), D ** -0.5),
        "w_router_grp": nrm((DEPTH, D, N_EXPERT_GROUPS), D ** -0.5),
        "b_router_grp": nrm((DEPTH, N_EXPERT_GROUPS), 0.01),
        "w_router_exp": nrm((DEPTH, D, N_EXPERTS), D ** -0.5),
        "b_router_exp": nrm((DEPTH, N_EXPERTS), 0.01),
        "w_exp_gate": nrm((DEPTH, N_EXPERTS, D, EXPERT_FF), D ** -0.5),
        "w_exp_up": nrm((DEPTH, N_EXPERTS, D, EXPERT_FF), D ** -0.5),
        "w_exp_down": nrm((DEPTH, N_EXPERTS, EXPERT_FF, D), EXPERT_FF ** -0.5),
    }


def reference(x_prompt, x_sample, cache_k_w128, cache_v_w128, cache_k_w512, cache_v_w512,
              cache_k_w2048, cache_v_w2048, state_ssm_re, state_ssm_im, c_prompt, c_sample,
              w_ada, b_ada, g_norm_mix, g_norm_ffn, g_final, w_in, ssm_a_re, ssm_a_im, ssm_b_re,
              ssm_b_im, ssm_c_re, ssm_c_im, ssm_log_step, ssm_d, w_glu, w_br_ssm, w_br_attn,
              w_mgate, b_mgate, w_out, w_router_grp, b_router_grp, w_router_exp, b_router_exp,
              w_exp_gate, w_exp_up, w_exp_down):
    p = dict(w_ada=w_ada, b_ada=b_ada, g_norm_mix=g_norm_mix, g_norm_ffn=g_norm_ffn, g_final=g_final,
             w_in=w_in, ssm_a_re=ssm_a_re, ssm_a_im=ssm_a_im, ssm_b_re=ssm_b_re, ssm_b_im=ssm_b_im,
             ssm_c_re=ssm_c_re, ssm_c_im=ssm_c_im, ssm_log_step=ssm_log_step, ssm_d=ssm_d, w_glu=w_glu,
             w_br_ssm=w_br_ssm, w_br_attn=w_br_attn, w_mgate=w_mgate, b_mgate=b_mgate, w_out=w_out,
             w_router_grp=w_router_grp, b_router_grp=b_router_grp, w_router_exp=w_router_exp,
             b_router_exp=b_router_exp, w_exp_gate=w_exp_gate, w_exp_up=w_exp_up, w_exp_down=w_exp_down)
    zeros_state = jnp.zeros((DEPTH, x_prompt.shape[0], N_SSM_GROUPS, SSM_STATE), jnp.float32)
    y_prompt, pstate = run_trunk(x_prompt, c_prompt, zeros_state, zeros_state, None, p)
    sample_bufs = [cache_k_w128, cache_v_w128, cache_k_w512, cache_v_w512, cache_k_w2048, cache_v_w2048]
    y_sample, sstate = run_trunk(x_sample, c_sample, state_ssm_re, state_ssm_im, sample_bufs, p)
    pk128, pv128, pk512, pv512, pk2048, pv2048, p_re, p_im = pstate
    sk128, sv128, sk512, sv512, sk2048, sv2048, s_re, s_im = sstate
    return (y_prompt, y_sample, pk128, pv128, pk512, pv512, pk2048, pv2048, p_re, p_im,
            sk128, sv128, sk512, sv512, sk2048, sv2048, s_re, s_im)
```

```python
import functools
import math

import jax
import jax.numpy as jnp
from jax import lax
from jax.experimental import pallas as pl
from jax.experimental.pallas import tpu as pltpu

F32 = jnp.float32
BF16 = jnp.bfloat16

D_MODEL = 1024
DEPTH = 2
SSM_WIDTH = 512
SSM_GROUP_CH = 16
N_SSM_GROUPS = 32
SSM_STATE = 64
SSM_LANES = N_SSM_GROUPS * SSM_STATE
HEAD_DIM = 64
HEADS = 4
GROUP_WIDTH = HEADS * HEAD_DIM
WINDOWS = (128, 512, 2048)
DILATIONS = (1, 4, 16)
N_BACK = 128
N_GROUPS = 3
QKV_WIDTH = N_GROUPS * GROUP_WIDTH
IN_PROJ_WIDTH = SSM_WIDTH + 3 * QKV_WIDTH
N_EXPERT_GROUPS = 4
EXPERTS_PER_GROUP = 8
N_EXPERTS = 32
EXPERT_FF = 256
RMS_EPS = 1e-6
NEG_INF = -1e30
ROUTER_LANES = 128
GROUP_LOGIT_LANE0 = N_EXPERTS

SUBLANES = 8
VMEM_LIMIT = 56 * 1024 * 1024


def _cparams(sem):
    return pltpu.CompilerParams(dimension_semantics=sem, vmem_limit_bytes=VMEM_LIMIT)


def _sigmoid(x):
    return 1.0 / (1.0 + jnp.exp(-x))


def _gelu_tanh(x):
    return 0.5 * x * (1.0 + jnp.tanh(math.sqrt(2.0 / math.pi) * (x + 0.044715 * (x * x * x))))


def _dot(a, b):
    return jnp.dot(a, b, preferred_element_type=F32)


def _rmsnorm(x, g):
    ms = jnp.mean(x * x, axis=-1, keepdims=True)
    return x * lax.rsqrt(ms + RMS_EPS) * g


def _ada_kernel(c_ref, w_ref, b_ref, o_ref):
    c = c_ref[...]
    s = (c * _sigmoid(c)).astype(BF16)
    o_ref[...] = _dot(s, w_ref[...].astype(BF16)) + b_ref[...]


def _ada_call(c_all, w_ada, b_ada):
    rows = c_all.shape[0]
    tn = 1536
    return pl.pallas_call(
        _ada_kernel,
        grid=(DEPTH, 6 * D_MODEL // tn),
        in_specs=[
            pl.BlockSpec((rows, D_MODEL), lambda l, j: (0, 0)),
            pl.BlockSpec((None, D_MODEL, tn), lambda l, j: (l, 0, j)),
            pl.BlockSpec((None, 1, tn), lambda l, j: (l, 0, j)),
        ],
        out_specs=pl.BlockSpec((None, rows, tn), lambda l, j: (l, 0, j)),
        out_shape=jax.ShapeDtypeStruct((DEPTH, rows, 6 * D_MODEL), F32),
        compiler_params=_cparams(("parallel", "parallel")),
        name="ada_mod",
    )(c_all, w_ada, b_ada.reshape(DEPTH, 1, 6 * D_MODEL))


def _ssm_param_kernel(ar_ref, ai_ref, ls_ref, br_ref, bi_ref, abr_ref, abi_ref, bbr_ref, bbi_ref):
    lr, li = ar_ref[...], ai_ref[...]
    dt = jnp.exp(ls_ref[...])
    mag = jnp.exp(lr * dt)
    abr = mag * jnp.cos(li * dt)
    abi = mag * jnp.sin(li * dt)
    den = lr * lr + li * li
    er, ei = abr - 1.0, abi
    fr = (er * lr + ei * li) / den
    fi = (ei * lr - er * li) / den
    br, bi = br_ref[...], bi_ref[...]
    abr_ref[...] = abr
    abi_ref[...] = abi
    bbr_ref[...] = fr * br - fi * bi
    bbi_ref[...] = fr * bi + fi * br


def _ssm_param_call(a_re, a_im, log_step, b_re, b_im):
    w = SSM_STATE * SSM_GROUP_CH
    rep = lambda a: jnp.repeat(a, SSM_GROUP_CH, axis=-1)
    big = pl.BlockSpec((None, N_SSM_GROUPS, w), lambda l: (l, 0, 0))
    shp = jax.ShapeDtypeStruct((DEPTH, N_SSM_GROUPS, w), F32)
    return pl.pallas_call(
        _ssm_param_kernel,
        grid=(DEPTH,),
        in_specs=[big, big, pl.BlockSpec((None, N_SSM_GROUPS, 1), lambda l: (l, 0, 0)), big, big],
        out_specs=[big, big, big, big],
        out_shape=[shp, shp, shp, shp],
        compiler_params=_cparams(("parallel",)),
        name="ssm_params",
    )(rep(a_re), rep(a_im), log_step.reshape(DEPTH, N_SSM_GROUPS, 1),
      b_re.reshape(DEPTH, N_SSM_GROUPS, w), b_im.reshape(DEPTH, N_SSM_GROUPS, w))


def _block_diag_halves(blocks, dtype):
    g, r, c = blocks.shape
    half = g // 2
    eye = jnp.eye(half, dtype=bool)

    def one(b):
        m = jnp.where(eye[:, None, :, None], b[:, :, None, :], 0.0)
        return m.reshape(half * r, half * c)

    return jnp.stack([one(blocks[:half]), one(blocks[half:])]).astype(dtype)


def _inproj_kernel(x_ref, mod_ref, g_ref, w_ref, u_ref, a0_ref, a1_ref, a2_ref):
    h = _rmsnorm(x_ref[...], g_ref[...]) * (1.0 + mod_ref[1]) + mod_ref[0]
    hb = h.astype(BF16)
    u_ref[...] = _dot(hb, w_ref[:, :SSM_WIDTH])
    for g, ref in enumerate((a0_ref, a1_ref, a2_ref)):
        for a in range(3):
            c0 = SSM_WIDTH + a * QKV_WIDTH + g * GROUP_WIDTH
            ref[a] = _dot(hb, w_ref[:, c0:c0 + GROUP_WIDTH])


def _mod_spec(mod, tm, tiles_per_seq):
    if mod.ndim == 4:
        return pl.BlockSpec((None, 6, 1, D_MODEL), lambda i: (i // tiles_per_seq, 0, 0, 0))
    return pl.BlockSpec((6, tm, D_MODEL), lambda i: (0, i, 0))


def _inproj_call(x, mod, g_mix, w_in, l, tm, tiles_per_seq):
    ntok = x.shape[0]
    grp = jax.ShapeDtypeStruct((3, ntok, GROUP_WIDTH), F32)
    grp_spec = pl.BlockSpec((3, tm, GROUP_WIDTH), lambda i: (0, i, 0))
    return pl.pallas_call(
        _inproj_kernel,
        grid=(ntok // tm,),
        in_specs=[
            pl.BlockSpec((tm, D_MODEL), lambda i: (i, 0)),
            _mod_spec(mod, tm, tiles_per_seq),
            pl.BlockSpec((None, 1, D_MODEL), lambda i: (l, 0, 0)),
            pl.BlockSpec((None, D_MODEL, IN_PROJ_WIDTH), lambda i: (l, 0, 0)),
        ],
        out_specs=[pl.BlockSpec((tm, SSM_WIDTH), lambda i: (i, 0)), grp_spec, grp_spec, grp_spec],
        out_shape=[jax.ShapeDtypeStruct((ntok, SSM_WIDTH), F32), grp, grp, grp],
        compiler_params=_cparams(("parallel",)),
        name="in_proj",
    )(x, mod, g_mix, w_in)


SCAN_LANES = 256
SCAN_CHUNK = 256


def _cmul(ar, ai, br, bi):
    return ar * br - ai * bi, ar * bi + ai * br


def _ssm_prompt_kernel(u_ref, ab_ref, bre_ref, bim_ref, cre_ref, cim_ref, d_ref,
                       y_ref, hre_ref, him_ref, bur, bui, tab, carry):
    c = pl.program_id(1)
    half = SSM_LANES // 2

    @pl.when(c == 0)
    def _():
        shp = (SUBLANES, SSM_LANES)
        sub = lax.broadcasted_iota(jnp.int32, shp, 0)
        p_r = jnp.broadcast_to(ab_ref[0:1, :], shp)
        p_i = jnp.broadcast_to(ab_ref[1:2, :], shp)
        a_r, a_i = p_r, p_i
        pw_r, pw_i = p_r, p_i
        for s in range(1, SUBLANES + 1):
            if s in (1, 2, 4):
                k = {1: 0, 2: 2, 4: 4}[s]
                tab[k] = jnp.where(sub >= s, p_r, 0.0)
                tab[k + 1] = jnp.where(sub >= s, p_i, 0.0)
            pw_r = jnp.where(sub >= s - 1, p_r, pw_r)
            pw_i = jnp.where(sub >= s - 1, p_i, pw_i)
            p_r, p_i = _cmul(p_r, p_i, a_r, a_i)
        tab[6] = pw_r
        tab[7] = pw_i
        carry[...] = jnp.zeros_like(carry)

    u = u_ref[...]
    ub = u.astype(BF16)
    for h in range(2):
        uh = ub[:, h * 256:(h + 1) * 256]
        bur[:, h * half:(h + 1) * half] = _dot(uh, bre_ref[h])
        bui[:, h * half:(h + 1) * half] = _dot(uh, bim_ref[h])

    n_blocks = SCAN_CHUNK // SUBLANES
    for lc in range(SSM_LANES // SCAN_LANES):
        sl = slice(lc * SCAN_LANES, (lc + 1) * SCAN_LANES)
        steps = [(tab[0, :, sl], tab[1, :, sl], 1), (tab[2, :, sl], tab[3, :, sl], 2),
                 (tab[4, :, sl], tab[5, :, sl], 4)]
        pw_r, pw_i = tab[6, :, sl], tab[7, :, sl]

        def body(b, cvals, sl=sl, steps=steps, pw_r=pw_r, pw_i=pw_i):
            cr, ci = cvals
            r0 = pl.multiple_of(b * SUBLANES, SUBLANES)
            xr = bur[pl.ds(r0, SUBLANES), sl]
            xi = bui[pl.ds(r0, SUBLANES), sl]
            for ar, ai, k in steps:
                rr = pltpu.roll(xr, k, axis=0)
                ri = pltpu.roll(xi, k, axis=0)
                xr, xi = xr + (ar * rr - ai * ri), xi + (ar * ri + ai * rr)
            hr = xr + (pw_r * cr - pw_i * ci)
            hi = xi + (pw_r * ci + pw_i * cr)
            bur[pl.ds(r0, SUBLANES), sl] = hr
            bui[pl.ds(r0, SUBLANES), sl] = hi
            shp = (SUBLANES, SCAN_LANES)
            return (jnp.broadcast_to(hr[SUBLANES - 1:SUBLANES, :], shp),
                    jnp.broadcast_to(hi[SUBLANES - 1:SUBLANES, :], shp))

        cr, ci = lax.fori_loop(0, n_blocks, body, (carry[0, :, sl], carry[1, :, sl]), unroll=2)
        carry[0, :, sl] = cr
        carry[1, :, sl] = ci

    for h in range(2):
        hs = slice(h * half, (h + 1) * half)
        cs = slice(h * 256, (h + 1) * 256)
        y = _dot(bur[:, hs].astype(BF16), cre_ref[h]) - _dot(bui[:, hs].astype(BF16), cim_ref[h])
        y_ref[:, cs] = y + d_ref[:, cs] * u[:, cs]

    @pl.when(c == pl.num_programs(1) - 1)
    def _():
        hre_ref[...] = carry[0]
        him_ref[...] = carry[1]


def _ssm_prompt_call(u, ab, bre, bim, cre, cim, d_skip, n_seq, seq):
    chunks = seq // SCAN_CHUNK
    half = SSM_LANES // 2
    wspec_b = pl.BlockSpec((2, 256, half), lambda n, c: (0, 0, 0))
    wspec_c = pl.BlockSpec((2, half, 256), lambda n, c: (0, 0, 0))
    st_spec = pl.BlockSpec((None, SUBLANES, SSM_LANES), lambda n, c: (n, 0, 0))
    st_shape = jax.ShapeDtypeStruct((n_seq, SUBLANES, SSM_LANES), F32)
    return pl.pallas_call(
        _ssm_prompt_kernel,
        grid=(n_seq, chunks),
        in_specs=[
            pl.BlockSpec((SCAN_CHUNK, SSM_WIDTH), lambda n, c: (n * chunks + c, 0)),
            pl.BlockSpec((2, SSM_LANES), lambda n, c: (0, 0)),
            wspec_b, wspec_b, wspec_c, wspec_c,
            pl.BlockSpec((1, SSM_WIDTH), lambda n, c: (0, 0)),
        ],
        out_specs=[pl.BlockSpec((SCAN_CHUNK, SSM_WIDTH), lambda n, c: (n * chunks + c, 0)),
                   st_spec, st_spec],
        out_shape=[jax.ShapeDtypeStruct((n_seq * seq, SSM_WIDTH), F32), st_shape, st_shape],
        scratch_shapes=[pltpu.VMEM((SCAN_CHUNK, SSM_LANES), F32),
                        pltpu.VMEM((SCAN_CHUNK, SSM_LANES), F32),
                        pltpu.VMEM((8, SUBLANES, SSM_LANES), F32),
                        pltpu.VMEM((2, SUBLANES, SSM_LANES), F32)],
        compiler_params=_cparams(("parallel", "arbitrary")),
        name="ssm_prompt",
    )(u, ab, bre, bim, cre, cim, d_skip)


def _ssm_sample_kernel(u_ref, ab_ref, bre_ref, bim_ref, cre_ref, cim_ref, d_ref, h0r_ref, h0i_ref,
                       y_ref, hr_ref, hi_ref, *, steps):
    half = SSM_LANES // 2
    ar, ai = ab_ref[0:1, :], ab_ref[1:2, :]
    hr_ref[...] = h0r_ref[...]
    hi_ref[...] = h0i_ref[...]
    for t in range(steps):
        u = u_ref[:, t * SSM_WIDTH:(t + 1) * SSM_WIDTH]
        ub = u.astype(BF16)
        for h in range(2):
            hs = slice(h * half, (h + 1) * half)
            cs = slice(h * 256, (h + 1) * 256)
            uh = ub[:, cs]
            hr, hi = hr_ref[:, hs], hi_ref[:, hs]
            a_r, a_i = ar[:, hs], ai[:, hs]
            nr = (a_r * hr - a_i * hi) + _dot(uh, bre_ref[h])
            ni = (a_r * hi + a_i * hr) + _dot(uh, bim_ref[h])
            hr_ref[:, hs] = nr
            hi_ref[:, hs] = ni
            y = _dot(nr.astype(BF16), cre_ref[h]) - _dot(ni.astype(BF16), cim_ref[h])
            y_ref[:, t * SSM_WIDTH + h * 256:t * SSM_WIDTH + (h + 1) * 256] = y + d_ref[:, cs] * u[:, cs]


def _ssm_sample_call(u, ab, bre, bim, cre, cim, d_skip, h0r, h0i, steps):
    n = u.shape[0]
    st = jax.ShapeDtypeStruct((n, SSM_LANES), F32)
    return pl.pallas_call(
        functools.partial(_ssm_sample_kernel, steps=steps),
        out_shape=[jax.ShapeDtypeStruct((n, steps * SSM_WIDTH), F32), st, st],
        compiler_params=pltpu.CompilerParams(vmem_limit_bytes=VMEM_LIMIT),
        name="ssm_sample",
    )(u, ab, bre, bim, cre, cim, d_skip, h0r, h0i)


def _attn_prompt_kernel(q_ref, kp_ref, kc_ref, vp_ref, vc_ref, o_ref, lse_ref):
    b = pl.program_id(2)
    q = (q_ref[...] * (HEAD_DIM ** -0.5)).astype(BF16)
    kk = jnp.concatenate([kp_ref[...], kc_ref[...]], axis=0).astype(BF16)
    vv = jnp.concatenate([vp_ref[...], vc_ref[...]], axis=0).astype(BF16)
    qi = lax.broadcasted_iota(jnp.int32, (N_BACK, 2 * N_BACK), 0)
    kj = lax.broadcasted_iota(jnp.int32, (N_BACK, 2 * N_BACK), 1)
    first_real = jnp.where(b > 0, 0, N_BACK)
    valid = (kj >= qi) & (kj <= qi + N_BACK) & (kj >= first_real)
    lane = lax.broadcasted_iota(jnp.int32, (N_BACK, 128), 1)
    for hp in range(2):
        ls = slice(hp * 128, (hp + 1) * 128)
        qp, kp, vp = q[:, ls], kk[:, ls], vv[:, ls]
        outs, lses = [], []
        for sub in range(2):
            in_head = (lane // HEAD_DIM) == sub
            qm = jnp.where(in_head, qp, jnp.zeros_like(qp))
            s = lax.dot_general(qm, kp, (((1,), (1,)), ((), ())), preferred_element_type=F32)
            s = jnp.where(valid, s, NEG_INF)
            m = jnp.max(s, axis=-1, keepdims=True)
            p = jnp.exp(s - m)
            den = jnp.sum(p, axis=-1, keepdims=True)
            outs.append(_dot(p.astype(BF16), vp) / den)
            lses.append(jnp.broadcast_to(m + jnp.log(den), (N_BACK, 128)))
        first = (lane // HEAD_DIM) == 0
        o_ref[:, ls] = jnp.where(first, outs[0], outs[1])
        lse_ref[:, ls] = jnp.where(first, lses[0], lses[1])


def _attn_prompt_call(qkv, n_seq, seq, dil):
    length = seq // dil
    nb = length // N_BACK
    arr = qkv.reshape(3, n_seq, length, dil * GROUP_WIDTH)
    blk = (None, None, N_BACK, GROUP_WIDTH)
    cur = lambda a: pl.BlockSpec(blk, lambda n, r, b: (a, n, b, r))
    prev = lambda a: pl.BlockSpec(blk, lambda n, r, b: (a, n, jnp.maximum(b - 1, 0), r))
    out_spec = pl.BlockSpec((None, N_BACK, GROUP_WIDTH), lambda n, r, b: (n, b, r))
    out_shape = jax.ShapeDtypeStruct((n_seq, length, dil * GROUP_WIDTH), F32)
    o, lse = pl.pallas_call(
        _attn_prompt_kernel,
        grid=(n_seq, dil, nb),
        in_specs=[cur(0), prev(1), cur(1), prev(2), cur(2)],
        out_specs=[out_spec, out_spec],
        out_shape=[out_shape, out_shape],
        compiler_params=_cparams(("parallel", "parallel", "parallel")),
        name=f"attn_prompt_d{dil}",
    )(arr, arr, arr, arr, arr)
    return o.reshape(n_seq * seq, GROUP_WIDTH), lse.reshape(n_seq * seq, GROUP_WIDTH)


SAMPLE_SEQ_BLOCK = 8


def _attn_sample_kernel(qkv_ref, kb_ref, vb_ref, o_ref, lse_ref, *, dil, steps):
    nb = qkv_ref.shape[1]
    qn = qkv_ref[0] * (HEAD_DIM ** -0.5)
    kn = qkv_ref[1]
    vn = qkv_ref[2]
    row = lax.broadcasted_iota(jnp.int32, (SUBLANES, GROUP_WIDTH), 0)
    lane = lax.broadcasted_iota(jnp.int32, (SUBLANES, GROUP_WIDTH), 1)
    hmask = jnp.where((lane // HEAD_DIM) == row, 1.0, 0.0)[None]
    key = lax.broadcasted_iota(jnp.int32, (nb, SUBLANES, N_BACK), 2)
    for t in range(steps):
        cs = slice(0, GROUP_WIDTH) if dil == 1 else slice(t * GROUP_WIDTH, (t + 1) * GROUP_WIDTH)
        kbuf = kb_ref[:, :, cs].astype(BF16)
        vbuf = vb_ref[:, :, cs].astype(BF16)
        qrows = qn[:, t:t + 1, :] * hmask
        s = jnp.einsum('nqc,nkc->nqk', qrows.astype(BF16), kbuf, preferred_element_type=F32)
        if dil == 1:
            s = jnp.where(key >= t, s, NEG_INF)
        new_ids = list(range(t + 1)) if dil == 1 else [t]
        s_new = [jnp.sum(qrows * kn[:, j:j + 1, :], axis=-1, keepdims=True) for j in new_ids]
        m = jnp.max(s, axis=-1, keepdims=True)
        for sn in s_new:
            m = jnp.maximum(m, sn)
        p = jnp.exp(s - m)
        den = jnp.sum(p, axis=-1, keepdims=True)
        acc = jnp.einsum('nqk,nkc->nqc', p.astype(BF16), vbuf, preferred_element_type=F32)
        for j, sn in zip(new_ids, s_new):
            pn = jnp.exp(sn - m)
            den = den + pn
            acc = acc + pn * vn[:, j:j + 1, :]
        o8 = acc / den
        lse8 = m + jnp.log(den)
        o_ref[:, t:t + 1, :] = jnp.sum(o8 * hmask, axis=1, keepdims=True)
        lse_ref[:, t:t + 1, :] = jnp.sum(lse8 * hmask, axis=1, keepdims=True)


def _attn_sample_call(qkv, k_cache, v_cache, l, n_seq, steps, dil):
    arr = qkv.reshape(3, n_seq, steps, GROUP_WIDTH)
    kc = k_cache.reshape(DEPTH, n_seq, N_BACK, dil * GROUP_WIDTH)
    vc = v_cache.reshape(DEPTH, n_seq, N_BACK, dil * GROUP_WIDTH)
    cw = GROUP_WIDTH if dil == 1 else steps * GROUP_WIDTH
    nb = SAMPLE_SEQ_BLOCK
    cache_spec = pl.BlockSpec((None, nb, N_BACK, cw), lambda i: (l, i, 0, 0))
    out_spec = pl.BlockSpec((nb, steps, GROUP_WIDTH), lambda i: (i, 0, 0))
    out_shape = jax.ShapeDtypeStruct((n_seq, steps, GROUP_WIDTH), F32)
    o, lse = pl.pallas_call(
        functools.partial(_attn_sample_kernel, dil=dil, steps=steps),
        grid=(n_seq // nb,),
        in_specs=[pl.BlockSpec((3, nb, steps, GROUP_WIDTH), lambda i: (0, i, 0, 0)),
                  cache_spec, cache_spec],
        out_specs=[out_spec, out_spec],
        out_shape=[out_shape, out_shape],
        compiler_params=_cparams(("parallel",)),
        name=f"attn_sample_d{dil}",
    )(arr, kc, vc)
    return o.reshape(n_seq * steps, GROUP_WIDTH), lse.reshape(n_seq * steps, GROUP_WIDTH)


def _merge_kernel(x_ref, mod_ref, y_ref, o0_ref, l0_ref, o1_ref, l1_ref, o2_ref, l2_ref,
                  gmix_ref, gffn_ref, wmg_ref, bmg_ref, wglu_ref, wbs_ref, wba_ref, wout_ref,
                  wrh_ref, wrl_ref, br_ref, x1_ref, h2_ref, comb_ref):
    x = x_ref[...]
    tm = x.shape[0]
    hb = (_rmsnorm(x, gmix_ref[...]) * (1.0 + mod_ref[1]) + mod_ref[0]).astype(BF16)

    y = _gelu_tanh(y_ref[...])
    yg = y * _sigmoid(_dot(y.astype(BF16), wglu_ref[...]))
    ssm_branch = _dot(yg.astype(BF16), wbs_ref[...])

    l0, l1, l2 = l0_ref[...], l1_ref[...], l2_ref[...]
    lm = jnp.maximum(jnp.maximum(l0, l1), l2)
    e0, e1, e2 = jnp.exp(l0 - lm), jnp.exp(l1 - lm), jnp.exp(l2 - lm)
    attn = (e0 * o0_ref[...] + e1 * o1_ref[...] + e2 * o2_ref[...]) / (e0 + e1 + e2)
    attn_branch = _dot(attn.astype(BF16), wba_ref[...])

    g_ssm = _sigmoid(_dot(hb, wmg_ref[:, :D_MODEL]) + bmg_ref[:, :D_MODEL])
    g_attn = _sigmoid(_dot(hb, wmg_ref[:, D_MODEL:]) + bmg_ref[:, D_MODEL:])
    merged = g_ssm * ssm_branch + g_attn * attn_branch
    x1 = x + mod_ref[2] * _dot(merged.astype(BF16), wout_ref[...])
    x1_ref[...] = x1

    h2 = _rmsnorm(x1, gffn_ref[...]) * (1.0 + mod_ref[4]) + mod_ref[3]
    h2_hi = h2.astype(BF16)
    h2_ref[...] = h2_hi
    h2_lo = (h2 - h2_hi.astype(F32)).astype(BF16)
    logits = (_dot(h2_hi, wrh_ref[...]) + _dot(h2_hi, wrl_ref[...]) + _dot(h2_lo, wrh_ref[...])
              + br_ref[...])

    lane = lax.broadcasted_iota(jnp.int32, (tm, ROUTER_LANES), 1)
    lanef = lane.astype(F32)
    low = jnp.float32(-3.0e38)
    is_grp = (lane >= GROUP_LOGIT_LANE0) & (lane < GROUP_LOGIT_LANE0 + N_EXPERT_GROUPS)
    lg = jnp.where(is_grp, logits, low)
    mg = jnp.max(lg, axis=-1, keepdims=True)
    gsel = jnp.min(jnp.where(lg == mg, lanef - GROUP_LOGIT_LANE0, 1e9), axis=-1, keepdims=True)
    pg_sel = 1.0 / jnp.sum(jnp.exp(lg - mg), axis=-1, keepdims=True)
    in_grp = (lane < N_EXPERTS) & ((lane // EXPERTS_PER_GROUP).astype(F32) == gsel)
    le = jnp.where(in_grp, logits, low)
    m1 = jnp.max(le, axis=-1, keepdims=True)
    i1 = jnp.min(jnp.where(le == m1, lanef, 1e9), axis=-1, keepdims=True)
    le2 = jnp.where(lanef == i1, low, le)
    m2 = jnp.max(le2, axis=-1, keepdims=True)
    i2 = jnp.min(jnp.where(le2 == m2, lanef, 1e9), axis=-1, keepdims=True)
    r = jnp.exp(m2 - m1)
    w1 = 1.0 / (1.0 + r)
    w2 = r / (1.0 + r)
    comb_ref[...] = pg_sel * (jnp.where(lanef == i1, w1, 0.0) + jnp.where(lanef == i2, w2, 0.0))


def _merge_call(x, mod, y, attn_outs, p, l, tm, tiles_per_seq):
    ntok = x.shape[0]
    tok = lambda w: pl.BlockSpec((tm, w), lambda i: (i, 0))
    lay = lambda *shape: pl.BlockSpec((None,) + shape, lambda i: (l,) + (0,) * len(shape))
    attn_specs = [tok(GROUP_WIDTH)] * 6
    return pl.pallas_call(
        _merge_kernel,
        grid=(ntok // tm,),
        in_specs=[tok(D_MODEL), _mod_spec(mod, tm, tiles_per_seq), tok(SSM_WIDTH)] + attn_specs + [
            lay(1, D_MODEL), lay(1, D_MODEL),
            lay(D_MODEL, 2 * D_MODEL), lay(1, 2 * D_MODEL),
            lay(SSM_WIDTH, SSM_WIDTH), lay(SSM_WIDTH, D_MODEL), lay(GROUP_WIDTH, D_MODEL),
            lay(D_MODEL, D_MODEL),
            lay(D_MODEL, ROUTER_LANES), lay(D_MODEL, ROUTER_LANES), lay(1, ROUTER_LANES),
        ],
        out_specs=[tok(D_MODEL), tok(D_MODEL), tok(ROUTER_LANES)],
        out_shape=[jax.ShapeDtypeStruct((ntok, D_MODEL), F32),
                   jax.ShapeDtypeStruct((ntok, D_MODEL), BF16),
                   jax.ShapeDtypeStruct((ntok, ROUTER_LANES), F32)],
        compiler_params=_cparams(("parallel",)),
        name="merge_router",
    )(x, mod, y, *attn_outs, p['g_mix'], p['g_ffn'], p['w_mgate'], p['b_mgate'], p['w_glu'],
      p['w_br_ssm'], p['w_br_attn'], p['w_out'], p['w_router_hi'], p['w_router_lo'], p['b_router'])


def _moe_kernel(h2_ref, comb_ref, x1_ref, mod_ref, wg_ref, wu_ref, wd_ref, gfin_ref, out_ref, acc,
                *, final):
    e = pl.program_id(1)

    @pl.when(e == 0)
    def _():
        acc[...] = jnp.zeros_like(acc)

    h = h2_ref[...]
    g = _dot(h, wg_ref[...])
    hid = g * _sigmoid(g) * _dot(h, wu_ref[...])
    lane = lax.broadcasted_iota(jnp.int32, comb_ref.shape, 1)
    w = jnp.sum(jnp.where(lane == e, comb_ref[...], 0.0), axis=-1, keepdims=True)
    acc[...] += _dot((hid * w).astype(BF16), wd_ref[...])

    @pl.when(e == pl.num_programs(1) - 1)
    def _():
        x2 = x1_ref[...] + mod_ref[5] * acc[...]
        out_ref[...] = _rmsnorm(x2, gfin_ref[...]) if final else x2


def _moe_call(h2, comb, x1, mod, p, l, tm, tiles_per_seq, final):
    ntok = h2.shape[0]
    tok = lambda w: pl.BlockSpec((tm, w), lambda i, e: (i, 0))
    if mod.ndim == 4:
        mod_spec = pl.BlockSpec((None, 6, 1, D_MODEL), lambda i, e: (i // tiles_per_seq, 0, 0, 0))
    else:
        mod_spec = pl.BlockSpec((6, tm, D_MODEL), lambda i, e: (0, i, 0))
    return pl.pallas_call(
        functools.partial(_moe_kernel, final=final),
        grid=(ntok // tm, N_EXPERTS),
        in_specs=[tok(D_MODEL), tok(ROUTER_LANES), tok(D_MODEL), mod_spec,
                  pl.BlockSpec((None, None, D_MODEL, EXPERT_FF), lambda i, e: (l, e, 0, 0)),
                  pl.BlockSpec((None, None, D_MODEL, EXPERT_FF), lambda i, e: (l, e, 0, 0)),
                  pl.BlockSpec((None, None, EXPERT_FF, D_MODEL), lambda i, e: (l, e, 0, 0)),
                  pl.BlockSpec((1, D_MODEL), lambda i, e: (0, 0))],
        out_specs=tok(D_MODEL),
        out_shape=jax.ShapeDtypeStruct((ntok, D_MODEL), F32),
        scratch_shapes=[pltpu.VMEM((tm, D_MODEL), F32)],
        compiler_params=_cparams(("parallel", "arbitrary")),
        name="moe_experts",
    )(h2, comb, x1, mod, p['w_exp_gate'], p['w_exp_up'], p['w_exp_down'], p['g_final'])


def _run_trunk(x, mod_layers, p, ssm, n_seq, seq, caches, h0):
    ntok = n_seq * seq
    prompt = caches is None
    tm_proj = 512 if prompt else 256
    tm_merge = 256
    tm_moe = 1024 if prompt else 256
    states = []
    for l in range(DEPTH):
        mod = mod_layers[l]
        u, *qkv = _inproj_call(x, mod, p['g_mix'], p['w_in'], l, tm_proj, seq // tm_proj if prompt else 1)
        ab, bre, bim, cre, cim, d_skip = ssm[l]
        if prompt:
            y, hre, him = _ssm_prompt_call(u, ab, bre, bim, cre, cim, d_skip, n_seq, seq)
            hre, him = hre[:, 0], him[:, 0]
            attn = []
            for g in range(N_GROUPS):
                attn += list(_attn_prompt_call(qkv[g], n_seq, seq, DILATIONS[g]))
        else:
            y, hre, him = _ssm_sample_call(u.reshape(n_seq, seq * SSM_WIDTH), ab, bre, bim, cre, cim,
                                           d_skip, h0[0][l], h0[1][l], seq)
            y = y.reshape(ntok, SSM_WIDTH)
            attn = []
            for g in range(N_GROUPS):
                attn += list(_attn_sample_call(qkv[g], caches[2 * g], caches[2 * g + 1], l, n_seq, seq,
                                               DILATIONS[g]))
        x1, h2, comb = _merge_call(x, mod, y, attn, p, l, tm_merge, seq // tm_merge if prompt else 1)
        x = _moe_call(h2, comb, x1, mod, p, l, tm_moe, seq // tm_moe if prompt else 1,
                      final=(l == DEPTH - 1))
        kv = []
        for g in range(N_GROUPS):
            keep = min(WINDOWS[g], seq)
            for a in (1, 2):
                full = qkv[g][a].reshape(n_seq, seq, HEADS, HEAD_DIM)
                kv.append(full[:, seq - keep:])
        states.append(kv + [hre.reshape(n_seq, N_SSM_GROUPS, SSM_STATE),
                            him.reshape(n_seq, N_SSM_GROUPS, SSM_STATE)])
    new_state = [jnp.stack([states[l][i] for l in range(DEPTH)], axis=0) for i in range(len(states[0]))]
    return x, new_state


def kernel(x_prompt, x_sample, cache_k_w128, cache_v_w128, cache_k_w512, cache_v_w512,
           cache_k_w2048, cache_v_w2048, state_ssm_re, state_ssm_im, c_prompt, c_sample,
           w_ada, b_ada, g_norm_mix, g_norm_ffn, g_final, w_in, ssm_a_re, ssm_a_im, ssm_b_re,
           ssm_b_im, ssm_c_re, ssm_c_im, ssm_log_step, ssm_d, w_glu, w_br_ssm, w_br_attn,
           w_mgate, b_mgate, w_out, w_router_grp, b_router_grp, w_router_exp, b_router_exp,
           w_exp_gate, w_exp_up, w_exp_down):
    batch, seq, _ = x_prompt.shape
    dec_batch, dec_seq, _ = x_sample.shape

    n_c = batch + dec_batch
    rows = -(-n_c // SUBLANES) * SUBLANES
    c_all = jnp.concatenate([c_prompt, c_sample, jnp.zeros((rows - n_c, D_MODEL), F32)], axis=0)
    mod_all = _ada_call(c_all, w_ada, b_ada)
    mod_prompt = [mod_all[l, :batch].reshape(batch, 6, 1, D_MODEL) for l in range(DEPTH)]
    mod_sample = [jnp.repeat(mod_all[l, batch:n_c].reshape(dec_batch, 6, D_MODEL), dec_seq, axis=0)
                  .transpose(1, 0, 2) for l in range(DEPTH)]

    abr, abi, bbr, bbi = _ssm_param_call(ssm_a_re, ssm_a_im, ssm_log_step, ssm_b_re, ssm_b_im)
    ssm = []
    for l in range(DEPTH):
        pick = lambda a: a[l].reshape(N_SSM_GROUPS, SSM_STATE, SSM_GROUP_CH)
        ab = jnp.stack([pick(abr)[:, :, 0].reshape(SSM_LANES), pick(abi)[:, :, 0].reshape(SSM_LANES)])
        bre = _block_diag_halves(pick(bbr).transpose(0, 2, 1), BF16)
        bim = _block_diag_halves(pick(bbi).transpose(0, 2, 1), BF16)
        cre = _block_diag_halves(ssm_c_re[l].transpose(0, 2, 1), BF16)
        cim = _block_diag_halves(ssm_c_im[l].transpose(0, 2, 1), BF16)
        ssm.append((ab, bre, bim, cre, cim, ssm_d[l].reshape(1, SSM_WIDTH)))

    w_router = jnp.concatenate(
        [w_router_exp, w_router_grp,
         jnp.zeros((DEPTH, D_MODEL, ROUTER_LANES - N_EXPERTS - N_EXPERT_GROUPS), F32)], axis=-1)
    b_router = jnp.concatenate(
        [b_router_exp, b_router_grp,
         jnp.zeros((DEPTH, ROUTER_LANES - N_EXPERTS - N_EXPERT_GROUPS), F32)], axis=-1)
    w_router_hi = w_router.astype(BF16)
    p = dict(
        g_mix=g_norm_mix.reshape(DEPTH, 1, D_MODEL), g_ffn=g_norm_ffn.reshape(DEPTH, 1, D_MODEL),
        g_final=g_final.reshape(1, D_MODEL), w_in=w_in.astype(BF16), w_glu=w_glu.astype(BF16),
        w_br_ssm=w_br_ssm.astype(BF16), w_br_attn=w_br_attn.astype(BF16),
        w_mgate=w_mgate.astype(BF16), b_mgate=b_mgate.reshape(DEPTH, 1, 2 * D_MODEL),
        w_out=w_out.astype(BF16), w_router_hi=w_router_hi,
        w_router_lo=(w_router - w_router_hi.astype(F32)).astype(BF16),
        b_router=b_router.reshape(DEPTH, 1, ROUTER_LANES),
        w_exp_gate=w_exp_gate.astype(BF16), w_exp_up=w_exp_up.astype(BF16),
        w_exp_down=w_exp_down.astype(BF16))

    y_prompt, pstate = _run_trunk(x_prompt.reshape(batch * seq, D_MODEL), mod_prompt, p, ssm,
                                  batch, seq, None, None)
    caches = [cache_k_w128, cache_v_w128, cache_k_w512, cache_v_w512, cache_k_w2048, cache_v_w2048]
    h0 = (state_ssm_re.reshape(DEPTH, dec_batch, SSM_LANES), state_ssm_im.reshape(DEPTH, dec_batch, SSM_LANES))
    y_sample, sstate = _run_trunk(x_sample.reshape(dec_batch * dec_seq, D_MODEL), mod_sample, p, ssm,
                                  dec_batch, dec_seq, caches, h0)
    return (y_prompt.reshape(batch, seq, D_MODEL), y_sample.reshape(dec_batch, dec_seq, D_MODEL),
            *pstate, *sstate)
```

```python
import functools
import math

import jax
import jax.numpy as jnp
from jax import lax
from jax.experimental import pallas as pl
from jax.experimental.pallas import tpu as pltpu

F32 = jnp.float32
BF16 = jnp.bfloat16

D_MODEL = 1024
DEPTH = 2
SSM_WIDTH = 512
SSM_GROUP_CH = 16
N_SSM_GROUPS = 32
SSM_STATE = 64
SSM_LANES = N_SSM_GROUPS * SSM_STATE
HEAD_DIM = 64
HEADS = 4
GROUP_WIDTH = HEADS * HEAD_DIM
PAIR_WIDTH = 2 * HEAD_DIM
WINDOWS = (128, 512, 2048)
DILATIONS = (1, 4, 16)
N_BACK = 128
N_GROUPS = 3
QKV_WIDTH = N_GROUPS * GROUP_WIDTH
IN_PROJ_WIDTH = SSM_WIDTH + 3 * QKV_WIDTH
N_EXPERT_GROUPS = 4
EXPERTS_PER_GROUP = 8
N_EXPERTS = 32
EXPERT_FF = 256
RMS_EPS = 1e-6
NEG_INF = -1e30
ROUTER_LANES = 128
GROUP_LOGIT_LANE0 = N_EXPERTS

SUBLANES = 8
VMEM_LIMIT = 56 * 1024 * 1024


def _cparams(sem):
    return pltpu.CompilerParams(dimension_semantics=sem, vmem_limit_bytes=VMEM_LIMIT)


def _sigmoid(x):
    return 1.0 / (1.0 + jnp.exp(-x))


def _gelu_tanh(x):
    return 0.5 * x * (1.0 + jnp.tanh(math.sqrt(2.0 / math.pi) * (x + 0.044715 * (x * x * x))))


def _dot(a, b):
    return jnp.dot(a, b, preferred_element_type=F32)


def _rmsnorm(x, g):
    ms = jnp.mean(x * x, axis=-1, keepdims=True)
    return x * lax.rsqrt(ms + RMS_EPS) * g


def _ada_kernel(c_ref, w_ref, b_ref, o_ref):
    c = c_ref[...]
    s = (c * _sigmoid(c)).astype(BF16)
    o_ref[...] = _dot(s, w_ref[...].astype(BF16)) + b_ref[...]


def _ada_call(c_all, w_ada, b_ada):
    rows = c_all.shape[0]
    tn = 1536
    return pl.pallas_call(
        _ada_kernel,
        grid=(DEPTH, 6 * D_MODEL // tn),
        in_specs=[
            pl.BlockSpec((rows, D_MODEL), lambda l, j: (0, 0)),
            pl.BlockSpec((None, D_MODEL, tn), lambda l, j: (l, 0, j)),
            pl.BlockSpec((None, 1, tn), lambda l, j: (l, 0, j)),
        ],
        out_specs=pl.BlockSpec((None, rows, tn), lambda l, j: (l, 0, j)),
        out_shape=jax.ShapeDtypeStruct((DEPTH, rows, 6 * D_MODEL), F32),
        compiler_params=_cparams(("parallel", "parallel")),
        name="ada_mod",
    )(c_all, w_ada, b_ada.reshape(DEPTH, 1, 6 * D_MODEL))


def _ssm_param_kernel(ar_ref, ai_ref, ls_ref, br_ref, bi_ref, abr_ref, abi_ref, bbr_ref, bbi_ref):
    lr, li = ar_ref[...], ai_ref[...]
    dt = jnp.exp(ls_ref[...])
    mag = jnp.exp(lr * dt)
    abr = mag * jnp.cos(li * dt)
    abi = mag * jnp.sin(li * dt)
    den = lr * lr + li * li
    er, ei = abr - 1.0, abi
    fr = (er * lr + ei * li) / den
    fi = (ei * lr - er * li) / den
    br, bi = br_ref[...], bi_ref[...]
    abr_ref[...] = abr
    abi_ref[...] = abi
    bbr_ref[...] = fr * br - fi * bi
    bbi_ref[...] = fr * bi + fi * br


def _ssm_param_call(a_re, a_im, log_step, b_re, b_im):
    w = SSM_STATE * SSM_GROUP_CH
    rep = lambda a: jnp.repeat(a, SSM_GROUP_CH, axis=-1)
    big = pl.BlockSpec((None, N_SSM_GROUPS, w), lambda l: (l, 0, 0))
    shp = jax.ShapeDtypeStruct((DEPTH, N_SSM_GROUPS, w), F32)
    return pl.pallas_call(
        _ssm_param_kernel,
        grid=(DEPTH,),
        in_specs=[big, big, pl.BlockSpec((None, N_SSM_GROUPS, 1), lambda l: (l, 0, 0)), big, big],
        out_specs=[big, big, big, big],
        out_shape=[shp, shp, shp, shp],
        compiler_params=_cparams(("parallel",)),
        name="ssm_params",
    )(rep(a_re), rep(a_im), log_step.reshape(DEPTH, N_SSM_GROUPS, 1),
      b_re.reshape(DEPTH, N_SSM_GROUPS, w), b_im.reshape(DEPTH, N_SSM_GROUPS, w))


def _block_diag_halves(blocks, dtype):
    g, r, c = blocks.shape
    half = g // 2
    eye = jnp.eye(half, dtype=bool)

    def one(b):
        m = jnp.where(eye[:, None, :, None], b[:, :, None, :], 0.0)
        return m.reshape(half * r, half * c)

    return jnp.stack([one(blocks[:half]), one(blocks[half:])]).astype(dtype)


def _inproj_kernel(x_ref, mod_ref, g_ref, w_ref, u_ref, a0_ref, a1_ref, a2_ref):
    h = _rmsnorm(x_ref[...], g_ref[...]) * (1.0 + mod_ref[1]) + mod_ref[0]
    hb = h.astype(BF16)
    u_ref[...] = _dot(hb, w_ref[:, :SSM_WIDTH])
    for g, ref in enumerate((a0_ref, a1_ref, a2_ref)):
        for a in range(3):
            c0 = SSM_WIDTH + a * QKV_WIDTH + g * GROUP_WIDTH
            val = _dot(hb, w_ref[:, c0:c0 + GROUP_WIDTH])
            ref[a, 0] = val[:, :PAIR_WIDTH]
            ref[a, 1] = val[:, PAIR_WIDTH:]


def _mod_spec(mod, tm, tiles_per_seq):
    if mod.ndim == 4:
        return pl.BlockSpec((None, 6, 1, D_MODEL), lambda i: (i // tiles_per_seq, 0, 0, 0))
    return pl.BlockSpec((6, tm, D_MODEL), lambda i: (0, i, 0))


def _inproj_call(x, mod, g_mix, w_in, l, tm, tiles_per_seq):
    ntok = x.shape[0]
    grp = jax.ShapeDtypeStruct((3, 2, ntok, PAIR_WIDTH), F32)
    grp_spec = pl.BlockSpec((3, 2, tm, PAIR_WIDTH), lambda i: (0, 0, i, 0))
    return pl.pallas_call(
        _inproj_kernel,
        grid=(ntok // tm,),
        in_specs=[
            pl.BlockSpec((tm, D_MODEL), lambda i: (i, 0)),
            _mod_spec(mod, tm, tiles_per_seq),
            pl.BlockSpec((None, 1, D_MODEL), lambda i: (l, 0, 0)),
            pl.BlockSpec((None, D_MODEL, IN_PROJ_WIDTH), lambda i: (l, 0, 0)),
        ],
        out_specs=[pl.BlockSpec((tm, SSM_WIDTH), lambda i: (i, 0)), grp_spec, grp_spec, grp_spec],
        out_shape=[jax.ShapeDtypeStruct((ntok, SSM_WIDTH), F32), grp, grp, grp],
        compiler_params=_cparams(("parallel",)),
        name="in_proj",
    )(x, mod, g_mix, w_in)


SCAN_LANES = 256
SCAN_CHUNK = 256


def _cmul(ar, ai, br, bi):
    return ar * br - ai * bi, ar * bi + ai * br


def _ssm_prompt_kernel(u_ref, ab_ref, bre_ref, bim_ref, cre_ref, cim_ref, d_ref,
                       y_ref, hre_ref, him_ref, bur, bui, tab, carry):
    c = pl.program_id(1)
    half = SSM_LANES // 2

    @pl.when(c == 0)
    def _():
        shp = (SUBLANES, SSM_LANES)
        sub = lax.broadcasted_iota(jnp.int32, shp, 0)
        p_r = jnp.broadcast_to(ab_ref[0:1, :], shp)
        p_i = jnp.broadcast_to(ab_ref[1:2, :], shp)
        a_r, a_i = p_r, p_i
        pw_r, pw_i = p_r, p_i
        for s in range(1, SUBLANES + 1):
            if s in (1, 2, 4):
                k = {1: 0, 2: 2, 4: 4}[s]
                tab[k] = jnp.where(sub >= s, p_r, 0.0)
                tab[k + 1] = jnp.where(sub >= s, p_i, 0.0)
            pw_r = jnp.where(sub >= s - 1, p_r, pw_r)
            pw_i = jnp.where(sub >= s - 1, p_i, pw_i)
            p_r, p_i = _cmul(p_r, p_i, a_r, a_i)
        tab[6] = pw_r
        tab[7] = pw_i
        carry[...] = jnp.zeros_like(carry)

    u = u_ref[...]
    ub = u.astype(BF16)
    for h in range(2):
        uh = ub[:, h * 256:(h + 1) * 256]
        bur[:, h * half:(h + 1) * half] = _dot(uh, bre_ref[h])
        bui[:, h * half:(h + 1) * half] = _dot(uh, bim_ref[h])

    n_blocks = SCAN_CHUNK // SUBLANES
    for lc in range(SSM_LANES // SCAN_LANES):
        sl = slice(lc * SCAN_LANES, (lc + 1) * SCAN_LANES)
        steps = [(tab[0, :, sl], tab[1, :, sl], 1), (tab[2, :, sl], tab[3, :, sl], 2),
                 (tab[4, :, sl], tab[5, :, sl], 4)]
        pw_r, pw_i = tab[6, :, sl], tab[7, :, sl]

        def body(b, cvals, sl=sl, steps=steps, pw_r=pw_r, pw_i=pw_i):
            cr, ci = cvals
            r0 = pl.multiple_of(b * SUBLANES, SUBLANES)
            xr = bur[pl.ds(r0, SUBLANES), sl]
            xi = bui[pl.ds(r0, SUBLANES), sl]
            for ar, ai, k in steps:
                rr = pltpu.roll(xr, k, axis=0)
                ri = pltpu.roll(xi, k, axis=0)
                xr, xi = xr + (ar * rr - ai * ri), xi + (ar * ri + ai * rr)
            hr = xr + (pw_r * cr - pw_i * ci)
            hi = xi + (pw_r * ci + pw_i * cr)
            bur[pl.ds(r0, SUBLANES), sl] = hr
            bui[pl.ds(r0, SUBLANES), sl] = hi
            shp = (SUBLANES, SCAN_LANES)
            return (jnp.broadcast_to(hr[SUBLANES - 1:SUBLANES, :], shp),
                    jnp.broadcast_to(hi[SUBLANES - 1:SUBLANES, :], shp))

        cr, ci = lax.fori_loop(0, n_blocks, body, (carry[0, :, sl], carry[1, :, sl]), unroll=2)
        carry[0, :, sl] = cr
        carry[1, :, sl] = ci

    for h in range(2):
        hs = slice(h * half, (h + 1) * half)
        cs = slice(h * 256, (h + 1) * 256)
        y = _dot(bur[:, hs].astype(BF16), cre_ref[h]) - _dot(bui[:, hs].astype(BF16), cim_ref[h])
        y_ref[:, cs] = y + d_ref[:, cs] * u[:, cs]

    @pl.when(c == pl.num_programs(1) - 1)
    def _():
        hre_ref[...] = carry[0]
        him_ref[...] = carry[1]


def _ssm_prompt_call(u, ab, bre, bim, cre, cim, d_skip, n_seq, seq):
    chunks = seq // SCAN_CHUNK
    half = SSM_LANES // 2
    wspec_b = pl.BlockSpec((2, 256, half), lambda n, c: (0, 0, 0))
    wspec_c = pl.BlockSpec((2, half, 256), lambda n, c: (0, 0, 0))
    st_spec = pl.BlockSpec((None, SUBLANES, SSM_LANES), lambda n, c: (n, 0, 0))
    st_shape = jax.ShapeDtypeStruct((n_seq, SUBLANES, SSM_LANES), F32)
    return pl.pallas_call(
        _ssm_prompt_kernel,
        grid=(n_seq, chunks),
        in_specs=[
            pl.BlockSpec((SCAN_CHUNK, SSM_WIDTH), lambda n, c: (n * chunks + c, 0)),
            pl.BlockSpec((2, SSM_LANES), lambda n, c: (0, 0)),
            wspec_b, wspec_b, wspec_c, wspec_c,
            pl.BlockSpec((1, SSM_WIDTH), lambda n, c: (0, 0)),
        ],
        out_specs=[pl.BlockSpec((SCAN_CHUNK, SSM_WIDTH), lambda n, c: (n * chunks + c, 0)),
                   st_spec, st_spec],
        out_shape=[jax.ShapeDtypeStruct((n_seq * seq, SSM_WIDTH), F32), st_shape, st_shape],
        scratch_shapes=[pltpu.VMEM((SCAN_CHUNK, SSM_LANES), F32),
                        pltpu.VMEM((SCAN_CHUNK, SSM_LANES), F32),
                        pltpu.VMEM((8, SUBLANES, SSM_LANES), F32),
                        pltpu.VMEM((2, SUBLANES, SSM_LANES), F32)],
        compiler_params=_cparams(("parallel", "arbitrary")),
        name="ssm_prompt",
    )(u, ab, bre, bim, cre, cim, d_skip)


def _ssm_sample_kernel(u_ref, ab_ref, bre_ref, bim_ref, cre_ref, cim_ref, d_ref, h0r_ref, h0i_ref,
                       y_ref, hr_ref, hi_ref, *, steps):
    half = SSM_LANES // 2
    ar, ai = ab_ref[0:1, :], ab_ref[1:2, :]
    hr_ref[...] = h0r_ref[...]
    hi_ref[...] = h0i_ref[...]
    for t in range(steps):
        u = u_ref[:, t * SSM_WIDTH:(t + 1) * SSM_WIDTH]
        ub = u.astype(BF16)
        for h in range(2):
            hs = slice(h * half, (h + 1) * half)
            cs = slice(h * 256, (h + 1) * 256)
            uh = ub[:, cs]
            hr, hi = hr_ref[:, hs], hi_ref[:, hs]
            a_r, a_i = ar[:, hs], ai[:, hs]
            nr = (a_r * hr - a_i * hi) + _dot(uh, bre_ref[h])
            ni = (a_r * hi + a_i * hr) + _dot(uh, bim_ref[h])
            hr_ref[:, hs] = nr
            hi_ref[:, hs] = ni
            y = _dot(nr.astype(BF16), cre_ref[h]) - _dot(ni.astype(BF16), cim_ref[h])
            y_ref[:, t * SSM_WIDTH + h * 256:t * SSM_WIDTH + (h + 1) * 256] = y + d_ref[:, cs] * u[:, cs]


def _ssm_sample_call(u, ab, bre, bim, cre, cim, d_skip, h0r, h0i, steps):
    n = u.shape[0]
    st = jax.ShapeDtypeStruct((n, SSM_LANES), F32)
    return pl.pallas_call(
        functools.partial(_ssm_sample_kernel, steps=steps),
        out_shape=[jax.ShapeDtypeStruct((n, steps * SSM_WIDTH), F32), st, st],
        compiler_params=pltpu.CompilerParams(vmem_limit_bytes=VMEM_LIMIT),
        name="ssm_sample",
    )(u, ab, bre, bim, cre, cim, d_skip, h0r, h0i)


_NT = (((1,), (1,)), ((), ()))


def _attn_prompt_kernel(q_ref, kp_ref, kc_ref, vp_ref, vc_ref, o_ref, lse_ref, *, dil):
    b = pl.program_id(2)
    qi = lax.broadcasted_iota(jnp.int32, (N_BACK, 2 * N_BACK), 0)
    kj = lax.broadcasted_iota(jnp.int32, (N_BACK, 2 * N_BACK), 1)
    first_real = jnp.where(b > 0, 0, N_BACK)
    valid = (kj >= qi) & (kj <= qi + N_BACK) & (kj >= first_real)
    lane = lax.broadcasted_iota(jnp.int32, (N_BACK, PAIR_WIDTH), 1)
    head0 = lane < HEAD_DIM

    def residue(r, carry):
        rows = pl.ds(r, N_BACK, stride=dil) if dil > 1 else pl.ds(0, N_BACK)
        q = (q_ref[rows, :] * (HEAD_DIM ** -0.5)).astype(BF16)
        kk = jnp.concatenate([kp_ref[rows, :], kc_ref[rows, :]], axis=0).astype(BF16)
        vv = jnp.concatenate([vp_ref[rows, :], vc_ref[rows, :]], axis=0).astype(BF16)
        outs, lses = [], []
        for sub in range(2):
            in_head = head0 if sub == 0 else jnp.logical_not(head0)
            qm = jnp.where(in_head, q, jnp.zeros_like(q))
            s = lax.dot_general(qm, kk, _NT, preferred_element_type=F32)
            s = jnp.where(valid, s, NEG_INF)
            m = jnp.max(s, axis=-1, keepdims=True)
            p = jnp.exp(s - m)
            den = jnp.sum(p, axis=-1, keepdims=True)
            outs.append(_dot(p.astype(BF16), vv) / den)
            lses.append(jnp.broadcast_to(m + jnp.log(den), (N_BACK, PAIR_WIDTH)))
        o_ref[rows, :] = jnp.where(head0, outs[0], outs[1])
        lse_ref[rows, :] = jnp.where(head0, lses[0], lses[1])
        return carry

    if dil == 1:
        residue(0, 0)
    else:
        lax.fori_loop(0, dil, residue, 0)


def _attn_prompt_call(qkv, n_seq, seq, dil):
    rows = N_BACK * dil
    nbk = seq // rows
    blk = (None, None, rows, PAIR_WIDTH)
    cur = lambda a: pl.BlockSpec(blk, lambda n, hp, b: (a, hp, n * nbk + b, 0))
    prev = lambda a: pl.BlockSpec(blk, lambda n, hp, b: (a, hp, n * nbk + jnp.maximum(b - 1, 0), 0))
    out_spec = pl.BlockSpec((None, rows, PAIR_WIDTH), lambda n, hp, b: (hp, n * nbk + b, 0))
    out_shape = jax.ShapeDtypeStruct((2, n_seq * seq, PAIR_WIDTH), F32)
    return pl.pallas_call(
        functools.partial(_attn_prompt_kernel, dil=dil),
        grid=(n_seq, 2, nbk),
        in_specs=[cur(0), prev(1), cur(1), prev(2), cur(2)],
        out_specs=[out_spec, out_spec],
        out_shape=[out_shape, out_shape],
        compiler_params=_cparams(("parallel", "parallel", "parallel")),
        name=f"attn_prompt_d{dil}",
    )(qkv, qkv, qkv, qkv, qkv)


def _kv_tail_kernel(x_ref, o_ref):
    o_ref[...] = x_ref[...].T


def _kv_tail_call(qkv, n_seq, seq, keep):
    blocks_per_seq = seq // keep
    return pl.pallas_call(
        _kv_tail_kernel,
        grid=(2, n_seq, 2),
        in_specs=[pl.BlockSpec((None, None, keep, PAIR_WIDTH),
                               lambda a, n, hp: (a + 1, hp, (n + 1) * blocks_per_seq - 1, 0))],
        out_specs=pl.BlockSpec((None, None, None, PAIR_WIDTH, keep), lambda a, n, hp: (a, n, hp, 0, 0)),
        out_shape=jax.ShapeDtypeStruct((2, n_seq, 2, PAIR_WIDTH, keep), F32),
        compiler_params=_cparams(("parallel", "parallel", "parallel")),
        name="kv_tail",
    )(qkv)


SAMPLE_SEQ_BLOCK = {128: 4, 512: 4, 2048: 2}


def _attn_sample_kernel(qkv_ref, kt_ref, vt_ref, o_ref, lse_ref, *, dil, steps):
    nb, _, _, w = kt_ref.shape
    rows = nb * steps
    row = lax.broadcasted_iota(jnp.int32, (rows, w), 0)
    pos = lax.broadcasted_iota(jnp.int32, (rows, w), 1)
    rown = lax.broadcasted_iota(jnp.int32, (rows, rows), 0)
    coln = lax.broadcasted_iota(jnp.int32, (rows, rows), 1)
    rowo = lax.broadcasted_iota(jnp.int32, (rows, HEAD_DIM), 0)
    for hp in range(2):
        q2 = qkv_ref[0, hp] * (HEAD_DIM ** -0.5)
        k2 = qkv_ref[1, hp]
        v2 = qkv_ref[2, hp]
        o_heads, lse_heads = [], []
        for sub in range(2):
            ls = slice(sub * HEAD_DIM, (sub + 1) * HEAD_DIM)
            h = hp * 2 + sub
            q = q2[:, ls].astype(BF16)
            kn = k2[:, ls].astype(BF16)
            vn = v2[:, ls].astype(BF16)
            o_h = jnp.zeros((rows, HEAD_DIM), F32)
            lse_h = jnp.zeros((rows, HEAD_DIM), F32)
            for n in range(nb):
                t = row - n * steps
                in_buf = (((pos - t) & (dil - 1)) == 0) & (pos >= t)
                tn = rown - n * steps
                sn = coln - n * steps
                in_new = ((sn >= 0) & (sn <= tn)) if dil == 1 else (sn == tn)
                s = jnp.where(in_buf, _dot(q, kt_ref[n, h].astype(BF16)), NEG_INF)
                s2 = jnp.where(in_new, lax.dot_general(q, kn, _NT, preferred_element_type=F32), NEG_INF)
                m = jnp.maximum(jnp.max(s, axis=-1, keepdims=True), jnp.max(s2, axis=-1, keepdims=True))
                p = jnp.exp(s - m)
                p2 = jnp.exp(s2 - m)
                den = jnp.sum(p, axis=-1, keepdims=True) + jnp.sum(p2, axis=-1, keepdims=True)
                o = lax.dot_general(p.astype(BF16), vt_ref[n, h].astype(BF16), _NT,
                                    preferred_element_type=F32)
                o = (o + _dot(p2.astype(BF16), vn)) / den
                mine = (rowo >= n * steps) & (rowo < (n + 1) * steps)
                o_h = jnp.where(mine, o, o_h)
                lse_h = jnp.where(mine, m + jnp.log(den), lse_h)
            o_heads.append(o_h)
            lse_heads.append(lse_h)
        o_ref[hp] = jnp.concatenate(o_heads, axis=-1)
        lse_ref[hp] = jnp.concatenate(lse_heads, axis=-1)


def _attn_sample_call(qkv, k_cache, v_cache, l, n_seq, steps, dil):
    w = k_cache.shape[2]
    kt = k_cache.transpose(0, 1, 3, 4, 2)
    vt = v_cache.transpose(0, 1, 3, 4, 2)
    nb = SAMPLE_SEQ_BLOCK[w]
    rows = nb * steps
    cache_spec = pl.BlockSpec((None, nb, HEADS, HEAD_DIM, w), lambda i: (l, i, 0, 0, 0))
    out_spec = pl.BlockSpec((2, rows, PAIR_WIDTH), lambda i: (0, i, 0))
    out_shape = jax.ShapeDtypeStruct((2, n_seq * steps, PAIR_WIDTH), F32)
    return pl.pallas_call(
        functools.partial(_attn_sample_kernel, dil=dil, steps=steps),
        grid=(n_seq // nb,),
        in_specs=[pl.BlockSpec((3, 2, rows, PAIR_WIDTH), lambda i: (0, 0, i, 0)), cache_spec, cache_spec],
        out_specs=[out_spec, out_spec],
        out_shape=[out_shape, out_shape],
        compiler_params=_cparams(("parallel",)),
        name=f"attn_sample_d{dil}",
    )(qkv, kt, vt)


def _merge_kernel(x_ref, mod_ref, y_ref, o0_ref, l0_ref, o1_ref, l1_ref, o2_ref, l2_ref,
                  gmix_ref, gffn_ref, wmg_ref, bmg_ref, wglu_ref, wbs_ref, wba_ref, wout_ref,
                  wrh_ref, wrl_ref, br_ref, x1_ref, h2_ref, comb_ref):
    x = x_ref[...]
    tm = x.shape[0]
    hb = (_rmsnorm(x, gmix_ref[...]) * (1.0 + mod_ref[1]) + mod_ref[0]).astype(BF16)

    y = _gelu_tanh(y_ref[...])
    yg = y * _sigmoid(_dot(y.astype(BF16), wglu_ref[...]))
    ssm_branch = _dot(yg.astype(BF16), wbs_ref[...])

    pairs = []
    for hp in range(2):
        l0, l1, l2 = l0_ref[hp], l1_ref[hp], l2_ref[hp]
        lm = jnp.maximum(jnp.maximum(l0, l1), l2)
        e0, e1, e2 = jnp.exp(l0 - lm), jnp.exp(l1 - lm), jnp.exp(l2 - lm)
        pairs.append((e0 * o0_ref[hp] + e1 * o1_ref[hp] + e2 * o2_ref[hp]) / (e0 + e1 + e2))
    attn = jnp.concatenate(pairs, axis=-1)
    attn_branch = _dot(attn.astype(BF16), wba_ref[...])

    g_ssm = _sigmoid(_dot(hb, wmg_ref[:, :D_MODEL]) + bmg_ref[:, :D_MODEL])
    g_attn = _sigmoid(_dot(hb, wmg_ref[:, D_MODEL:]) + bmg_ref[:, D_MODEL:])
    merged = g_ssm * ssm_branch + g_attn * attn_branch
    x1 = x + mod_ref[2] * _dot(merged.astype(BF16), wout_ref[...])
    x1_ref[...] = x1

    h2 = _rmsnorm(x1, gffn_ref[...]) * (1.0 + mod_ref[4]) + mod_ref[3]
    h2_hi = h2.astype(BF16)
    h2_ref[...] = h2_hi
    h2_lo = (h2 - h2_hi.astype(F32)).astype(BF16)
    logits = (_dot(h2_hi, wrh_ref[...]) + _dot(h2_hi, wrl_ref[...]) + _dot(h2_lo, wrh_ref[...])
              + br_ref[...])

    lane = lax.broadcasted_iota(jnp.int32, (tm, ROUTER_LANES), 1)
    lanef = lane.astype(F32)
    low = jnp.float32(-3.0e38)
    is_grp = (lane >= GROUP_LOGIT_LANE0) & (lane < GROUP_LOGIT_LANE0 + N_EXPERT_GROUPS)
    lg = jnp.where(is_grp, logits, low)
    mg = jnp.max(lg, axis=-1, keepdims=True)
    gsel = jnp.min(jnp.where(lg == mg, lanef - GROUP_LOGIT_LANE0, 1e9), axis=-1, keepdims=True)
    pg_sel = 1.0 / jnp.sum(jnp.exp(lg - mg), axis=-1, keepdims=True)
    in_grp = (lane < N_EXPERTS) & ((lane // EXPERTS_PER_GROUP).astype(F32) == gsel)
    le = jnp.where(in_grp, logits, low)
    m1 = jnp.max(le, axis=-1, keepdims=True)
    i1 = jnp.min(jnp.where(le == m1, lanef, 1e9), axis=-1, keepdims=True)
    le2 = jnp.where(lanef == i1, low, le)
    m2 = jnp.max(le2, axis=-1, keepdims=True)
    i2 = jnp.min(jnp.where(le2 == m2, lanef, 1e9), axis=-1, keepdims=True)
    r = jnp.exp(m2 - m1)
    w1 = 1.0 / (1.0 + r)
    w2 = r / (1.0 + r)
    comb_ref[...] = pg_sel * (jnp.where(lanef == i1, w1, 0.0) + jnp.where(lanef == i2, w2, 0.0))


def _merge_call(x, mod, y, attn_outs, p, l, tm, tiles_per_seq):
    ntok = x.shape[0]
    tok = lambda w: pl.BlockSpec((tm, w), lambda i: (i, 0))
    lay = lambda *shape: pl.BlockSpec((None,) + shape, lambda i: (l,) + (0,) * len(shape))
    attn_specs = [pl.BlockSpec((2, tm, PAIR_WIDTH), lambda i: (0, i, 0))] * 6
    return pl.pallas_call(
        _merge_kernel,
        grid=(ntok // tm,),
        in_specs=[tok(D_MODEL), _mod_spec(mod, tm, tiles_per_seq), tok(SSM_WIDTH)] + attn_specs + [
            lay(1, D_MODEL), lay(1, D_MODEL),
            lay(D_MODEL, 2 * D_MODEL), lay(1, 2 * D_MODEL),
            lay(SSM_WIDTH, SSM_WIDTH), lay(SSM_WIDTH, D_MODEL), lay(GROUP_WIDTH, D_MODEL),
            lay(D_MODEL, D_MODEL),
            lay(D_MODEL, ROUTER_LANES), lay(D_MODEL, ROUTER_LANES), lay(1, ROUTER_LANES),
        ],
        out_specs=[tok(D_MODEL), tok(D_MODEL), tok(ROUTER_LANES)],
        out_shape=[jax.ShapeDtypeStruct((ntok, D_MODEL), F32),
                   jax.ShapeDtypeStruct((ntok, D_MODEL), BF16),
                   jax.ShapeDtypeStruct((ntok, ROUTER_LANES), F32)],
        compiler_params=_cparams(("parallel",)),
        name="merge_router",
    )(x, mod, y, *attn_outs, p['g_mix'], p['g_ffn'], p['w_mgate'], p['b_mgate'], p['w_glu'],
      p['w_br_ssm'], p['w_br_attn'], p['w_out'], p['w_router_hi'], p['w_router_lo'], p['b_router'])


def _moe_kernel(h2_ref, comb_ref, x1_ref, mod_ref, wg_ref, wu_ref, wd_ref, gfin_ref, out_ref, acc,
                *, final):
    e = pl.program_id(1)

    @pl.when(e == 0)
    def _():
        acc[...] = jnp.zeros_like(acc)

    h = h2_ref[...]
    g = _dot(h, wg_ref[...])
    hid = g * _sigmoid(g) * _dot(h, wu_ref[...])
    lane = lax.broadcasted_iota(jnp.int32, comb_ref.shape, 1)
    w = jnp.sum(jnp.where(lane == e, comb_ref[...], 0.0), axis=-1, keepdims=True)
    acc[...] += _dot((hid * w).astype(BF16), wd_ref[...])

    @pl.when(e == pl.num_programs(1) - 1)
    def _():
        x2 = x1_ref[...] + mod_ref[5] * acc[...]
        out_ref[...] = _rmsnorm(x2, gfin_ref[...]) if final else x2


def _moe_call(h2, comb, x1, mod, p, l, tm, tiles_per_seq, final):
    ntok = h2.shape[0]
    tok = lambda w: pl.BlockSpec((tm, w), lambda i, e: (i, 0))
    if mod.ndim == 4:
        mod_spec = pl.BlockSpec((None, 6, 1, D_MODEL), lambda i, e: (i // tiles_per_seq, 0, 0, 0))
    else:
        mod_spec = pl.BlockSpec((6, tm, D_MODEL), lambda i, e: (0, i, 0))
    return pl.pallas_call(
        functools.partial(_moe_kernel, final=final),
        grid=(ntok // tm, N_EXPERTS),
        in_specs=[tok(D_MODEL), tok(ROUTER_LANES), tok(D_MODEL), mod_spec,
                  pl.BlockSpec((None, None, D_MODEL, EXPERT_FF), lambda i, e: (l, e, 0, 0)),
                  pl.BlockSpec((None, None, D_MODEL, EXPERT_FF), lambda i, e: (l, e, 0, 0)),
                  pl.BlockSpec((None, None, EXPERT_FF, D_MODEL), lambda i, e: (l, e, 0, 0)),
                  pl.BlockSpec((1, D_MODEL), lambda i, e: (0, 0))],
        out_specs=tok(D_MODEL),
        out_shape=jax.ShapeDtypeStruct((ntok, D_MODEL), F32),
        scratch_shapes=[pltpu.VMEM((tm, D_MODEL), F32)],
        compiler_params=_cparams(("parallel", "arbitrary")),
        name="moe_experts",
    )(h2, comb, x1, mod, p['w_exp_gate'], p['w_exp_up'], p['w_exp_down'], p['g_final'])


def _run_trunk(x, mod_layers, p, ssm, n_seq, seq, caches, h0):
    ntok = n_seq * seq
    prompt = caches is None
    tm_proj = 512 if prompt else 256
    tm_merge = 256
    tm_moe = 1024 if prompt else 256
    states = []
    for l in range(DEPTH):
        mod = mod_layers[l]
        u, *qkv = _inproj_call(x, mod, p['g_mix'], p['w_in'], l, tm_proj, seq // tm_proj if prompt else 1)
        ab, bre, bim, cre, cim, d_skip = ssm[l]
        if prompt:
            y, hre, him = _ssm_prompt_call(u, ab, bre, bim, cre, cim, d_skip, n_seq, seq)
            hre, him = hre[:, 0], him[:, 0]
            attn = []
            for g in range(N_GROUPS):
                attn += list(_attn_prompt_call(qkv[g], n_seq, seq, DILATIONS[g]))
        else:
            y, hre, him = _ssm_sample_call(u.reshape(n_seq, seq * SSM_WIDTH), ab, bre, bim, cre, cim,
                                           d_skip, h0[0][l], h0[1][l], seq)
            y = y.reshape(ntok, SSM_WIDTH)
            attn = []
            for g in range(N_GROUPS):
                attn += list(_attn_sample_call(qkv[g], caches[2 * g], caches[2 * g + 1], l, n_seq, seq,
                                               DILATIONS[g]))
        x1, h2, comb = _merge_call(x, mod, y, attn, p, l, tm_merge, seq // tm_merge if prompt else 1)
        x = _moe_call(h2, comb, x1, mod, p, l, tm_moe, seq // tm_moe if prompt else 1,
                      final=(l == DEPTH - 1))
        kv = []
        for g in range(N_GROUPS):
            keep = min(WINDOWS[g], seq)
            if prompt:
                tails = _kv_tail_call(qkv[g], n_seq, seq, keep).reshape(2, n_seq, HEADS, HEAD_DIM, keep)
                kv += [tails[0].transpose(0, 3, 1, 2), tails[1].transpose(0, 3, 1, 2)]
            else:
                for a in (1, 2):
                    new = qkv[g][a].reshape(2, n_seq, seq, 2, HEAD_DIM)
                    kv.append(new.transpose(1, 2, 0, 3, 4).reshape(n_seq, seq, HEADS, HEAD_DIM))
        states.append(kv + [hre.reshape(n_seq, N_SSM_GROUPS, SSM_STATE),
                            him.reshape(n_seq, N_SSM_GROUPS, SSM_STATE)])
    new_state = [jnp.stack([states[l][i] for l in range(DEPTH)], axis=0) for i in range(len(states[0]))]
    return x, new_state


def kernel(x_prompt, x_sample, cache_k_w128, cache_v_w128, cache_k_w512, cache_v_w512,
           cache_k_w2048, cache_v_w2048, state_ssm_re, state_ssm_im, c_prompt, c_sample,
           w_ada, b_ada, g_norm_mix, g_norm_ffn, g_final, w_in, ssm_a_re, ssm_a_im, ssm_b_re,
           ssm_b_im, ssm_c_re, ssm_c_im, ssm_log_step, ssm_d, w_glu, w_br_ssm, w_br_attn,
           w_mgate, b_mgate, w_out, w_router_grp, b_router_grp, w_router_exp, b_router_exp,
           w_exp_gate, w_exp_up, w_exp_down):
    batch, seq, _ = x_prompt.shape
    dec_batch, dec_seq, _ = x_sample.shape

    n_c = batch + dec_batch
    rows = -(-n_c // SUBLANES) * SUBLANES
    c_all = jnp.concatenate([c_prompt, c_sample, jnp.zeros((rows - n_c, D_MODEL), F32)], axis=0)
    mod_all = _ada_call(c_all, w_ada, b_ada)
    mod_prompt = [mod_all[l, :batch].reshape(batch, 6, 1, D_MODEL) for l in range(DEPTH)]
    mod_sample = [jnp.repeat(mod_all[l, batch:n_c].reshape(dec_batch, 6, D_MODEL), dec_seq, axis=0)
                  .transpose(1, 0, 2) for l in range(DEPTH)]

    abr, abi, bbr, bbi = _ssm_param_call(ssm_a_re, ssm_a_im, ssm_log_step, ssm_b_re, ssm_b_im)
    ssm = []
    for l in range(DEPTH):
        pick = lambda a: a[l].reshape(N_SSM_GROUPS, SSM_STATE, SSM_GROUP_CH)
        ab = jnp.stack([pick(abr)[:, :, 0].reshape(SSM_LANES), pick(abi)[:, :, 0].reshape(SSM_LANES)])
        bre = _block_diag_halves(pick(bbr).transpose(0, 2, 1), BF16)
        bim = _block_diag_halves(pick(bbi).transpose(0, 2, 1), BF16)
        cre = _block_diag_halves(ssm_c_re[l].transpose(0, 2, 1), BF16)
        cim = _block_diag_halves(ssm_c_im[l].transpose(0, 2, 1), BF16)
        ssm.append((ab, bre, bim, cre, cim, ssm_d[l].reshape(1, SSM_WIDTH)))

    w_router = jnp.concatenate(
        [w_router_exp, w_router_grp,
         jnp.zeros((DEPTH, D_MODEL, ROUTER_LANES - N_EXPERTS - N_EXPERT_GROUPS), F32)], axis=-1)
    b_router = jnp.concatenate(
        [b_router_exp, b_router_grp,
         jnp.zeros((DEPTH, ROUTER_LANES - N_EXPERTS - N_EXPERT_GROUPS), F32)], axis=-1)
    w_router_hi = w_router.astype(BF16)
    p = dict(
        g_mix=g_norm_mix.reshape(DEPTH, 1, D_MODEL), g_ffn=g_norm_ffn.reshape(DEPTH, 1, D_MODEL),
        g_final=g_final.reshape(1, D_MODEL), w_in=w_in.astype(BF16), w_glu=w_glu.astype(BF16),
        w_br_ssm=w_br_ssm.astype(BF16), w_br_attn=w_br_attn.astype(BF16),
        w_mgate=w_mgate.astype(BF16), b_mgate=b_mgate.reshape(DEPTH, 1, 2 * D_MODEL),
        w_out=w_out.astype(BF16), w_router_hi=w_router_hi,
        w_router_lo=(w_router - w_router_hi.astype(F32)).astype(BF16),
        b_router=b_router.reshape(DEPTH, 1, ROUTER_LANES),
        w_exp_gate=w_exp_gate.astype(BF16), w_exp_up=w_exp_up.astype(BF16),
        w_exp_down=w_exp_down.astype(BF16))

    y_prompt, pstate = _run_trunk(x_prompt.reshape(batch * seq, D_MODEL), mod_prompt, p, ssm,
                                  batch, seq, None, None)
    caches = [cache_k_w128, cache_v_w128, cache_k_w512, cache_v_w512, cache_k_w2048, cache_v_w2048]
    h0 = (state_ssm_re.reshape(DEPTH, dec_batch, SSM_LANES), state_ssm_im.reshape(DEPTH, dec_batch, SSM_LANES))
    y_sample, sstate = _run_trunk(x_sample.reshape(dec_batch * dec_seq, D_MODEL), mod_sample, p, ssm,
                                  dec_batch, dec_seq, caches, h0)
    return (y_prompt.reshape(batch, seq, D_MODEL), y_sample.reshape(dec_batch, dec_seq, D_MODEL),
            *pstate, *sstate)
```

```python
import functools
import math

import jax
import jax.numpy as jnp
from jax import lax
from jax.experimental import pallas as pl
from jax.experimental.pallas import tpu as pltpu

F32 = jnp.float32
BF16 = jnp.bfloat16

D_MODEL = 1024
DEPTH = 2
SSM_WIDTH = 512
SSM_GROUP_CH = 16
N_SSM_GROUPS = 32
SSM_STATE = 64
SSM_LANES = N_SSM_GROUPS * SSM_STATE
HEAD_DIM = 64
HEADS = 4
GROUP_WIDTH = HEADS * HEAD_DIM
PAIR_WIDTH = 2 * HEAD_DIM
WINDOWS = (128, 512, 2048)
DILATIONS = (1, 4, 16)
N_BACK = 128
N_GROUPS = 3
QKV_WIDTH = N_GROUPS * GROUP_WIDTH
IN_PROJ_WIDTH = SSM_WIDTH + 3 * QKV_WIDTH
N_EXPERT_GROUPS = 4
EXPERTS_PER_GROUP = 8
N_EXPERTS = 32
EXPERT_FF = 256
RMS_EPS = 1e-6
NEG_INF = -1e30
ROUTER_LANES = 128
GROUP_LOGIT_LANE0 = N_EXPERTS

SUBLANES = 8
VMEM_LIMIT = 56 * 1024 * 1024


def _cparams(sem):
    return pltpu.CompilerParams(dimension_semantics=sem, vmem_limit_bytes=VMEM_LIMIT)


def _sigmoid(x):
    return 1.0 / (1.0 + jnp.exp(-x))


def _gelu_tanh(x):
    return 0.5 * x * (1.0 + jnp.tanh(math.sqrt(2.0 / math.pi) * (x + 0.044715 * (x * x * x))))


def _dot(a, b):
    return jnp.dot(a, b, preferred_element_type=F32)


def _rmsnorm(x, g):
    ms = jnp.mean(x * x, axis=-1, keepdims=True)
    return x * lax.rsqrt(ms + RMS_EPS) * g


def _ada_kernel(c_ref, w_ref, b_ref, o_ref):
    c = c_ref[...]
    s = (c * _sigmoid(c)).astype(BF16)
    o_ref[...] = _dot(s, w_ref[...].astype(BF16)) + b_ref[...]


def _ada_call(c_all, w_ada, b_ada):
    rows = c_all.shape[0]
    tn = 1536
    return pl.pallas_call(
        _ada_kernel,
        grid=(DEPTH, 6 * D_MODEL // tn),
        in_specs=[
            pl.BlockSpec((rows, D_MODEL), lambda l, j: (0, 0)),
            pl.BlockSpec((None, D_MODEL, tn), lambda l, j: (l, 0, j)),
            pl.BlockSpec((None, 1, tn), lambda l, j: (l, 0, j)),
        ],
        out_specs=pl.BlockSpec((None, rows, tn), lambda l, j: (l, 0, j)),
        out_shape=jax.ShapeDtypeStruct((DEPTH, rows, 6 * D_MODEL), F32),
        compiler_params=_cparams(("parallel", "parallel")),
        name="ada_mod",
    )(c_all, w_ada, b_ada.reshape(DEPTH, 1, 6 * D_MODEL))


def _ssm_param_kernel(ar_ref, ai_ref, ls_ref, br_ref, bi_ref, abr_ref, abi_ref, bbr_ref, bbi_ref):
    lr, li = ar_ref[...], ai_ref[...]
    dt = jnp.exp(ls_ref[...])
    mag = jnp.exp(lr * dt)
    abr = mag * jnp.cos(li * dt)
    abi = mag * jnp.sin(li * dt)
    den = lr * lr + li * li
    er, ei = abr - 1.0, abi
    fr = (er * lr + ei * li) / den
    fi = (ei * lr - er * li) / den
    br, bi = br_ref[...], bi_ref[...]
    abr_ref[...] = abr
    abi_ref[...] = abi
    bbr_ref[...] = fr * br - fi * bi
    bbi_ref[...] = fr * bi + fi * br


def _ssm_param_call(a_re, a_im, log_step, b_re, b_im):
    w = SSM_STATE * SSM_GROUP_CH
    rep = lambda a: jnp.repeat(a, SSM_GROUP_CH, axis=-1)
    big = pl.BlockSpec((None, N_SSM_GROUPS, w), lambda l: (l, 0, 0))
    shp = jax.ShapeDtypeStruct((DEPTH, N_SSM_GROUPS, w), F32)
    return pl.pallas_call(
        _ssm_param_kernel,
        grid=(DEPTH,),
        in_specs=[big, big, pl.BlockSpec((None, N_SSM_GROUPS, 1), lambda l: (l, 0, 0)), big, big],
        out_specs=[big, big, big, big],
        out_shape=[shp, shp, shp, shp],
        compiler_params=_cparams(("parallel",)),
        name="ssm_params",
    )(rep(a_re), rep(a_im), log_step.reshape(DEPTH, N_SSM_GROUPS, 1),
      b_re.reshape(DEPTH, N_SSM_GROUPS, w), b_im.reshape(DEPTH, N_SSM_GROUPS, w))


def _block_diag_halves(blocks, dtype):
    g, r, c = blocks.shape
    half = g // 2
    eye = jnp.eye(half, dtype=bool)

    def one(b):
        m = jnp.where(eye[:, None, :, None], b[:, :, None, :], 0.0)
        return m.reshape(half * r, half * c)

    return jnp.stack([one(blocks[:half]), one(blocks[half:])]).astype(dtype)


def _inproj_kernel(x_ref, mod_ref, g_ref, w_ref, u_ref, a0_ref, a1_ref, a2_ref):
    h = _rmsnorm(x_ref[...], g_ref[...]) * (1.0 + mod_ref[1]) + mod_ref[0]
    hb = h.astype(BF16)
    u_ref[...] = _dot(hb, w_ref[:, :SSM_WIDTH])
    for g, ref in enumerate((a0_ref, a1_ref, a2_ref)):
        for a in range(3):
            c0 = SSM_WIDTH + a * QKV_WIDTH + g * GROUP_WIDTH
            val = _dot(hb, w_ref[:, c0:c0 + GROUP_WIDTH])
            ref[a, 0] = val[:, :PAIR_WIDTH]
            ref[a, 1] = val[:, PAIR_WIDTH:]


def _mod_spec(mod, tm, tiles_per_seq):
    if mod.ndim == 4:
        return pl.BlockSpec((None, 6, 1, D_MODEL), lambda i: (i // tiles_per_seq, 0, 0, 0))
    return pl.BlockSpec((6, tm, D_MODEL), lambda i: (0, i, 0))


def _inproj_call(x, mod, g_mix, w_in, l, tm, tiles_per_seq):
    ntok = x.shape[0]
    grp = jax.ShapeDtypeStruct((3, 2, ntok, PAIR_WIDTH), F32)
    grp_spec = pl.BlockSpec((3, 2, tm, PAIR_WIDTH), lambda i: (0, 0, i, 0))
    return pl.pallas_call(
        _inproj_kernel,
        grid=(ntok // tm,),
        in_specs=[
            pl.BlockSpec((tm, D_MODEL), lambda i: (i, 0)),
            _mod_spec(mod, tm, tiles_per_seq),
            pl.BlockSpec((None, 1, D_MODEL), lambda i: (l, 0, 0)),
            pl.BlockSpec((None, D_MODEL, IN_PROJ_WIDTH), lambda i: (l, 0, 0)),
        ],
        out_specs=[pl.BlockSpec((tm, SSM_WIDTH), lambda i: (i, 0)), grp_spec, grp_spec, grp_spec],
        out_shape=[jax.ShapeDtypeStruct((ntok, SSM_WIDTH), F32), grp, grp, grp],
        compiler_params=_cparams(("parallel",)),
        name="in_proj",
    )(x, mod, g_mix, w_in)


LANE_TILE = 128
SCAN_CHUNK = 256
SEG_LEN = SCAN_CHUNK // SUBLANES
SCAN_TILE_GROUP = 4


def _cmul(ar, ai, br, bi):
    return ar * br - ai * bi, ar * bi + ai * br


def _ssm_prompt_kernel(u_ref, ab_ref, bre_ref, bim_ref, cre_ref, cim_ref, d_ref,
                       y_ref, hre_ref, him_ref, bur, bui, tab, carry):
    c = pl.program_id(1)
    half = SSM_LANES // 2
    tiles_per_half = half // LANE_TILE

    @pl.when(c == 0)
    def _():
        shp = (SUBLANES, SSM_LANES)
        sub = lax.broadcasted_iota(jnp.int32, shp, 0)
        b_r = jnp.broadcast_to(ab_ref[0:1, :], shp)
        b_i = jnp.broadcast_to(ab_ref[1:2, :], shp)
        for _ in range(SEG_LEN.bit_length() - 1):
            b_r, b_i = _cmul(b_r, b_i, b_r, b_i)
        p_r, p_i = b_r, b_i
        ps_r, ps_i = jnp.ones(shp, F32), jnp.zeros(shp, F32)
        for s in range(1, SUBLANES):
            if s in (1, 2, 4):
                k = {1: 0, 2: 2, 4: 4}[s]
                tab[k] = jnp.where(sub >= s, p_r, 0.0)
                tab[k + 1] = jnp.where(sub >= s, p_i, 0.0)
            ps_r = jnp.where(sub >= s, p_r, ps_r)
            ps_i = jnp.where(sub >= s, p_i, ps_i)
            p_r, p_i = _cmul(p_r, p_i, b_r, b_i)
        tab[6] = ps_r
        tab[7] = ps_i
        carry[...] = jnp.zeros_like(carry)

    u = u_ref[...]
    ub = u.astype(BF16)
    for h in range(2):
        uh = ub[:, h * 256:(h + 1) * 256]
        res_r = _dot(uh, bre_ref[h])
        res_i = _dot(uh, bim_ref[h])
        for j in range(tiles_per_half):
            bur[h * tiles_per_half + j] = res_r[:, j * LANE_TILE:(j + 1) * LANE_TILE]
            bui[h * tiles_per_half + j] = res_i[:, j * LANE_TILE:(j + 1) * LANE_TILE]

    vshape = (SUBLANES, LANE_TILE)
    sub1 = lax.broadcasted_iota(jnp.int32, vshape, 0) >= 1
    for g0 in range(0, SSM_LANES // LANE_TILE, SCAN_TILE_GROUP):
        tiles = list(range(g0, g0 + SCAN_TILE_GROUP))
        lanes = [slice(lt * LANE_TILE, (lt + 1) * LANE_TILE) for lt in tiles]
        a_r = [jnp.broadcast_to(ab_ref[0:1, sl], vshape) for sl in lanes]
        a_i = [jnp.broadcast_to(ab_ref[1:2, sl], vshape) for sl in lanes]

        def step(t, hs, store):
            out = []
            for j, lt in enumerate(tiles):
                rows = pl.ds(t, SUBLANES, stride=SEG_LEN)
                hr, hi = hs[2 * j], hs[2 * j + 1]
                nr = (a_r[j] * hr - a_i[j] * hi) + bur[lt, rows, :]
                ni = (a_r[j] * hi + a_i[j] * hr) + bui[lt, rows, :]
                if store:
                    bur[lt, rows, :] = nr
                    bui[lt, rows, :] = ni
                out += [nr, ni]
            return tuple(out)

        zeros = tuple(jnp.zeros(vshape, F32) for _ in range(2 * SCAN_TILE_GROUP))
        ends = lax.fori_loop(0, SEG_LEN, lambda t, hs: step(t, hs, False), zeros, unroll=4)

        starts = []
        for j, sl in enumerate(lanes):
            fr, fi = ends[2 * j], ends[2 * j + 1]
            for k in (1, 2, 4):
                ti = {1: 0, 2: 2, 4: 4}[k]
                mr, mi = tab[ti, :, sl], tab[ti + 1, :, sl]
                rr = pltpu.roll(fr, k, axis=0)
                ri = pltpu.roll(fi, k, axis=0)
                fr, fi = fr + (mr * rr - mi * ri), fi + (mr * ri + mi * rr)
            cr, ci = carry[0, :, sl], carry[1, :, sl]
            ps_r, ps_i = tab[6, :, sl], tab[7, :, sl]
            sr = (ps_r * cr - ps_i * ci) + jnp.where(sub1, pltpu.roll(fr, 1, axis=0), 0.0)
            si = (ps_r * ci + ps_i * cr) + jnp.where(sub1, pltpu.roll(fi, 1, axis=0), 0.0)
            starts += [sr, si]

        last = lax.fori_loop(0, SEG_LEN, lambda t, hs: step(t, hs, True), tuple(starts), unroll=4)
        for j, sl in enumerate(lanes):
            carry[0, :, sl] = jnp.broadcast_to(last[2 * j][SUBLANES - 1:SUBLANES, :], vshape)
            carry[1, :, sl] = jnp.broadcast_to(last[2 * j + 1][SUBLANES - 1:SUBLANES, :], vshape)

    for h in range(2):
        cs = slice(h * 256, (h + 1) * 256)
        tl = range(h * tiles_per_half, (h + 1) * tiles_per_half)
        hr = jnp.concatenate([bur[lt] for lt in tl], axis=-1).astype(BF16)
        hi = jnp.concatenate([bui[lt] for lt in tl], axis=-1).astype(BF16)
        y = _dot(hr, cre_ref[h]) - _dot(hi, cim_ref[h])
        y_ref[:, cs] = y + d_ref[:, cs] * u[:, cs]

    @pl.when(c == pl.num_programs(1) - 1)
    def _():
        hre_ref[...] = carry[0]
        him_ref[...] = carry[1]


def _ssm_prompt_call(u, ab, bre, bim, cre, cim, d_skip, n_seq, seq):
    chunks = seq // SCAN_CHUNK
    half = SSM_LANES // 2
    wspec_b = pl.BlockSpec((2, 256, half), lambda n, c: (0, 0, 0))
    wspec_c = pl.BlockSpec((2, half, 256), lambda n, c: (0, 0, 0))
    st_spec = pl.BlockSpec((None, SUBLANES, SSM_LANES), lambda n, c: (n, 0, 0))
    st_shape = jax.ShapeDtypeStruct((n_seq, SUBLANES, SSM_LANES), F32)
    return pl.pallas_call(
        _ssm_prompt_kernel,
        grid=(n_seq, chunks),
        in_specs=[
            pl.BlockSpec((SCAN_CHUNK, SSM_WIDTH), lambda n, c: (n * chunks + c, 0)),
            pl.BlockSpec((2, SSM_LANES), lambda n, c: (0, 0)),
            wspec_b, wspec_b, wspec_c, wspec_c,
            pl.BlockSpec((1, SSM_WIDTH), lambda n, c: (0, 0)),
        ],
        out_specs=[pl.BlockSpec((SCAN_CHUNK, SSM_WIDTH), lambda n, c: (n * chunks + c, 0)),
                   st_spec, st_spec],
        out_shape=[jax.ShapeDtypeStruct((n_seq * seq, SSM_WIDTH), F32), st_shape, st_shape],
        scratch_shapes=[pltpu.VMEM((SSM_LANES // LANE_TILE, SCAN_CHUNK, LANE_TILE), F32),
                        pltpu.VMEM((SSM_LANES // LANE_TILE, SCAN_CHUNK, LANE_TILE), F32),
                        pltpu.VMEM((8, SUBLANES, SSM_LANES), F32),
                        pltpu.VMEM((2, SUBLANES, SSM_LANES), F32)],
        compiler_params=_cparams(("parallel", "arbitrary")),
        name="ssm_prompt",
    )(u, ab, bre, bim, cre, cim, d_skip)


def _ssm_sample_kernel(u_ref, ab_ref, bre_ref, bim_ref, cre_ref, cim_ref, d_ref, h0r_ref, h0i_ref,
                       y_ref, hr_ref, hi_ref, *, steps):
    half = SSM_LANES // 2
    ar, ai = ab_ref[0:1, :], ab_ref[1:2, :]
    hr_ref[...] = h0r_ref[...]
    hi_ref[...] = h0i_ref[...]
    for t in range(steps):
        u = u_ref[:, t * SSM_WIDTH:(t + 1) * SSM_WIDTH]
        ub = u.astype(BF16)
        for h in range(2):
            hs = slice(h * half, (h + 1) * half)
            cs = slice(h * 256, (h + 1) * 256)
            uh = ub[:, cs]
            hr, hi = hr_ref[:, hs], hi_ref[:, hs]
            a_r, a_i = ar[:, hs], ai[:, hs]
            nr = (a_r * hr - a_i * hi) + _dot(uh, bre_ref[h])
            ni = (a_r * hi + a_i * hr) + _dot(uh, bim_ref[h])
            hr_ref[:, hs] = nr
            hi_ref[:, hs] = ni
            y = _dot(nr.astype(BF16), cre_ref[h]) - _dot(ni.astype(BF16), cim_ref[h])
            y_ref[:, t * SSM_WIDTH + h * 256:t * SSM_WIDTH + (h + 1) * 256] = y + d_ref[:, cs] * u[:, cs]


def _ssm_sample_call(u, ab, bre, bim, cre, cim, d_skip, h0r, h0i, steps):
    n = u.shape[0]
    st = jax.ShapeDtypeStruct((n, SSM_LANES), F32)
    return pl.pallas_call(
        functools.partial(_ssm_sample_kernel, steps=steps),
        out_shape=[jax.ShapeDtypeStruct((n, steps * SSM_WIDTH), F32), st, st],
        compiler_params=pltpu.CompilerParams(vmem_limit_bytes=VMEM_LIMIT),
        name="ssm_sample",
    )(u, ab, bre, bim, cre, cim, d_skip, h0r, h0i)


_NT = (((1,), (1,)), ((), ()))


def _attn_prompt_kernel(q_ref, kp_ref, kc_ref, vp_ref, vc_ref, o_ref, lse_ref, *, dil, q_blocks):
    b = pl.program_id(2)
    qi = lax.broadcasted_iota(jnp.int32, (N_BACK, 2 * N_BACK), 0)
    kj = lax.broadcasted_iota(jnp.int32, (N_BACK, 2 * N_BACK), 1)
    band = (kj >= qi) & (kj <= qi + N_BACK)
    band_first = band & (kj >= jnp.where(b > 0, 0, N_BACK))
    lane = lax.broadcasted_iota(jnp.int32, (N_BACK, PAIR_WIDTH), 1)
    head0 = lane < HEAD_DIM

    def rows(j, r):
        start = j * N_BACK * dil + r
        return pl.ds(start, N_BACK, stride=dil) if dil > 1 else pl.ds(start, N_BACK)

    def one(j, r):
        cur = rows(j, r)
        q = (q_ref[cur, :] * (HEAD_DIM ** -0.5)).astype(BF16)
        k_prev = kp_ref[rows(0, r), :] if j == 0 else kc_ref[rows(j - 1, r), :]
        v_prev = vp_ref[rows(0, r), :] if j == 0 else vc_ref[rows(j - 1, r), :]
        kk = jnp.concatenate([k_prev, kc_ref[cur, :]], axis=0).astype(BF16)
        vv = jnp.concatenate([v_prev, vc_ref[cur, :]], axis=0).astype(BF16)
        valid = band_first if j == 0 else band
        outs, lses = [], []
        for sub in range(2):
            in_head = head0 if sub == 0 else jnp.logical_not(head0)
            qm = jnp.where(in_head, q, jnp.zeros_like(q))
            s = lax.dot_general(qm, kk, _NT, preferred_element_type=F32)
            s = jnp.where(valid, s, NEG_INF)
            m = jnp.max(s, axis=-1, keepdims=True)
            p = jnp.exp(s - m)
            den = jnp.sum(p, axis=-1, keepdims=True)
            outs.append(_dot(p.astype(BF16), vv) / den)
            lses.append(jnp.broadcast_to(m + jnp.log(den), (N_BACK, PAIR_WIDTH)))
        o_ref[cur, :] = jnp.where(head0, outs[0], outs[1])
        lse_ref[cur, :] = jnp.where(head0, lses[0], lses[1])

    def residue(r, carry):
        for j in range(q_blocks):
            one(j, r)
        return carry

    if dil <= ATTN_UNROLL:
        for r in range(dil):
            residue(r, 0)
    else:
        lax.fori_loop(0, dil, residue, 0, unroll=ATTN_UNROLL)


ATTN_UNROLL = 4
ATTN_Q_BLOCKS = {1: 4, 4: 1, 16: 1}


def _attn_prompt_call(qkv, n_seq, seq, dil):
    q_blocks = ATTN_Q_BLOCKS[dil]
    prev_rows = N_BACK * dil
    rows = prev_rows * q_blocks
    nbk = seq // rows
    cur = lambda a: pl.BlockSpec((None, None, rows, PAIR_WIDTH), lambda n, hp, b: (a, hp, n * nbk + b, 0))
    prev = lambda a: pl.BlockSpec(
        (None, None, prev_rows, PAIR_WIDTH),
        lambda n, hp, b: (a, hp, jnp.maximum((n * nbk + b) * q_blocks - 1, n * nbk * q_blocks), 0))
    out_spec = pl.BlockSpec((None, rows, PAIR_WIDTH), lambda n, hp, b: (hp, n * nbk + b, 0))
    out_shape = jax.ShapeDtypeStruct((2, n_seq * seq, PAIR_WIDTH), F32)
    return pl.pallas_call(
        functools.partial(_attn_prompt_kernel, dil=dil, q_blocks=q_blocks),
        grid=(n_seq, 2, nbk),
        in_specs=[cur(0), prev(1), cur(1), prev(2), cur(2)],
        out_specs=[out_spec, out_spec],
        out_shape=[out_shape, out_shape],
        compiler_params=_cparams(("parallel", "parallel", "parallel")),
        name=f"attn_prompt_d{dil}",
    )(qkv, qkv, qkv, qkv, qkv)


def _kv_tail_kernel(x_ref, o_ref):
    o_ref[...] = x_ref[...].T


def _kv_tail_call(qkv, n_seq, seq, keep):
    blocks_per_seq = seq // keep
    return pl.pallas_call(
        _kv_tail_kernel,
        grid=(2, n_seq, 2),
        in_specs=[pl.BlockSpec((None, None, keep, PAIR_WIDTH),
                               lambda a, n, hp: (a + 1, hp, (n + 1) * blocks_per_seq - 1, 0))],
        out_specs=pl.BlockSpec((None, None, None, PAIR_WIDTH, keep), lambda a, n, hp: (a, n, hp, 0, 0)),
        out_shape=jax.ShapeDtypeStruct((2, n_seq, 2, PAIR_WIDTH, keep), F32),
        compiler_params=_cparams(("parallel", "parallel", "parallel")),
        name="kv_tail",
    )(qkv)


SAMPLE_SEQ_BLOCK = {128: 4, 512: 4, 2048: 2}


def _attn_sample_kernel(qkv_ref, kt_ref, vt_ref, o_ref, lse_ref, *, dil, steps):
    nb, _, _, w = kt_ref.shape
    rows = nb * steps
    row = lax.broadcasted_iota(jnp.int32, (rows, w), 0)
    pos = lax.broadcasted_iota(jnp.int32, (rows, w), 1)
    rown = lax.broadcasted_iota(jnp.int32, (rows, rows), 0)
    coln = lax.broadcasted_iota(jnp.int32, (rows, rows), 1)
    rowo = lax.broadcasted_iota(jnp.int32, (rows, HEAD_DIM), 0)
    for hp in range(2):
        q2 = qkv_ref[0, hp] * (HEAD_DIM ** -0.5)
        k2 = qkv_ref[1, hp]
        v2 = qkv_ref[2, hp]
        o_heads, lse_heads = [], []
        for sub in range(2):
            ls = slice(sub * HEAD_DIM, (sub + 1) * HEAD_DIM)
            h = hp * 2 + sub
            q = q2[:, ls].astype(BF16)
            kn = k2[:, ls].astype(BF16)
            vn = v2[:, ls].astype(BF16)
            o_h = jnp.zeros((rows, HEAD_DIM), F32)
            lse_h = jnp.zeros((rows, HEAD_DIM), F32)
            for n in range(nb):
                t = row - n * steps
                in_buf = (((pos - t) & (dil - 1)) == 0) & (pos >= t)
                tn = rown - n * steps
                sn = coln - n * steps
                in_new = ((sn >= 0) & (sn <= tn)) if dil == 1 else (sn == tn)
                s = jnp.where(in_buf, _dot(q, kt_ref[n, h].astype(BF16)), NEG_INF)
                s2 = jnp.where(in_new, lax.dot_general(q, kn, _NT, preferred_element_type=F32), NEG_INF)
                m = jnp.maximum(jnp.max(s, axis=-1, keepdims=True), jnp.max(s2, axis=-1, keepdims=True))
                p = jnp.exp(s - m)
                p2 = jnp.exp(s2 - m)
                den = jnp.sum(p, axis=-1, keepdims=True) + jnp.sum(p2, axis=-1, keepdims=True)
                o = lax.dot_general(p.astype(BF16), vt_ref[n, h].astype(BF16), _NT,
                                    preferred_element_type=F32)
                o = (o + _dot(p2.astype(BF16), vn)) / den
                mine = (rowo >= n * steps) & (rowo < (n + 1) * steps)
                o_h = jnp.where(mine, o, o_h)
                lse_h = jnp.where(mine, m + jnp.log(den), lse_h)
            o_heads.append(o_h)
            lse_heads.append(lse_h)
        o_ref[hp] = jnp.concatenate(o_heads, axis=-1)
        lse_ref[hp] = jnp.concatenate(lse_heads, axis=-1)


def _attn_sample_call(qkv, k_cache, v_cache, l, n_seq, steps, dil):
    w = k_cache.shape[2]
    kt = k_cache.transpose(0, 1, 3, 4, 2)
    vt = v_cache.transpose(0, 1, 3, 4, 2)
    nb = SAMPLE_SEQ_BLOCK[w]
    rows = nb * steps
    cache_spec = pl.BlockSpec((None, nb, HEADS, HEAD_DIM, w), lambda i: (l, i, 0, 0, 0))
    out_spec = pl.BlockSpec((2, rows, PAIR_WIDTH), lambda i: (0, i, 0))
    out_shape = jax.ShapeDtypeStruct((2, n_seq * steps, PAIR_WIDTH), F32)
    return pl.pallas_call(
        functools.partial(_attn_sample_kernel, dil=dil, steps=steps),
        grid=(n_seq // nb,),
        in_specs=[pl.BlockSpec((3, 2, rows, PAIR_WIDTH), lambda i: (0, 0, i, 0)), cache_spec, cache_spec],
        out_specs=[out_spec, out_spec],
        out_shape=[out_shape, out_shape],
        compiler_params=_cparams(("parallel",)),
        name=f"attn_sample_d{dil}",
    )(qkv, kt, vt)


def _merge_kernel(x_ref, mod_ref, y_ref, o0_ref, l0_ref, o1_ref, l1_ref, o2_ref, l2_ref,
                  gmix_ref, gffn_ref, wmg_ref, bmg_ref, wglu_ref, wbs_ref, wba_ref, wout_ref,
                  wrh_ref, wrl_ref, br_ref, x1_ref, h2_ref, comb_ref):
    x = x_ref[...]
    tm = x.shape[0]
    hb = (_rmsnorm(x, gmix_ref[...]) * (1.0 + mod_ref[1]) + mod_ref[0]).astype(BF16)

    y = _gelu_tanh(y_ref[...])
    yg = y * _sigmoid(_dot(y.astype(BF16), wglu_ref[...]))
    ssm_branch = _dot(yg.astype(BF16), wbs_ref[...])

    pairs = []
    for hp in range(2):
        l0, l1, l2 = l0_ref[hp], l1_ref[hp], l2_ref[hp]
        lm = jnp.maximum(jnp.maximum(l0, l1), l2)
        e0, e1, e2 = jnp.exp(l0 - lm), jnp.exp(l1 - lm), jnp.exp(l2 - lm)
        pairs.append((e0 * o0_ref[hp] + e1 * o1_ref[hp] + e2 * o2_ref[hp]) / (e0 + e1 + e2))
    attn = jnp.concatenate(pairs, axis=-1)
    attn_branch = _dot(attn.astype(BF16), wba_ref[...])

    g_ssm = _sigmoid(_dot(hb, wmg_ref[:, :D_MODEL]) + bmg_ref[:, :D_MODEL])
    g_attn = _sigmoid(_dot(hb, wmg_ref[:, D_MODEL:]) + bmg_ref[:, D_MODEL:])
    merged = g_ssm * ssm_branch + g_attn * attn_branch
    x1 = x + mod_ref[2] * _dot(merged.astype(BF16), wout_ref[...])
    x1_ref[...] = x1

    h2 = _rmsnorm(x1, gffn_ref[...]) * (1.0 + mod_ref[4]) + mod_ref[3]
    h2_hi = h2.astype(BF16)
    h2_ref[...] = h2_hi
    h2_lo = (h2 - h2_hi.astype(F32)).astype(BF16)
    logits = (_dot(h2_hi, wrh_ref[...]) + _dot(h2_hi, wrl_ref[...]) + _dot(h2_lo, wrh_ref[...])
              + br_ref[...])

    lane = lax.broadcasted_iota(jnp.int32, (tm, ROUTER_LANES), 1)
    lanef = lane.astype(F32)
    low = jnp.float32(-3.0e38)
    is_grp = (lane >= GROUP_LOGIT_LANE0) & (lane < GROUP_LOGIT_LANE0 + N_EXPERT_GROUPS)
    lg = jnp.where(is_grp, logits, low)
    mg = jnp.max(lg, axis=-1, keepdims=True)
    gsel = jnp.min(jnp.where(lg == mg, lanef - GROUP_LOGIT_LANE0, 1e9), axis=-1, keepdims=True)
    pg_sel = 1.0 / jnp.sum(jnp.exp(lg - mg), axis=-1, keepdims=True)
    in_grp = (lane < N_EXPERTS) & ((lane // EXPERTS_PER_GROUP).astype(F32) == gsel)
    le = jnp.where(in_grp, logits, low)
    m1 = jnp.max(le, axis=-1, keepdims=True)
    i1 = jnp.min(jnp.where(le == m1, lanef, 1e9), axis=-1, keepdims=True)
    le2 = jnp.where(lanef == i1, low, le)
    m2 = jnp.max(le2, axis=-1, keepdims=True)
    i2 = jnp.min(jnp.where(le2 == m2, lanef, 1e9), axis=-1, keepdims=True)
    r = jnp.exp(m2 - m1)
    w1 = 1.0 / (1.0 + r)
    w2 = r / (1.0 + r)
    comb = pg_sel * (jnp.where(lanef == i1, w1, 0.0) + jnp.where(lanef == i2, w2, 0.0))
    comb_ref[...] = comb + jnp.where(lane == GROUP_ID_LANE, gsel, 0.0)


def _merge_call(x, mod, y, attn_outs, p, l, tm, tiles_per_seq):
    ntok = x.shape[0]
    tok = lambda w: pl.BlockSpec((tm, w), lambda i: (i, 0))
    lay = lambda *shape: pl.BlockSpec((None,) + shape, lambda i: (l,) + (0,) * len(shape))
    attn_specs = [pl.BlockSpec((2, tm, PAIR_WIDTH), lambda i: (0, i, 0))] * 6
    return pl.pallas_call(
        _merge_kernel,
        grid=(ntok // tm,),
        in_specs=[tok(D_MODEL), _mod_spec(mod, tm, tiles_per_seq), tok(SSM_WIDTH)] + attn_specs + [
            lay(1, D_MODEL), lay(1, D_MODEL),
            lay(D_MODEL, 2 * D_MODEL), lay(1, 2 * D_MODEL),
            lay(SSM_WIDTH, SSM_WIDTH), lay(SSM_WIDTH, D_MODEL), lay(GROUP_WIDTH, D_MODEL),
            lay(D_MODEL, D_MODEL),
            lay(D_MODEL, ROUTER_LANES), lay(D_MODEL, ROUTER_LANES), lay(1, ROUTER_LANES),
        ],
        out_specs=[tok(D_MODEL), tok(D_MODEL), tok(ROUTER_LANES)],
        out_shape=[jax.ShapeDtypeStruct((ntok, D_MODEL), F32),
                   jax.ShapeDtypeStruct((ntok, D_MODEL), BF16),
                   jax.ShapeDtypeStruct((ntok, ROUTER_LANES), F32)],
        compiler_params=_cparams(("parallel",)),
        name="merge_router",
    )(x, mod, y, *attn_outs, p['g_mix'], p['g_ffn'], p['w_mgate'], p['b_mgate'], p['w_glu'],
      p['w_br_ssm'], p['w_br_attn'], p['w_out'], p['w_router_hi'], p['w_router_lo'], p['b_router'])


MOE_EXPERT_BLOCK = 4
MOE_ROW_CHUNK = 128
GROUP_ID_LANE = 64
_TN = (((0,), (0,)), ((), ()))


def _split3(x):
    a = x.astype(BF16)
    r = x - a.astype(F32)
    b = r.astype(BF16)
    return a, b, (r - b.astype(F32)).astype(BF16)


def _moe_kernel(h2_ref, comb_ref, x1_ref, mod_ref, wg_ref, wu_ref, wd_ref, gfin_ref, out_ref,
                perm, hs, combs, ys, bounds, *, final):
    s = pl.program_id(1)
    tm = h2_ref.shape[0]
    ch = MOE_ROW_CHUNK

    @pl.when(s == 0)
    def _():
        comb = comb_ref[...]
        lane = lax.broadcasted_iota(jnp.int32, (tm, ROUTER_LANES), 1)
        gid = jnp.sum(jnp.where(lane == GROUP_ID_LANE, comb, 0.0), axis=-1, keepdims=True)
        onehot = jnp.where(lane.astype(F32) == gid, 1.0, 0.0)
        row = lax.broadcasted_iota(jnp.int32, (tm, tm), 0)
        col = lax.broadcasted_iota(jnp.int32, (tm, tm), 1)
        earlier = jnp.where(col < row, 1.0, 0.0).astype(BF16)
        rank = _dot(earlier, onehot.astype(BF16))
        tot = jnp.sum(onehot, axis=0, keepdims=True)
        lane1 = lax.broadcasted_iota(jnp.int32, (1, ROUTER_LANES), 1)
        start = jnp.zeros((1, ROUTER_LANES), F32)
        run = jnp.zeros((1, 1), F32)
        for g in range(N_EXPERT_GROUPS):
            bounds[2 * g] = run[0, 0].astype(jnp.int32)
            start = start + jnp.where(lane1 == g, run, 0.0)
            run = run + jnp.sum(jnp.where(lane1 == g, tot, 0.0), axis=-1, keepdims=True)
            bounds[2 * g + 1] = run[0, 0].astype(jnp.int32)
        pos = jnp.sum(onehot * (rank + start), axis=-1, keepdims=True)
        perm[...] = jnp.where(col.astype(F32) == pos, 1.0, 0.0).astype(BF16)
        hs[...] = lax.dot_general(perm[...], h2_ref[...], _TN, preferred_element_type=F32).astype(BF16)
        parts = lax.dot_general(perm[...], jnp.concatenate(_split3(comb), axis=-1), _TN,
                                preferred_element_type=F32)
        combs[...] = (parts[:, :ROUTER_LANES] + parts[:, ROUTER_LANES:2 * ROUTER_LANES]) + parts[:, 2 * ROUTER_LANES:]
        ys[...] = jnp.zeros_like(ys)

    g = s // (EXPERTS_PER_GROUP // MOE_EXPERT_BLOCK)
    lo = bounds[2 * g] // ch
    hi = (bounds[2 * g + 1] + (ch - 1)) // ch
    lane = lax.broadcasted_iota(jnp.int32, (ch, ROUTER_LANES), 1)

    def chunk(c, carry):
        r0 = pl.multiple_of(c * ch, ch)
        h = hs[pl.ds(r0, ch), :]
        comb = combs[pl.ds(r0, ch), :]
        hids = []
        for j in range(MOE_EXPERT_BLOCK):
            gt = _dot(h, wg_ref[j])
            w = jnp.sum(jnp.where(lane == s * MOE_EXPERT_BLOCK + j, comb, 0.0), axis=-1, keepdims=True)
            hids.append((gt * _sigmoid(gt) * _dot(h, wu_ref[j]) * w).astype(BF16))
        ys[pl.ds(r0, ch), :] += _dot(jnp.concatenate(hids, axis=-1), wd_ref[...])
        return carry

    lax.fori_loop(lo, hi, chunk, 0)

    @pl.when(s == pl.num_programs(1) - 1)
    def _():
        y = ys[...]
        y_hi = y.astype(BF16)
        y_lo = (y - y_hi.astype(F32)).astype(BF16)
        p = perm[...]
        x2 = x1_ref[...] + mod_ref[5] * (_dot(p, y_hi) + _dot(p, y_lo))
        out_ref[...] = _rmsnorm(x2, gfin_ref[...]) if final else x2


def _moe_call(h2, comb, x1, mod, p, l, tm, tiles_per_seq, final):
    ntok = h2.shape[0]
    eb = MOE_EXPERT_BLOCK
    tok = lambda w: pl.BlockSpec((tm, w), lambda i, s: (i, 0))
    if mod.ndim == 4:
        mod_spec = pl.BlockSpec((None, 6, 1, D_MODEL), lambda i, s: (i // tiles_per_seq, 0, 0, 0))
    else:
        mod_spec = pl.BlockSpec((6, tm, D_MODEL), lambda i, s: (0, i, 0))
    w_down = p['w_exp_down'].reshape(DEPTH, N_EXPERTS // eb, eb * EXPERT_FF, D_MODEL)
    return pl.pallas_call(
        functools.partial(_moe_kernel, final=final),
        grid=(ntok // tm, N_EXPERTS // eb),
        in_specs=[tok(D_MODEL), tok(ROUTER_LANES), tok(D_MODEL), mod_spec,
                  pl.BlockSpec((None, eb, D_MODEL, EXPERT_FF), lambda i, s: (l, s, 0, 0)),
                  pl.BlockSpec((None, eb, D_MODEL, EXPERT_FF), lambda i, s: (l, s, 0, 0)),
                  pl.BlockSpec((None, None, eb * EXPERT_FF, D_MODEL), lambda i, s: (l, s, 0, 0)),
                  pl.BlockSpec((1, D_MODEL), lambda i, s: (0, 0))],
        out_specs=tok(D_MODEL),
        out_shape=jax.ShapeDtypeStruct((ntok, D_MODEL), F32),
        scratch_shapes=[pltpu.VMEM((tm, tm), BF16), pltpu.VMEM((tm, D_MODEL), BF16),
                        pltpu.VMEM((tm, ROUTER_LANES), F32), pltpu.VMEM((tm, D_MODEL), F32),
                        pltpu.SMEM((2 * N_EXPERT_GROUPS,), jnp.int32)],
        compiler_params=_cparams(("parallel", "arbitrary")),
        name="moe_experts",
    )(h2, comb, x1, mod, p['w_exp_gate'], p['w_exp_up'], w_down, p['g_final'])


def _run_trunk(x, mod_layers, p, ssm, n_seq, seq, caches, h0):
    ntok = n_seq * seq
    prompt = caches is None
    tm_proj = 512 if prompt else 256
    tm_merge = 256
    tm_moe = 1024 if prompt else 256
    states = []
    for l in range(DEPTH):
        mod = mod_layers[l]
        u, *qkv = _inproj_call(x, mod, p['g_mix'], p['w_in'], l, tm_proj, seq // tm_proj if prompt else 1)
        ab, bre, bim, cre, cim, d_skip = ssm[l]
        if prompt:
            y, hre, him = _ssm_prompt_call(u, ab, bre, bim, cre, cim, d_skip, n_seq, seq)
            hre, him = hre[:, 0], him[:, 0]
            attn = []
            for g in range(N_GROUPS):
                attn += list(_attn_prompt_call(qkv[g], n_seq, seq, DILATIONS[g]))
        else:
            y, hre, him = _ssm_sample_call(u.reshape(n_seq, seq * SSM_WIDTH), ab, bre, bim, cre, cim,
                                           d_skip, h0[0][l], h0[1][l], seq)
            y = y.reshape(ntok, SSM_WIDTH)
            attn = []
            for g in range(N_GROUPS):
                attn += list(_attn_sample_call(qkv[g], caches[2 * g], caches[2 * g + 1], l, n_seq, seq,
                                               DILATIONS[g]))
        x1, h2, comb = _merge_call(x, mod, y, attn, p, l, tm_merge, seq // tm_merge if prompt else 1)
        x = _moe_call(h2, comb, x1, mod, p, l, tm_moe, seq // tm_moe if prompt else 1,
                      final=(l == DEPTH - 1))
        kv = []
        for g in range(N_GROUPS):
            keep = min(WINDOWS[g], seq)
            if prompt:
                tails = _kv_tail_call(qkv[g], n_seq, seq, keep).reshape(2, n_seq, HEADS, HEAD_DIM, keep)
                kv += [tails[0].transpose(0, 3, 1, 2), tails[1].transpose(0, 3, 1, 2)]
            else:
                for a in (1, 2):
                    new = qkv[g][a].reshape(2, n_seq, seq, 2, HEAD_DIM)
                    kv.append(new.transpose(1, 2, 0, 3, 4).reshape(n_seq, seq, HEADS, HEAD_DIM))
        states.append(kv + [hre.reshape(n_seq, N_SSM_GROUPS, SSM_STATE),
                            him.reshape(n_seq, N_SSM_GROUPS, SSM_STATE)])
    new_state = [jnp.stack([states[l][i] for l in range(DEPTH)], axis=0) for i in range(len(states[0]))]
    return x, new_state


def kernel(x_prompt, x_sample, cache_k_w128, cache_v_w128, cache_k_w512, cache_v_w512,
           cache_k_w2048, cache_v_w2048, state_ssm_re, state_ssm_im, c_prompt, c_sample,
           w_ada, b_ada, g_norm_mix, g_norm_ffn, g_final, w_in, ssm_a_re, ssm_a_im, ssm_b_re,
           ssm_b_im, ssm_c_re, ssm_c_im, ssm_log_step, ssm_d, w_glu, w_br_ssm, w_br_attn,
           w_mgate, b_mgate, w_out, w_router_grp, b_router_grp, w_router_exp, b_router_exp,
           w_exp_gate, w_exp_up, w_exp_down):
    batch, seq, _ = x_prompt.shape
    dec_batch, dec_seq, _ = x_sample.shape

    n_c = batch + dec_batch
    rows = -(-n_c // SUBLANES) * SUBLANES
    c_all = jnp.concatenate([c_prompt, c_sample, jnp.zeros((rows - n_c, D_MODEL), F32)], axis=0)
    mod_all = _ada_call(c_all, w_ada, b_ada)
    mod_prompt = [mod_all[l, :batch].reshape(batch, 6, 1, D_MODEL) for l in range(DEPTH)]
    mod_sample = [jnp.repeat(mod_all[l, batch:n_c].reshape(dec_batch, 6, D_MODEL), dec_seq, axis=0)
                  .transpose(1, 0, 2) for l in range(DEPTH)]

    abr, abi, bbr, bbi = _ssm_param_call(ssm_a_re, ssm_a_im, ssm_log_step, ssm_b_re, ssm_b_im)
    ssm = []
    for l in range(DEPTH):
        pick = lambda a: a[l].reshape(N_SSM_GROUPS, SSM_STATE, SSM_GROUP_CH)
        ab = jnp.stack([pick(abr)[:, :, 0].reshape(SSM_LANES), pick(abi)[:, :, 0].reshape(SSM_LANES)])
        bre = _block_diag_halves(pick(bbr).transpose(0, 2, 1), BF16)
        bim = _block_diag_halves(pick(bbi).transpose(0, 2, 1), BF16)
        cre = _block_diag_halves(ssm_c_re[l].transpose(0, 2, 1), BF16)
        cim = _block_diag_halves(ssm_c_im[l].transpose(0, 2, 1), BF16)
        ssm.append((ab, bre, bim, cre, cim, ssm_d[l].reshape(1, SSM_WIDTH)))

    w_router = jnp.concatenate(
        [w_router_exp, w_router_grp,
         jnp.zeros((DEPTH, D_MODEL, ROUTER_LANES - N_EXPERTS - N_EXPERT_GROUPS), F32)], axis=-1)
    b_router = jnp.concatenate(
        [b_router_exp, b_router_grp,
         jnp.zeros((DEPTH, ROUTER_LANES - N_EXPERTS - N_EXPERT_GROUPS), F32)], axis=-1)
    w_router_hi = w_router.astype(BF16)
    p = dict(
        g_mix=g_norm_mix.reshape(DEPTH, 1, D_MODEL), g_ffn=g_norm_ffn.reshape(DEPTH, 1, D_MODEL),
        g_final=g_final.reshape(1, D_MODEL), w_in=w_in.astype(BF16), w_glu=w_glu.astype(BF16),
        w_br_ssm=w_br_ssm.astype(BF16), w_br_attn=w_br_attn.astype(BF16),
        w_mgate=w_mgate.astype(BF16), b_mgate=b_mgate.reshape(DEPTH, 1, 2 * D_MODEL),
        w_out=w_out.astype(BF16), w_router_hi=w_router_hi,
        w_router_lo=(w_router - w_router_hi.astype(F32)).astype(BF16),
        b_router=b_router.reshape(DEPTH, 1, ROUTER_LANES),
        w_exp_gate=w_exp_gate.astype(BF16), w_exp_up=w_exp_up.astype(BF16),
        w_exp_down=w_exp_down.astype(BF16))

    y_prompt, pstate = _run_trunk(x_prompt.reshape(batch * seq, D_MODEL), mod_prompt, p, ssm,
                                  batch, seq, None, None)
    caches = [cache_k_w128, cache_v_w128, cache_k_w512, cache_v_w512, cache_k_w2048, cache_v_w2048]
    h0 = (state_ssm_re.reshape(DEPTH, dec_batch, SSM_LANES), state_ssm_im.reshape(DEPTH, dec_batch, SSM_LANES))
    y_sample, sstate = _run_trunk(x_sample.reshape(dec_batch * dec_seq, D_MODEL), mod_sample, p, ssm,
                                  dec_batch, dec_seq, caches, h0)
    return (y_prompt.reshape(batch, seq, D_MODEL), y_sample.reshape(dec_batch, dec_seq, D_MODEL),
            *pstate, *sstate)
```

```python
import functools
import math

import jax
import jax.numpy as jnp
from jax import lax
from jax.experimental import pallas as pl
from jax.experimental.pallas import tpu as pltpu

F32 = jnp.float32
BF16 = jnp.bfloat16

D_MODEL = 1024
DEPTH = 2
SSM_WIDTH = 512
SSM_GROUP_CH = 16
N_SSM_GROUPS = 32
SSM_STATE = 64
SSM_LANES = N_SSM_GROUPS * SSM_STATE
HEAD_DIM = 64
HEADS = 4
GROUP_WIDTH = HEADS * HEAD_DIM
PAIR_WIDTH = 2 * HEAD_DIM
WINDOWS = (128, 512, 2048)
DILATIONS = (1, 4, 16)
N_BACK = 128
N_GROUPS = 3
QKV_WIDTH = N_GROUPS * GROUP_WIDTH
IN_PROJ_WIDTH = SSM_WIDTH + 3 * QKV_WIDTH
N_EXPERT_GROUPS = 4
EXPERTS_PER_GROUP = 8
N_EXPERTS = 32
EXPERT_FF = 256
RMS_EPS = 1e-6
NEG_INF = -1e30
ROUTER_LANES = 128
GROUP_LOGIT_LANE0 = N_EXPERTS

SUBLANES = 8
VMEM_LIMIT = 56 * 1024 * 1024


def _cparams(sem):
    return pltpu.CompilerParams(dimension_semantics=sem, vmem_limit_bytes=VMEM_LIMIT)


def _sigmoid(x):
    return 1.0 / (1.0 + jnp.exp(-x))


def _gelu_tanh(x):
    return 0.5 * x * (1.0 + jnp.tanh(math.sqrt(2.0 / math.pi) * (x + 0.044715 * (x * x * x))))


def _dot(a, b):
    return jnp.dot(a, b, preferred_element_type=F32)


def _rmsnorm(x, g):
    ms = jnp.mean(x * x, axis=-1, keepdims=True)
    return x * lax.rsqrt(ms + RMS_EPS) * g


def _ada_kernel(c_ref, w_ref, b_ref, o_ref):
    c = c_ref[...]
    s = (c * _sigmoid(c)).astype(BF16)
    o_ref[...] = _dot(s, w_ref[...].astype(BF16)) + b_ref[...]


def _ada_call(c_all, w_ada, b_ada):
    rows = c_all.shape[0]
    tn = 1536
    return pl.pallas_call(
        _ada_kernel,
        grid=(DEPTH, 6 * D_MODEL // tn),
        in_specs=[
            pl.BlockSpec((rows, D_MODEL), lambda l, j: (0, 0)),
            pl.BlockSpec((None, D_MODEL, tn), lambda l, j: (l, 0, j)),
            pl.BlockSpec((None, 1, tn), lambda l, j: (l, 0, j)),
        ],
        out_specs=pl.BlockSpec((None, rows, tn), lambda l, j: (l, 0, j)),
        out_shape=jax.ShapeDtypeStruct((DEPTH, rows, 6 * D_MODEL), F32),
        compiler_params=_cparams(("parallel", "parallel")),
        name="ada_mod",
    )(c_all, w_ada, b_ada.reshape(DEPTH, 1, 6 * D_MODEL))


def _ssm_param_kernel(ar_ref, ai_ref, ls_ref, br_ref, bi_ref, abr_ref, abi_ref, bbr_ref, bbi_ref):
    lr, li = ar_ref[...], ai_ref[...]
    dt = jnp.exp(ls_ref[...])
    mag = jnp.exp(lr * dt)
    abr = mag * jnp.cos(li * dt)
    abi = mag * jnp.sin(li * dt)
    den = lr * lr + li * li
    er, ei = abr - 1.0, abi
    fr = (er * lr + ei * li) / den
    fi = (ei * lr - er * li) / den
    br, bi = br_ref[...], bi_ref[...]
    abr_ref[...] = abr
    abi_ref[...] = abi
    bbr_ref[...] = fr * br - fi * bi
    bbi_ref[...] = fr * bi + fi * br


def _ssm_param_call(a_re, a_im, log_step, b_re, b_im):
    w = SSM_STATE * SSM_GROUP_CH
    rep = lambda a: jnp.repeat(a, SSM_GROUP_CH, axis=-1)
    big = pl.BlockSpec((None, N_SSM_GROUPS, w), lambda l: (l, 0, 0))
    shp = jax.ShapeDtypeStruct((DEPTH, N_SSM_GROUPS, w), F32)
    return pl.pallas_call(
        _ssm_param_kernel,
        grid=(DEPTH,),
        in_specs=[big, big, pl.BlockSpec((None, N_SSM_GROUPS, 1), lambda l: (l, 0, 0)), big, big],
        out_specs=[big, big, big, big],
        out_shape=[shp, shp, shp, shp],
        compiler_params=_cparams(("parallel",)),
        name="ssm_params",
    )(rep(a_re), rep(a_im), log_step.reshape(DEPTH, N_SSM_GROUPS, 1),
      b_re.reshape(DEPTH, N_SSM_GROUPS, w), b_im.reshape(DEPTH, N_SSM_GROUPS, w))


def _block_diag_halves(blocks, dtype):
    g, r, c = blocks.shape
    half = g // 2
    eye = jnp.eye(half, dtype=bool)

    def one(b):
        m = jnp.where(eye[:, None, :, None], b[:, :, None, :], 0.0)
        return m.reshape(half * r, half * c)

    return jnp.stack([one(blocks[:half]), one(blocks[half:])]).astype(dtype)


def _inproj_kernel(x_ref, mod_ref, g_ref, w_ref, u_ref, a0_ref, a1_ref, a2_ref):
    h = _rmsnorm(x_ref[...], g_ref[...]) * (1.0 + mod_ref[1]) + mod_ref[0]
    hb = h.astype(BF16)
    u_ref[...] = _dot(hb, w_ref[:, :SSM_WIDTH])
    for g, ref in enumerate((a0_ref, a1_ref, a2_ref)):
        for a in range(3):
            c0 = SSM_WIDTH + a * QKV_WIDTH + g * GROUP_WIDTH
            val = _dot(hb, w_ref[:, c0:c0 + GROUP_WIDTH])
            ref[a, 0] = val[:, :PAIR_WIDTH]
            ref[a, 1] = val[:, PAIR_WIDTH:]


def _mod_spec(mod, tm, tiles_per_seq):
    if mod.ndim == 4:
        return pl.BlockSpec((None, 6, 1, D_MODEL), lambda i: (i // tiles_per_seq, 0, 0, 0))
    return pl.BlockSpec((6, tm, D_MODEL), lambda i: (0, i, 0))


def _inproj_call(x, mod, g_mix, w_in, l, tm, tiles_per_seq):
    ntok = x.shape[0]
    grp = jax.ShapeDtypeStruct((3, 2, ntok, PAIR_WIDTH), F32)
    grp_spec = pl.BlockSpec((3, 2, tm, PAIR_WIDTH), lambda i: (0, 0, i, 0))
    return pl.pallas_call(
        _inproj_kernel,
        grid=(ntok // tm,),
        in_specs=[
            pl.BlockSpec((tm, D_MODEL), lambda i: (i, 0)),
            _mod_spec(mod, tm, tiles_per_seq),
            pl.BlockSpec((None, 1, D_MODEL), lambda i: (l, 0, 0)),
            pl.BlockSpec((None, D_MODEL, IN_PROJ_WIDTH), lambda i: (l, 0, 0)),
        ],
        out_specs=[pl.BlockSpec((tm, SSM_WIDTH), lambda i: (i, 0)), grp_spec, grp_spec, grp_spec],
        out_shape=[jax.ShapeDtypeStruct((ntok, SSM_WIDTH), F32), grp, grp, grp],
        compiler_params=_cparams(("parallel",)),
        name="in_proj",
    )(x, mod, g_mix, w_in)


LANE_TILE = 128
SCAN_CHUNK = 256
SEG_LEN = SCAN_CHUNK // SUBLANES
SCAN_TILE_GROUP = 4


def _cmul(ar, ai, br, bi):
    return ar * br - ai * bi, ar * bi + ai * br


def _ssm_prompt_kernel(u_ref, pm_ref, pmt_ref, ab_ref, bre_ref, bim_ref, cre_ref, cim_ref, d_ref,
                       y_ref, hre_ref, him_ref, bur, bui, tab, carry):
    c = pl.program_id(1)
    half = SSM_LANES // 2
    tiles_per_half = half // LANE_TILE

    @pl.when(c == 0)
    def _():
        shp = (SUBLANES, SSM_LANES)
        sub = lax.broadcasted_iota(jnp.int32, shp, 0)
        b_r = jnp.broadcast_to(ab_ref[0:1, :], shp)
        b_i = jnp.broadcast_to(ab_ref[1:2, :], shp)
        for _ in range(SEG_LEN.bit_length() - 1):
            b_r, b_i = _cmul(b_r, b_i, b_r, b_i)
        p_r, p_i = b_r, b_i
        ps_r, ps_i = jnp.ones(shp, F32), jnp.zeros(shp, F32)
        for s in range(1, SUBLANES):
            if s in (1, 2, 4):
                k = {1: 0, 2: 2, 4: 4}[s]
                tab[k] = jnp.where(sub >= s, p_r, 0.0)
                tab[k + 1] = jnp.where(sub >= s, p_i, 0.0)
            ps_r = jnp.where(sub >= s, p_r, ps_r)
            ps_i = jnp.where(sub >= s, p_i, ps_i)
            p_r, p_i = _cmul(p_r, p_i, b_r, b_i)
        tab[6] = ps_r
        tab[7] = ps_i
        carry[...] = jnp.zeros_like(carry)

    u = u_ref[...]
    ub = _dot(pm_ref[...], u.astype(BF16)).astype(BF16)
    for h in range(2):
        uh = ub[:, h * 256:(h + 1) * 256]
        res_r = _dot(uh, bre_ref[h])
        res_i = _dot(uh, bim_ref[h])
        for j in range(tiles_per_half):
            bur[h * tiles_per_half + j] = res_r[:, j * LANE_TILE:(j + 1) * LANE_TILE]
            bui[h * tiles_per_half + j] = res_i[:, j * LANE_TILE:(j + 1) * LANE_TILE]

    vshape = (SUBLANES, LANE_TILE)
    sub1 = lax.broadcasted_iota(jnp.int32, vshape, 0) >= 1
    for g0 in range(0, SSM_LANES // LANE_TILE, SCAN_TILE_GROUP):
        tiles = list(range(g0, g0 + SCAN_TILE_GROUP))
        lanes = [slice(lt * LANE_TILE, (lt + 1) * LANE_TILE) for lt in tiles]
        a_r = [jnp.broadcast_to(ab_ref[0:1, sl], vshape) for sl in lanes]
        a_i = [jnp.broadcast_to(ab_ref[1:2, sl], vshape) for sl in lanes]

        def step(t, hs, store):
            out = []
            for j, lt in enumerate(tiles):
                rows = pl.ds(pl.multiple_of(t * SUBLANES, SUBLANES), SUBLANES)
                hr, hi = hs[2 * j], hs[2 * j + 1]
                nr = (a_r[j] * hr - a_i[j] * hi) + bur[lt, rows, :]
                ni = (a_r[j] * hi + a_i[j] * hr) + bui[lt, rows, :]
                if store:
                    bur[lt, rows, :] = nr
                    bui[lt, rows, :] = ni
                out += [nr, ni]
            return tuple(out)

        zeros = tuple(jnp.zeros(vshape, F32) for _ in range(2 * SCAN_TILE_GROUP))
        ends = lax.fori_loop(0, SEG_LEN, lambda t, hs: step(t, hs, False), zeros, unroll=4)

        starts = []
        for j, sl in enumerate(lanes):
            fr, fi = ends[2 * j], ends[2 * j + 1]
            for k in (1, 2, 4):
                ti = {1: 0, 2: 2, 4: 4}[k]
                mr, mi = tab[ti, :, sl], tab[ti + 1, :, sl]
                rr = pltpu.roll(fr, k, axis=0)
                ri = pltpu.roll(fi, k, axis=0)
                fr, fi = fr + (mr * rr - mi * ri), fi + (mr * ri + mi * rr)
            cr, ci = carry[0, :, sl], carry[1, :, sl]
            ps_r, ps_i = tab[6, :, sl], tab[7, :, sl]
            sr = (ps_r * cr - ps_i * ci) + jnp.where(sub1, pltpu.roll(fr, 1, axis=0), 0.0)
            si = (ps_r * ci + ps_i * cr) + jnp.where(sub1, pltpu.roll(fi, 1, axis=0), 0.0)
            starts += [sr, si]

        last = lax.fori_loop(0, SEG_LEN, lambda t, hs: step(t, hs, True), tuple(starts), unroll=4)
        for j, sl in enumerate(lanes):
            carry[0, :, sl] = jnp.broadcast_to(last[2 * j][SUBLANES - 1:SUBLANES, :], vshape)
            carry[1, :, sl] = jnp.broadcast_to(last[2 * j + 1][SUBLANES - 1:SUBLANES, :], vshape)

    for h in range(2):
        cs = slice(h * 256, (h + 1) * 256)
        tl = range(h * tiles_per_half, (h + 1) * tiles_per_half)
        hr = jnp.concatenate([bur[lt] for lt in tl], axis=-1).astype(BF16)
        hi = jnp.concatenate([bui[lt] for lt in tl], axis=-1).astype(BF16)
        y = _dot(hr, cre_ref[h]) - _dot(hi, cim_ref[h])
        y_hi = y.astype(BF16)
        y_lo = (y - y_hi.astype(F32)).astype(BF16)
        y = _dot(pmt_ref[...], y_hi) + _dot(pmt_ref[...], y_lo)
        y_ref[:, cs] = y + d_ref[:, cs] * u[:, cs]

    @pl.when(c == pl.num_programs(1) - 1)
    def _():
        hre_ref[...] = carry[0]
        him_ref[...] = carry[1]


def _ssm_prompt_call(u, ab, bre, bim, cre, cim, d_skip, n_seq, seq):
    chunks = seq // SCAN_CHUNK
    half = SSM_LANES // 2
    wspec_b = pl.BlockSpec((2, 256, half), lambda n, c: (0, 0, 0))
    wspec_c = pl.BlockSpec((2, half, 256), lambda n, c: (0, 0, 0))
    st_spec = pl.BlockSpec((None, SUBLANES, SSM_LANES), lambda n, c: (n, 0, 0))
    st_shape = jax.ShapeDtypeStruct((n_seq, SUBLANES, SSM_LANES), F32)
    r = jnp.arange(SCAN_CHUNK)
    pm = (r[None, :] == ((r % SUBLANES) * SEG_LEN + r // SUBLANES)[:, None]).astype(BF16)
    perm_spec = pl.BlockSpec((SCAN_CHUNK, SCAN_CHUNK), lambda n, c: (0, 0))
    return pl.pallas_call(
        _ssm_prompt_kernel,
        grid=(n_seq, chunks),
        in_specs=[
            pl.BlockSpec((SCAN_CHUNK, SSM_WIDTH), lambda n, c: (n * chunks + c, 0)),
            perm_spec, perm_spec,
            pl.BlockSpec((2, SSM_LANES), lambda n, c: (0, 0)),
            wspec_b, wspec_b, wspec_c, wspec_c,
            pl.BlockSpec((1, SSM_WIDTH), lambda n, c: (0, 0)),
        ],
        out_specs=[pl.BlockSpec((SCAN_CHUNK, SSM_WIDTH), lambda n, c: (n * chunks + c, 0)),
                   st_spec, st_spec],
        out_shape=[jax.ShapeDtypeStruct((n_seq * seq, SSM_WIDTH), F32), st_shape, st_shape],
        scratch_shapes=[pltpu.VMEM((SSM_LANES // LANE_TILE, SCAN_CHUNK, LANE_TILE), F32),
                        pltpu.VMEM((SSM_LANES // LANE_TILE, SCAN_CHUNK, LANE_TILE), F32),
                        pltpu.VMEM((8, SUBLANES, SSM_LANES), F32),
                        pltpu.VMEM((2, SUBLANES, SSM_LANES), F32)],
        compiler_params=_cparams(("parallel", "arbitrary")),
        name="ssm_prompt",
    )(u, pm, pm.T, ab, bre, bim, cre, cim, d_skip)


def _ssm_sample_kernel(u_ref, ab_ref, bre_ref, bim_ref, cre_ref, cim_ref, d_ref, h0r_ref, h0i_ref,
                       y_ref, hr_ref, hi_ref, *, steps):
    half = SSM_LANES // 2
    ar, ai = ab_ref[0:1, :], ab_ref[1:2, :]
    hr_ref[...] = h0r_ref[...]
    hi_ref[...] = h0i_ref[...]
    for t in range(steps):
        u = u_ref[:, t * SSM_WIDTH:(t + 1) * SSM_WIDTH]
        ub = u.astype(BF16)
        for h in range(2):
            hs = slice(h * half, (h + 1) * half)
            cs = slice(h * 256, (h + 1) * 256)
            uh = ub[:, cs]
            hr, hi = hr_ref[:, hs], hi_ref[:, hs]
            a_r, a_i = ar[:, hs], ai[:, hs]
            nr = (a_r * hr - a_i * hi) + _dot(uh, bre_ref[h])
            ni = (a_r * hi + a_i * hr) + _dot(uh, bim_ref[h])
            hr_ref[:, hs] = nr
            hi_ref[:, hs] = ni
            y = _dot(nr.astype(BF16), cre_ref[h]) - _dot(ni.astype(BF16), cim_ref[h])
            y_ref[:, t * SSM_WIDTH + h * 256:t * SSM_WIDTH + (h + 1) * 256] = y + d_ref[:, cs] * u[:, cs]


def _ssm_sample_call(u, ab, bre, bim, cre, cim, d_skip, h0r, h0i, steps):
    n = u.shape[0]
    st = jax.ShapeDtypeStruct((n, SSM_LANES), F32)
    return pl.pallas_call(
        functools.partial(_ssm_sample_kernel, steps=steps),
        out_shape=[jax.ShapeDtypeStruct((n, steps * SSM_WIDTH), F32), st, st],
        compiler_params=pltpu.CompilerParams(vmem_limit_bytes=VMEM_LIMIT),
        name="ssm_sample",
    )(u, ab, bre, bim, cre, cim, d_skip, h0r, h0i)


_NT = (((1,), (1,)), ((), ()))


def _attn_prompt_kernel(q_ref, kp_ref, kc_ref, vp_ref, vc_ref, o_ref, lse_ref, *, dil, q_blocks):
    b = pl.program_id(2)
    qi = lax.broadcasted_iota(jnp.int32, (N_BACK, 2 * N_BACK), 0)
    kj = lax.broadcasted_iota(jnp.int32, (N_BACK, 2 * N_BACK), 1)
    band = (kj >= qi) & (kj <= qi + N_BACK)
    band_first = band & (kj >= jnp.where(b > 0, 0, N_BACK))
    lane = lax.broadcasted_iota(jnp.int32, (N_BACK, PAIR_WIDTH), 1)
    head0 = lane < HEAD_DIM

    def rows(j, r):
        start = j * N_BACK * dil + r
        return pl.ds(start, N_BACK, stride=dil) if dil > 1 else pl.ds(start, N_BACK)

    def one(j, r):
        cur = rows(j, r)
        q = (q_ref[cur, :] * (HEAD_DIM ** -0.5)).astype(BF16)
        k_prev = kp_ref[rows(0, r), :] if j == 0 else kc_ref[rows(j - 1, r), :]
        v_prev = vp_ref[rows(0, r), :] if j == 0 else vc_ref[rows(j - 1, r), :]
        kk = jnp.concatenate([k_prev, kc_ref[cur, :]], axis=0).astype(BF16)
        vv = jnp.concatenate([v_prev, vc_ref[cur, :]], axis=0).astype(BF16)
        valid = band_first if j == 0 else band
        outs, lses = [], []
        for sub in range(2):
            in_head = head0 if sub == 0 else jnp.logical_not(head0)
            qm = jnp.where(in_head, q, jnp.zeros_like(q))
            s = lax.dot_general(qm, kk, _NT, preferred_element_type=F32)
            s = jnp.where(valid, s, NEG_INF)
            m = jnp.max(s, axis=-1, keepdims=True)
            p = jnp.exp(s - m)
            den = jnp.sum(p, axis=-1, keepdims=True)
            outs.append(_dot(p.astype(BF16), vv) / den)
            lses.append(jnp.broadcast_to(m + jnp.log(den), (N_BACK, PAIR_WIDTH)))
        o_ref[cur, :] = jnp.where(head0, outs[0], outs[1])
        lse_ref[cur, :] = jnp.where(head0, lses[0], lses[1])

    def residue(r, carry):
        for j in range(q_blocks):
            one(j, r)
        return carry

    if dil <= ATTN_UNROLL:
        for r in range(dil):
            residue(r, 0)
    else:
        lax.fori_loop(0, dil, residue, 0, unroll=ATTN_UNROLL)


ATTN_UNROLL = 4
ATTN_Q_BLOCKS = {1: 4, 4: 1, 16: 1}


def _attn_prompt_call(qkv, n_seq, seq, dil):
    q_blocks = ATTN_Q_BLOCKS[dil]
    prev_rows = N_BACK * dil
    rows = prev_rows * q_blocks
    nbk = seq // rows
    cur = lambda a: pl.BlockSpec((None, None, rows, PAIR_WIDTH), lambda n, hp, b: (a, hp, n * nbk + b, 0))
    prev = lambda a: pl.BlockSpec(
        (None, None, prev_rows, PAIR_WIDTH),
        lambda n, hp, b: (a, hp, jnp.maximum((n * nbk + b) * q_blocks - 1, n * nbk * q_blocks), 0))
    out_spec = pl.BlockSpec((None, rows, PAIR_WIDTH), lambda n, hp, b: (hp, n * nbk + b, 0))
    out_shape = jax.ShapeDtypeStruct((2, n_seq * seq, PAIR_WIDTH), F32)
    return pl.pallas_call(
        functools.partial(_attn_prompt_kernel, dil=dil, q_blocks=q_blocks),
        grid=(n_seq, 2, nbk),
        in_specs=[cur(0), prev(1), cur(1), prev(2), cur(2)],
        out_specs=[out_spec, out_spec],
        out_shape=[out_shape, out_shape],
        compiler_params=_cparams(("parallel", "parallel", "parallel")),
        name=f"attn_prompt_d{dil}",
    )(qkv, qkv, qkv, qkv, qkv)


def _kv_tail_kernel(x_ref, o_ref):
    o_ref[...] = x_ref[...].T


def _kv_tail_call(qkv, n_seq, seq, keep):
    blocks_per_seq = seq // keep
    return pl.pallas_call(
        _kv_tail_kernel,
        grid=(2, n_seq, 2),
        in_specs=[pl.BlockSpec((None, None, keep, PAIR_WIDTH),
                               lambda a, n, hp: (a + 1, hp, (n + 1) * blocks_per_seq - 1, 0))],
        out_specs=pl.BlockSpec((None, None, None, PAIR_WIDTH, keep), lambda a, n, hp: (a, n, hp, 0, 0)),
        out_shape=jax.ShapeDtypeStruct((2, n_seq, 2, PAIR_WIDTH, keep), F32),
        compiler_params=_cparams(("parallel", "parallel", "parallel")),
        name="kv_tail",
    )(qkv)


SAMPLE_SEQ_BLOCK = {128: 4, 512: 4, 2048: 2}


def _attn_sample_kernel(qkv_ref, kt_ref, vt_ref, o_ref, lse_ref, *, dil, steps):
    nb, _, _, w = kt_ref.shape
    rows = nb * steps
    row = lax.broadcasted_iota(jnp.int32, (rows, w), 0)
    pos = lax.broadcasted_iota(jnp.int32, (rows, w), 1)
    rown = lax.broadcasted_iota(jnp.int32, (rows, rows), 0)
    coln = lax.broadcasted_iota(jnp.int32, (rows, rows), 1)
    rowo = lax.broadcasted_iota(jnp.int32, (rows, HEAD_DIM), 0)
    for hp in range(2):
        q2 = qkv_ref[0, hp] * (HEAD_DIM ** -0.5)
        k2 = qkv_ref[1, hp]
        v2 = qkv_ref[2, hp]
        o_heads, lse_heads = [], []
        for sub in range(2):
            ls = slice(sub * HEAD_DIM, (sub + 1) * HEAD_DIM)
            h = hp * 2 + sub
            q = q2[:, ls].astype(BF16)
            kn = k2[:, ls].astype(BF16)
            vn = v2[:, ls].astype(BF16)
            o_h = jnp.zeros((rows, HEAD_DIM), F32)
            lse_h = jnp.zeros((rows, HEAD_DIM), F32)
            for n in range(nb):
                t = row - n * steps
                in_buf = (((pos - t) & (dil - 1)) == 0) & (pos >= t)
                tn = rown - n * steps
                sn = coln - n * steps
                in_new = ((sn >= 0) & (sn <= tn)) if dil == 1 else (sn == tn)
                s = jnp.where(in_buf, _dot(q, kt_ref[n, h].astype(BF16)), NEG_INF)
                s2 = jnp.where(in_new, lax.dot_general(q, kn, _NT, preferred_element_type=F32), NEG_INF)
                m = jnp.maximum(jnp.max(s, axis=-1, keepdims=True), jnp.max(s2, axis=-1, keepdims=True))
                p = jnp.exp(s - m)
                p2 = jnp.exp(s2 - m)
                den = jnp.sum(p, axis=-1, keepdims=True) + jnp.sum(p2, axis=-1, keepdims=True)
                o = lax.dot_general(p.astype(BF16), vt_ref[n, h].astype(BF16), _NT,
                                    preferred_element_type=F32)
                o = (o + _dot(p2.astype(BF16), vn)) / den
                mine = (rowo >= n * steps) & (rowo < (n + 1) * steps)
                o_h = jnp.where(mine, o, o_h)
                lse_h = jnp.where(mine, m + jnp.log(den), lse_h)
            o_heads.append(o_h)
            lse_heads.append(lse_h)
        o_ref[hp] = jnp.concatenate(o_heads, axis=-1)
        lse_ref[hp] = jnp.concatenate(lse_heads, axis=-1)


def _attn_sample_call(qkv, k_cache, v_cache, l, n_seq, steps, dil):
    w = k_cache.shape[2]
    kt = k_cache.transpose(0, 1, 3, 4, 2)
    vt = v_cache.transpose(0, 1, 3, 4, 2)
    nb = SAMPLE_SEQ_BLOCK[w]
    rows = nb * steps
    cache_spec = pl.BlockSpec((None, nb, HEADS, HEAD_DIM, w), lambda i: (l, i, 0, 0, 0))
    out_spec = pl.BlockSpec((2, rows, PAIR_WIDTH), lambda i: (0, i, 0))
    out_shape = jax.ShapeDtypeStruct((2, n_seq * steps, PAIR_WIDTH), F32)
    return pl.pallas_call(
        functools.partial(_attn_sample_kernel, dil=dil, steps=steps),
        grid=(n_seq // nb,),
        in_specs=[pl.BlockSpec((3, 2, rows, PAIR_WIDTH), lambda i: (0, 0, i, 0)), cache_spec, cache_spec],
        out_specs=[out_spec, out_spec],
        out_shape=[out_shape, out_shape],
        compiler_params=_cparams(("parallel",)),
        name=f"attn_sample_d{dil}",
    )(qkv, kt, vt)


def _merge_kernel(x_ref, mod_ref, y_ref, o0_ref, l0_ref, o1_ref, l1_ref, o2_ref, l2_ref,
                  gmix_ref, gffn_ref, wmg_ref, bmg_ref, wglu_ref, wbs_ref, wba_ref, wout_ref,
                  wrh_ref, wrl_ref, br_ref, x1_ref, h2_ref, comb_ref):
    x = x_ref[...]
    tm = x.shape[0]
    hb = (_rmsnorm(x, gmix_ref[...]) * (1.0 + mod_ref[1]) + mod_ref[0]).astype(BF16)

    y = _gelu_tanh(y_ref[...])
    yg = y * _sigmoid(_dot(y.astype(BF16), wglu_ref[...]))
    ssm_branch = _dot(yg.astype(BF16), wbs_ref[...])

    pairs = []
    for hp in range(2):
        l0, l1, l2 = l0_ref[hp], l1_ref[hp], l2_ref[hp]
        lm = jnp.maximum(jnp.maximum(l0, l1), l2)
        e0, e1, e2 = jnp.exp(l0 - lm), jnp.exp(l1 - lm), jnp.exp(l2 - lm)
        pairs.append((e0 * o0_ref[hp] + e1 * o1_ref[hp] + e2 * o2_ref[hp]) / (e0 + e1 + e2))
    attn = jnp.concatenate(pairs, axis=-1)
    attn_branch = _dot(attn.astype(BF16), wba_ref[...])

    g_ssm = _sigmoid(_dot(hb, wmg_ref[:, :D_MODEL]) + bmg_ref[:, :D_MODEL])
    g_attn = _sigmoid(_dot(hb, wmg_ref[:, D_MODEL:]) + bmg_ref[:, D_MODEL:])
    merged = g_ssm * ssm_branch + g_attn * attn_branch
    x1 = x + mod_ref[2] * _dot(merged.astype(BF16), wout_ref[...])
    x1_ref[...] = x1

    h2 = _rmsnorm(x1, gffn_ref[...]) * (1.0 + mod_ref[4]) + mod_ref[3]
    h2_hi = h2.astype(BF16)
    h2_ref[...] = h2_hi
    h2_lo = (h2 - h2_hi.astype(F32)).astype(BF16)
    logits = (_dot(h2_hi, wrh_ref[...]) + _dot(h2_hi, wrl_ref[...]) + _dot(h2_lo, wrh_ref[...])
              + br_ref[...])

    lane = lax.broadcasted_iota(jnp.int32, (tm, ROUTER_LANES), 1)
    lanef = lane.astype(F32)
    low = jnp.float32(-3.0e38)
    is_grp = (lane >= GROUP_LOGIT_LANE0) & (lane < GROUP_LOGIT_LANE0 + N_EXPERT_GROUPS)
    lg = jnp.where(is_grp, logits, low)
    mg = jnp.max(lg, axis=-1, keepdims=True)
    gsel = jnp.min(jnp.where(lg == mg, lanef - GROUP_LOGIT_LANE0, 1e9), axis=-1, keepdims=True)
    pg_sel = 1.0 / jnp.sum(jnp.exp(lg - mg), axis=-1, keepdims=True)
    in_grp = (lane < N_EXPERTS) & ((lane // EXPERTS_PER_GROUP).astype(F32) == gsel)
    le = jnp.where(in_grp, logits, low)
    m1 = jnp.max(le, axis=-1, keepdims=True)
    i1 = jnp.min(jnp.where(le == m1, lanef, 1e9), axis=-1, keepdims=True)
    le2 = jnp.where(lanef == i1, low, le)
    m2 = jnp.max(le2, axis=-1, keepdims=True)
    i2 = jnp.min(jnp.where(le2 == m2, lanef, 1e9), axis=-1, keepdims=True)
    r = jnp.exp(m2 - m1)
    w1 = 1.0 / (1.0 + r)
    w2 = r / (1.0 + r)
    comb = pg_sel * (jnp.where(lanef == i1, w1, 0.0) + jnp.where(lanef == i2, w2, 0.0))
    comb_ref[...] = comb + jnp.where(lane == GROUP_ID_LANE, gsel, 0.0)


def _merge_call(x, mod, y, attn_outs, p, l, tm, tiles_per_seq):
    ntok = x.shape[0]
    tok = lambda w: pl.BlockSpec((tm, w), lambda i: (i, 0))
    lay = lambda *shape: pl.BlockSpec((None,) + shape, lambda i: (l,) + (0,) * len(shape))
    attn_specs = [pl.BlockSpec((2, tm, PAIR_WIDTH), lambda i: (0, i, 0))] * 6
    return pl.pallas_call(
        _merge_kernel,
        grid=(ntok // tm,),
        in_specs=[tok(D_MODEL), _mod_spec(mod, tm, tiles_per_seq), tok(SSM_WIDTH)] + attn_specs + [
            lay(1, D_MODEL), lay(1, D_MODEL),
            lay(D_MODEL, 2 * D_MODEL), lay(1, 2 * D_MODEL),
            lay(SSM_WIDTH, SSM_WIDTH), lay(SSM_WIDTH, D_MODEL), lay(GROUP_WIDTH, D_MODEL),
            lay(D_MODEL, D_MODEL),
            lay(D_MODEL, ROUTER_LANES), lay(D_MODEL, ROUTER_LANES), lay(1, ROUTER_LANES),
        ],
        out_specs=[tok(D_MODEL), tok(D_MODEL), tok(ROUTER_LANES)],
        out_shape=[jax.ShapeDtypeStruct((ntok, D_MODEL), F32),
                   jax.ShapeDtypeStruct((ntok, D_MODEL), BF16),
                   jax.ShapeDtypeStruct((ntok, ROUTER_LANES), F32)],
        compiler_params=_cparams(("parallel",)),
        name="merge_router",
    )(x, mod, y, *attn_outs, p['g_mix'], p['g_ffn'], p['w_mgate'], p['b_mgate'], p['w_glu'],
      p['w_br_ssm'], p['w_br_attn'], p['w_out'], p['w_router_hi'], p['w_router_lo'], p['b_router'])


MOE_EXPERT_BLOCK = 4
MOE_ROW_CHUNK = 128
GROUP_ID_LANE = 64
_TN = (((0,), (0,)), ((), ()))


def _split3(x):
    a = x.astype(BF16)
    r = x - a.astype(F32)
    b = r.astype(BF16)
    return a, b, (r - b.astype(F32)).astype(BF16)


def _moe_kernel(h2_ref, comb_ref, x1_ref, mod_ref, wg_ref, wu_ref, wd_ref, gfin_ref, out_ref,
                perm, hs, combs, ys, bounds, *, final):
    s = pl.program_id(1)
    tm = h2_ref.shape[0]
    ch = MOE_ROW_CHUNK

    @pl.when(s == 0)
    def _():
        comb = comb_ref[...]
        lane = lax.broadcasted_iota(jnp.int32, (tm, ROUTER_LANES), 1)
        gid = jnp.sum(jnp.where(lane == GROUP_ID_LANE, comb, 0.0), axis=-1, keepdims=True)
        onehot = jnp.where(lane.astype(F32) == gid, 1.0, 0.0)
        row = lax.broadcasted_iota(jnp.int32, (tm, tm), 0)
        col = lax.broadcasted_iota(jnp.int32, (tm, tm), 1)
        earlier = jnp.where(col < row, 1.0, 0.0).astype(BF16)
        rank = _dot(earlier, onehot.astype(BF16))
        tot = jnp.sum(onehot, axis=0, keepdims=True)
        lane1 = lax.broadcasted_iota(jnp.int32, (1, ROUTER_LANES), 1)
        start = jnp.zeros((1, ROUTER_LANES), F32)
        run = jnp.zeros((1, 1), F32)
        for g in range(N_EXPERT_GROUPS):
            bounds[2 * g] = run[0, 0].astype(jnp.int32)
            start = start + jnp.where(lane1 == g, run, 0.0)
            run = run + jnp.sum(jnp.where(lane1 == g, tot, 0.0), axis=-1, keepdims=True)
            bounds[2 * g + 1] = run[0, 0].astype(jnp.int32)
        pos = jnp.sum(onehot * (rank + start), axis=-1, keepdims=True)
        perm[...] = jnp.where(col.astype(F32) == pos, 1.0, 0.0).astype(BF16)
        hs[...] = lax.dot_general(perm[...], h2_ref[...], _TN, preferred_element_type=F32).astype(BF16)
        parts = lax.dot_general(perm[...], jnp.concatenate(_split3(comb), axis=-1), _TN,
                                preferred_element_type=F32)
        combs[...] = (parts[:, :ROUTER_LANES] + parts[:, ROUTER_LANES:2 * ROUTER_LANES]) + parts[:, 2 * ROUTER_LANES:]
        ys[...] = jnp.zeros_like(ys)

    g = s // (EXPERTS_PER_GROUP // MOE_EXPERT_BLOCK)
    lo = bounds[2 * g] // ch
    hi = (bounds[2 * g + 1] + (ch - 1)) // ch
    lane = lax.broadcasted_iota(jnp.int32, (ch, ROUTER_LANES), 1)

    def chunk(c, carry):
        r0 = pl.multiple_of(c * ch, ch)
        h = hs[pl.ds(r0, ch), :]
        comb = combs[pl.ds(r0, ch), :]
        hids = []
        for j in range(MOE_EXPERT_BLOCK):
            gt = _dot(h, wg_ref[j])
            w = jnp.sum(jnp.where(lane == s * MOE_EXPERT_BLOCK + j, comb, 0.0), axis=-1, keepdims=True)
            hids.append((gt * _sigmoid(gt) * _dot(h, wu_ref[j]) * w).astype(BF16))
        ys[pl.ds(r0, ch), :] += _dot(jnp.concatenate(hids, axis=-1), wd_ref[...])
        return carry

    lax.fori_loop(lo, hi, chunk, 0)

    @pl.when(s == pl.num_programs(1) - 1)
    def _():
        y = ys[...]
        y_hi = y.astype(BF16)
        y_lo = (y - y_hi.astype(F32)).astype(BF16)
        p = perm[...]
        x2 = x1_ref[...] + mod_ref[5] * (_dot(p, y_hi) + _dot(p, y_lo))
        out_ref[...] = _rmsnorm(x2, gfin_ref[...]) if final else x2


def _moe_call(h2, comb, x1, mod, p, l, tm, tiles_per_seq, final):
    ntok = h2.shape[0]
    eb = MOE_EXPERT_BLOCK
    tok = lambda w: pl.BlockSpec((tm, w), lambda i, s: (i, 0))
    if mod.ndim == 4:
        mod_spec = pl.BlockSpec((None, 6, 1, D_MODEL), lambda i, s: (i // tiles_per_seq, 0, 0, 0))
    else:
        mod_spec = pl.BlockSpec((6, tm, D_MODEL), lambda i, s: (0, i, 0))
    w_down = p['w_exp_down'].reshape(DEPTH, N_EXPERTS // eb, eb * EXPERT_FF, D_MODEL)
    return pl.pallas_call(
        functools.partial(_moe_kernel, final=final),
        grid=(ntok // tm, N_EXPERTS // eb),
        in_specs=[tok(D_MODEL), tok(ROUTER_LANES), tok(D_MODEL), mod_spec,
                  pl.BlockSpec((None, eb, D_MODEL, EXPERT_FF), lambda i, s: (l, s, 0, 0)),
                  pl.BlockSpec((None, eb, D_MODEL, EXPERT_FF), lambda i, s: (l, s, 0, 0)),
                  pl.BlockSpec((None, None, eb * EXPERT_FF, D_MODEL), lambda i, s: (l, s, 0, 0)),
                  pl.BlockSpec((1, D_MODEL), lambda i, s: (0, 0))],
        out_specs=tok(D_MODEL),
        out_shape=jax.ShapeDtypeStruct((ntok, D_MODEL), F32),
        scratch_shapes=[pltpu.VMEM((tm, tm), BF16), pltpu.VMEM((tm, D_MODEL), BF16),
                        pltpu.VMEM((tm, ROUTER_LANES), F32), pltpu.VMEM((tm, D_MODEL), F32),
                        pltpu.SMEM((2 * N_EXPERT_GROUPS,), jnp.int32)],
        compiler_params=_cparams(("parallel", "arbitrary")),
        name="moe_experts",
    )(h2, comb, x1, mod, p['w_exp_gate'], p['w_exp_up'], w_down, p['g_final'])


def _run_trunk(x, mod_layers, p, ssm, n_seq, seq, caches, h0):
    ntok = n_seq * seq
    prompt = caches is None
    tm_proj = 512 if prompt else 256
    tm_merge = 512 if prompt else 256
    tm_moe = 1024 if prompt else 256
    states = []
    for l in range(DEPTH):
        mod = mod_layers[l]
        u, *qkv = _inproj_call(x, mod, p['g_mix'], p['w_in'], l, tm_proj, seq // tm_proj if prompt else 1)
        ab, bre, bim, cre, cim, d_skip = ssm[l]
        if prompt:
            y, hre, him = _ssm_prompt_call(u, ab, bre, bim, cre, cim, d_skip, n_seq, seq)
            hre, him = hre[:, 0], him[:, 0]
            attn = []
            for g in range(N_GROUPS):
                attn += list(_attn_prompt_call(qkv[g], n_seq, seq, DILATIONS[g]))
        else:
            y, hre, him = _ssm_sample_call(u.reshape(n_seq, seq * SSM_WIDTH), ab, bre, bim, cre, cim,
                                           d_skip, h0[0][l], h0[1][l], seq)
            y = y.reshape(ntok, SSM_WIDTH)
            attn = []
            for g in range(N_GROUPS):
                attn += list(_attn_sample_call(qkv[g], caches[2 * g], caches[2 * g + 1], l, n_seq, seq,
                                               DILATIONS[g]))
        x1, h2, comb = _merge_call(x, mod, y, attn, p, l, tm_merge, seq // tm_merge if prompt else 1)
        x = _moe_call(h2, comb, x1, mod, p, l, tm_moe, seq // tm_moe if prompt else 1,
                      final=(l == DEPTH - 1))
        kv = []
        for g in range(N_GROUPS):
            keep = min(WINDOWS[g], seq)
            if prompt:
                tails = _kv_tail_call(qkv[g], n_seq, seq, keep).reshape(2, n_seq, HEADS, HEAD_DIM, keep)
                kv += [tails[0].transpose(0, 3, 1, 2), tails[1].transpose(0, 3, 1, 2)]
            else:
                for a in (1, 2):
                    new = qkv[g][a].reshape(2, n_seq, seq, 2, HEAD_DIM)
                    kv.append(new.transpose(1, 2, 0, 3, 4).reshape(n_seq, seq, HEADS, HEAD_DIM))
        states.append(kv + [hre.reshape(n_seq, N_SSM_GROUPS, SSM_STATE),
                            him.reshape(n_seq, N_SSM_GROUPS, SSM_STATE)])
    new_state = [jnp.stack([states[l][i] for l in range(DEPTH)], axis=0) for i in range(len(states[0]))]
    return x, new_state


def kernel(x_prompt, x_sample, cache_k_w128, cache_v_w128, cache_k_w512, cache_v_w512,
           cache_k_w2048, cache_v_w2048, state_ssm_re, state_ssm_im, c_prompt, c_sample,
           w_ada, b_ada, g_norm_mix, g_norm_ffn, g_final, w_in, ssm_a_re, ssm_a_im, ssm_b_re,
           ssm_b_im, ssm_c_re, ssm_c_im, ssm_log_step, ssm_d, w_glu, w_br_ssm, w_br_attn,
           w_mgate, b_mgate, w_out, w_router_grp, b_router_grp, w_router_exp, b_router_exp,
           w_exp_gate, w_exp_up, w_exp_down):
    batch, seq, _ = x_prompt.shape
    dec_batch, dec_seq, _ = x_sample.shape

    n_c = batch + dec_batch
    rows = -(-n_c // SUBLANES) * SUBLANES
    c_all = jnp.concatenate([c_prompt, c_sample, jnp.zeros((rows - n_c, D_MODEL), F32)], axis=0)
    mod_all = _ada_call(c_all, w_ada, b_ada)
    mod_prompt = [mod_all[l, :batch].reshape(batch, 6, 1, D_MODEL) for l in range(DEPTH)]
    mod_sample = [jnp.repeat(mod_all[l, batch:n_c].reshape(dec_batch, 6, D_MODEL), dec_seq, axis=0)
                  .transpose(1, 0, 2) for l in range(DEPTH)]

    abr, abi, bbr, bbi = _ssm_param_call(ssm_a_re, ssm_a_im, ssm_log_step, ssm_b_re, ssm_b_im)
    ssm = []
    for l in range(DEPTH):
        pick = lambda a: a[l].reshape(N_SSM_GROUPS, SSM_STATE, SSM_GROUP_CH)
        ab = jnp.stack([pick(abr)[:, :, 0].reshape(SSM_LANES), pick(abi)[:, :, 0].reshape(SSM_LANES)])
        bre = _block_diag_halves(pick(bbr).transpose(0, 2, 1), BF16)
        bim = _block_diag_halves(pick(bbi).transpose(0, 2, 1), BF16)
        cre = _block_diag_halves(ssm_c_re[l].transpose(0, 2, 1), BF16)
        cim = _block_diag_halves(ssm_c_im[l].transpose(0, 2, 1), BF16)
        ssm.append((ab, bre, bim, cre, cim, ssm_d[l].reshape(1, SSM_WIDTH)))

    w_router = jnp.concatenate(
        [w_router_exp, w_router_grp,
         jnp.zeros((DEPTH, D_MODEL, ROUTER_LANES - N_EXPERTS - N_EXPERT_GROUPS), F32)], axis=-1)
    b_router = jnp.concatenate(
        [b_router_exp, b_router_grp,
         jnp.zeros((DEPTH, ROUTER_LANES - N_EXPERTS - N_EXPERT_GROUPS), F32)], axis=-1)
    w_router_hi = w_router.astype(BF16)
    p = dict(
        g_mix=g_norm_mix.reshape(DEPTH, 1, D_MODEL), g_ffn=g_norm_ffn.reshape(DEPTH, 1, D_MODEL),
        g_final=g_final.reshape(1, D_MODEL), w_in=w_in.astype(BF16), w_glu=w_glu.astype(BF16),
        w_br_ssm=w_br_ssm.astype(BF16), w_br_attn=w_br_attn.astype(BF16),
        w_mgate=w_mgate.astype(BF16), b_mgate=b_mgate.reshape(DEPTH, 1, 2 * D_MODEL),
        w_out=w_out.astype(BF16), w_router_hi=w_router_hi,
        w_router_lo=(w_router - w_router_hi.astype(F32)).astype(BF16),
        b_router=b_router.reshape(DEPTH, 1, ROUTER_LANES),
        w_exp_gate=w_exp_gate.astype(BF16), w_exp_up=w_exp_up.astype(BF16),
        w_exp_down=w_exp_down.astype(BF16))

    y_prompt, pstate = _run_trunk(x_prompt.reshape(batch * seq, D_MODEL), mod_prompt, p, ssm,
                                  batch, seq, None, None)
    caches = [cache_k_w128, cache_v_w128, cache_k_w512, cache_v_w512, cache_k_w2048, cache_v_w2048]
    h0 = (state_ssm_re.reshape(DEPTH, dec_batch, SSM_LANES), state_ssm_im.reshape(DEPTH, dec_batch, SSM_LANES))
    y_sample, sstate = _run_trunk(x_sample.reshape(dec_batch * dec_seq, D_MODEL), mod_sample, p, ssm,
                                  dec_batch, dec_seq, caches, h0)
    return (y_prompt.reshape(batch, seq, D_MODEL), y_sample.reshape(dec_batch, dec_seq, D_MODEL),
            *pstate, *sstate)
```

```python
import functools
import math

import jax
import jax.numpy as jnp
from jax import lax
from jax.experimental import pallas as pl
from jax.experimental.pallas import tpu as pltpu

F32 = jnp.float32
BF16 = jnp.bfloat16

D_MODEL = 1024
DEPTH = 2
SSM_WIDTH = 512
SSM_GROUP_CH = 16
N_SSM_GROUPS = 32
SSM_STATE = 64
SSM_LANES = N_SSM_GROUPS * SSM_STATE
HEAD_DIM = 64
HEADS = 4
GROUP_WIDTH = HEADS * HEAD_DIM
PAIR_WIDTH = 2 * HEAD_DIM
WINDOWS = (128, 512, 2048)
DILATIONS = (1, 4, 16)
N_BACK = 128
N_GROUPS = 3
QKV_WIDTH = N_GROUPS * GROUP_WIDTH
IN_PROJ_WIDTH = SSM_WIDTH + 3 * QKV_WIDTH
N_EXPERT_GROUPS = 4
EXPERTS_PER_GROUP = 8
N_EXPERTS = 32
EXPERT_FF = 256
RMS_EPS = 1e-6
NEG_INF = -1e30
ROUTER_LANES = 128
GROUP_LOGIT_LANE0 = N_EXPERTS

SUBLANES = 8
VMEM_LIMIT = 56 * 1024 * 1024


def _cparams(sem):
    return pltpu.CompilerParams(dimension_semantics=sem, vmem_limit_bytes=VMEM_LIMIT)


def _sigmoid(x):
    return 1.0 / (1.0 + jnp.exp(-x))


def _gelu_tanh(x):
    return 0.5 * x * (1.0 + jnp.tanh(math.sqrt(2.0 / math.pi) * (x + 0.044715 * (x * x * x))))


def _dot(a, b):
    return jnp.dot(a, b, preferred_element_type=F32)


def _rmsnorm(x, g):
    ms = jnp.mean(x * x, axis=-1, keepdims=True)
    return x * lax.rsqrt(ms + RMS_EPS) * g


def _ada_kernel(c_ref, w_ref, b_ref, o_ref):
    c = c_ref[...]
    s = (c * _sigmoid(c)).astype(BF16)
    o_ref[...] = _dot(s, w_ref[...].astype(BF16)) + b_ref[...]


def _ada_call(c_all, w_ada, b_ada):
    rows = c_all.shape[0]
    tn = 1536
    return pl.pallas_call(
        _ada_kernel,
        grid=(DEPTH, 6 * D_MODEL // tn),
        in_specs=[
            pl.BlockSpec((rows, D_MODEL), lambda l, j: (0, 0)),
            pl.BlockSpec((None, D_MODEL, tn), lambda l, j: (l, 0, j)),
            pl.BlockSpec((None, 1, tn), lambda l, j: (l, 0, j)),
        ],
        out_specs=pl.BlockSpec((None, rows, tn), lambda l, j: (l, 0, j)),
        out_shape=jax.ShapeDtypeStruct((DEPTH, rows, 6 * D_MODEL), F32),
        compiler_params=_cparams(("parallel", "parallel")),
        name="ada_mod",
    )(c_all, w_ada, b_ada.reshape(DEPTH, 1, 6 * D_MODEL))


def _ssm_param_kernel(ar_ref, ai_ref, ls_ref, br_ref, bi_ref, abr_ref, abi_ref, bbr_ref, bbi_ref):
    lr, li = ar_ref[...], ai_ref[...]
    dt = jnp.exp(ls_ref[...])
    mag = jnp.exp(lr * dt)
    abr = mag * jnp.cos(li * dt)
    abi = mag * jnp.sin(li * dt)
    den = lr * lr + li * li
    er, ei = abr - 1.0, abi
    fr = (er * lr + ei * li) / den
    fi = (ei * lr - er * li) / den
    br, bi = br_ref[...], bi_ref[...]
    abr_ref[...] = abr
    abi_ref[...] = abi
    bbr_ref[...] = fr * br - fi * bi
    bbi_ref[...] = fr * bi + fi * br


def _ssm_param_call(a_re, a_im, log_step, b_re, b_im):
    w = SSM_STATE * SSM_GROUP_CH
    rep = lambda a: jnp.repeat(a, SSM_GROUP_CH, axis=-1)
    big = pl.BlockSpec((None, N_SSM_GROUPS, w), lambda l: (l, 0, 0))
    shp = jax.ShapeDtypeStruct((DEPTH, N_SSM_GROUPS, w), F32)
    return pl.pallas_call(
        _ssm_param_kernel,
        grid=(DEPTH,),
        in_specs=[big, big, pl.BlockSpec((None, N_SSM_GROUPS, 1), lambda l: (l, 0, 0)), big, big],
        out_specs=[big, big, big, big],
        out_shape=[shp, shp, shp, shp],
        compiler_params=_cparams(("parallel",)),
        name="ssm_params",
    )(rep(a_re), rep(a_im), log_step.reshape(DEPTH, N_SSM_GROUPS, 1),
      b_re.reshape(DEPTH, N_SSM_GROUPS, w), b_im.reshape(DEPTH, N_SSM_GROUPS, w))


def _block_diag_halves(blocks, dtype):
    g, r, c = blocks.shape
    half = g // 2
    eye = jnp.eye(half, dtype=bool)

    def one(b):
        m = jnp.where(eye[:, None, :, None], b[:, :, None, :], 0.0)
        return m.reshape(half * r, half * c)

    return jnp.stack([one(blocks[:half]), one(blocks[half:])]).astype(dtype)


def _inproj_kernel(x_ref, mod_ref, g_ref, w_ref, u_ref, a0_ref, a1_ref, a2_ref):
    h = _rmsnorm(x_ref[...], g_ref[...]) * (1.0 + mod_ref[1]) + mod_ref[0]
    hb = h.astype(BF16)
    u_ref[...] = _dot(hb, w_ref[:, :SSM_WIDTH])
    for g, ref in enumerate((a0_ref, a1_ref, a2_ref)):
        for a in range(3):
            c0 = SSM_WIDTH + a * QKV_WIDTH + g * GROUP_WIDTH
            val = _dot(hb, w_ref[:, c0:c0 + GROUP_WIDTH])
            ref[a, 0] = val[:, :PAIR_WIDTH]
            ref[a, 1] = val[:, PAIR_WIDTH:]


def _mod_spec(mod, tm, tiles_per_seq):
    if mod.ndim == 4:
        return pl.BlockSpec((None, 6, 1, D_MODEL), lambda i: (i // tiles_per_seq, 0, 0, 0))
    return pl.BlockSpec((6, tm, D_MODEL), lambda i: (0, i, 0))


def _inproj_call(x, mod, g_mix, w_in, l, tm, tiles_per_seq):
    ntok = x.shape[0]
    grp = jax.ShapeDtypeStruct((3, 2, ntok, PAIR_WIDTH), F32)
    grp_spec = pl.BlockSpec((3, 2, tm, PAIR_WIDTH), lambda i: (0, 0, i, 0))
    return pl.pallas_call(
        _inproj_kernel,
        grid=(ntok // tm,),
        in_specs=[
            pl.BlockSpec((tm, D_MODEL), lambda i: (i, 0)),
            _mod_spec(mod, tm, tiles_per_seq),
            pl.BlockSpec((None, 1, D_MODEL), lambda i: (l, 0, 0)),
            pl.BlockSpec((None, D_MODEL, IN_PROJ_WIDTH), lambda i: (l, 0, 0)),
        ],
        out_specs=[pl.BlockSpec((tm, SSM_WIDTH), lambda i: (i, 0)), grp_spec, grp_spec, grp_spec],
        out_shape=[jax.ShapeDtypeStruct((ntok, SSM_WIDTH), F32), grp, grp, grp],
        compiler_params=_cparams(("parallel",)),
        name="in_proj",
    )(x, mod, g_mix, w_in)


LANE_TILE = 128
SCAN_CHUNK = 256
SEG_LEN = SCAN_CHUNK // SUBLANES
SCAN_TILE_GROUP = 4


def _cmul(ar, ai, br, bi):
    return ar * br - ai * bi, ar * bi + ai * br


def _ssm_prompt_kernel(u_ref, pm_ref, pmt_ref, ab_ref, bre_ref, bim_ref, cre_ref, cim_ref, d_ref,
                       y_ref, hre_ref, him_ref, bur, bui, tab, carry):
    c = pl.program_id(1)
    half = SSM_LANES // 2
    tiles_per_half = half // LANE_TILE

    @pl.when(c == 0)
    def _():
        shp = (SUBLANES, SSM_LANES)
        sub = lax.broadcasted_iota(jnp.int32, shp, 0)
        b_r = jnp.broadcast_to(ab_ref[0:1, :], shp)
        b_i = jnp.broadcast_to(ab_ref[1:2, :], shp)
        for _ in range(SEG_LEN.bit_length() - 1):
            b_r, b_i = _cmul(b_r, b_i, b_r, b_i)
        p_r, p_i = b_r, b_i
        ps_r, ps_i = jnp.ones(shp, F32), jnp.zeros(shp, F32)
        for s in range(1, SUBLANES):
            if s in (1, 2, 4):
                k = {1: 0, 2: 2, 4: 4}[s]
                tab[k] = jnp.where(sub >= s, p_r, 0.0)
                tab[k + 1] = jnp.where(sub >= s, p_i, 0.0)
            ps_r = jnp.where(sub >= s, p_r, ps_r)
            ps_i = jnp.where(sub >= s, p_i, ps_i)
            p_r, p_i = _cmul(p_r, p_i, b_r, b_i)
        tab[6] = ps_r
        tab[7] = ps_i
        carry[...] = jnp.zeros_like(carry)

    u = u_ref[...]
    ub = _dot(pm_ref[...], u.astype(BF16)).astype(BF16)
    for h in range(2):
        uh = ub[:, h * 256:(h + 1) * 256]
        res_r = _dot(uh, bre_ref[h])
        res_i = _dot(uh, bim_ref[h])
        for j in range(tiles_per_half):
            bur[h * tiles_per_half + j] = res_r[:, j * LANE_TILE:(j + 1) * LANE_TILE]
            bui[h * tiles_per_half + j] = res_i[:, j * LANE_TILE:(j + 1) * LANE_TILE]

    vshape = (SUBLANES, LANE_TILE)
    sub1 = lax.broadcasted_iota(jnp.int32, vshape, 0) >= 1
    for g0 in range(0, SSM_LANES // LANE_TILE, SCAN_TILE_GROUP):
        tiles = list(range(g0, g0 + SCAN_TILE_GROUP))
        lanes = [slice(lt * LANE_TILE, (lt + 1) * LANE_TILE) for lt in tiles]
        a_r = [jnp.broadcast_to(ab_ref[0:1, sl], vshape) for sl in lanes]
        a_i = [jnp.broadcast_to(ab_ref[1:2, sl], vshape) for sl in lanes]

        def step(t, hs, store):
            out = []
            for j, lt in enumerate(tiles):
                rows = pl.ds(pl.multiple_of(t * SUBLANES, SUBLANES), SUBLANES)
                hr, hi = hs[2 * j], hs[2 * j + 1]
                nr = (a_r[j] * hr - a_i[j] * hi) + bur[lt, rows, :]
                ni = (a_r[j] * hi + a_i[j] * hr) + bui[lt, rows, :]
                if store:
                    bur[lt, rows, :] = nr
                    bui[lt, rows, :] = ni
                out += [nr, ni]
            return tuple(out)

        zeros = tuple(jnp.zeros(vshape, F32) for _ in range(2 * SCAN_TILE_GROUP))
        ends = lax.fori_loop(0, SEG_LEN, lambda t, hs: step(t, hs, False), zeros, unroll=4)

        starts = []
        for j, sl in enumerate(lanes):
            fr, fi = ends[2 * j], ends[2 * j + 1]
            for k in (1, 2, 4):
                ti = {1: 0, 2: 2, 4: 4}[k]
                mr, mi = tab[ti, :, sl], tab[ti + 1, :, sl]
                rr = pltpu.roll(fr, k, axis=0)
                ri = pltpu.roll(fi, k, axis=0)
                fr, fi = fr + (mr * rr - mi * ri), fi + (mr * ri + mi * rr)
            cr, ci = carry[0, :, sl], carry[1, :, sl]
            ps_r, ps_i = tab[6, :, sl], tab[7, :, sl]
            sr = (ps_r * cr - ps_i * ci) + jnp.where(sub1, pltpu.roll(fr, 1, axis=0), 0.0)
            si = (ps_r * ci + ps_i * cr) + jnp.where(sub1, pltpu.roll(fi, 1, axis=0), 0.0)
            starts += [sr, si]

        last = lax.fori_loop(0, SEG_LEN, lambda t, hs: step(t, hs, True), tuple(starts), unroll=4)
        for j, sl in enumerate(lanes):
            carry[0, :, sl] = jnp.broadcast_to(last[2 * j][SUBLANES - 1:SUBLANES, :], vshape)
            carry[1, :, sl] = jnp.broadcast_to(last[2 * j + 1][SUBLANES - 1:SUBLANES, :], vshape)

    for h in range(2):
        cs = slice(h * 256, (h + 1) * 256)
        tl = range(h * tiles_per_half, (h + 1) * tiles_per_half)
        hr = jnp.concatenate([bur[lt] for lt in tl], axis=-1).astype(BF16)
        hi = jnp.concatenate([bui[lt] for lt in tl], axis=-1).astype(BF16)
        y = _dot(hr, cre_ref[h]) - _dot(hi, cim_ref[h])
        y_hi = y.astype(BF16)
        y_lo = (y - y_hi.astype(F32)).astype(BF16)
        y = _dot(pmt_ref[...], y_hi) + _dot(pmt_ref[...], y_lo)
        y_ref[:, cs] = y + d_ref[:, cs] * u[:, cs]

    @pl.when(c == pl.num_programs(1) - 1)
    def _():
        hre_ref[...] = carry[0]
        him_ref[...] = carry[1]


def _ssm_prompt_call(u, ab, bre, bim, cre, cim, d_skip, n_seq, seq):
    chunks = seq // SCAN_CHUNK
    half = SSM_LANES // 2
    wspec_b = pl.BlockSpec((2, 256, half), lambda n, c: (0, 0, 0))
    wspec_c = pl.BlockSpec((2, half, 256), lambda n, c: (0, 0, 0))
    st_spec = pl.BlockSpec((None, SUBLANES, SSM_LANES), lambda n, c: (n, 0, 0))
    st_shape = jax.ShapeDtypeStruct((n_seq, SUBLANES, SSM_LANES), F32)
    r = jnp.arange(SCAN_CHUNK)
    pm = (r[None, :] == ((r % SUBLANES) * SEG_LEN + r // SUBLANES)[:, None]).astype(BF16)
    perm_spec = pl.BlockSpec((SCAN_CHUNK, SCAN_CHUNK), lambda n, c: (0, 0))
    return pl.pallas_call(
        _ssm_prompt_kernel,
        grid=(n_seq, chunks),
        in_specs=[
            pl.BlockSpec((SCAN_CHUNK, SSM_WIDTH), lambda n, c: (n * chunks + c, 0)),
            perm_spec, perm_spec,
            pl.BlockSpec((2, SSM_LANES), lambda n, c: (0, 0)),
            wspec_b, wspec_b, wspec_c, wspec_c,
            pl.BlockSpec((1, SSM_WIDTH), lambda n, c: (0, 0)),
        ],
        out_specs=[pl.BlockSpec((SCAN_CHUNK, SSM_WIDTH), lambda n, c: (n * chunks + c, 0)),
                   st_spec, st_spec],
        out_shape=[jax.ShapeDtypeStruct((n_seq * seq, SSM_WIDTH), F32), st_shape, st_shape],
        scratch_shapes=[pltpu.VMEM((SSM_LANES // LANE_TILE, SCAN_CHUNK, LANE_TILE), F32),
                        pltpu.VMEM((SSM_LANES // LANE_TILE, SCAN_CHUNK, LANE_TILE), F32),
                        pltpu.VMEM((8, SUBLANES, SSM_LANES), F32),
                        pltpu.VMEM((2, SUBLANES, SSM_LANES), F32)],
        compiler_params=_cparams(("parallel", "arbitrary")),
        name="ssm_prompt",
    )(u, pm, pm.T, ab, bre, bim, cre, cim, d_skip)


def _ssm_sample_kernel(u_ref, ab_ref, bre_ref, bim_ref, cre_ref, cim_ref, d_ref, h0r_ref, h0i_ref,
                       y_ref, hr_ref, hi_ref, *, steps):
    half = SSM_LANES // 2
    ar, ai = ab_ref[0:1, :], ab_ref[1:2, :]
    hr_ref[...] = h0r_ref[...]
    hi_ref[...] = h0i_ref[...]
    for t in range(steps):
        u = u_ref[:, t * SSM_WIDTH:(t + 1) * SSM_WIDTH]
        ub = u.astype(BF16)
        for h in range(2):
            hs = slice(h * half, (h + 1) * half)
            cs = slice(h * 256, (h + 1) * 256)
            uh = ub[:, cs]
            hr, hi = hr_ref[:, hs], hi_ref[:, hs]
            a_r, a_i = ar[:, hs], ai[:, hs]
            nr = (a_r * hr - a_i * hi) + _dot(uh, bre_ref[h])
            ni = (a_r * hi + a_i * hr) + _dot(uh, bim_ref[h])
            hr_ref[:, hs] = nr
            hi_ref[:, hs] = ni
            y = _dot(nr.astype(BF16), cre_ref[h]) - _dot(ni.astype(BF16), cim_ref[h])
            y_ref[:, t * SSM_WIDTH + h * 256:t * SSM_WIDTH + (h + 1) * 256] = y + d_ref[:, cs] * u[:, cs]


def _ssm_sample_call(u, ab, bre, bim, cre, cim, d_skip, h0r, h0i, steps):
    n = u.shape[0]
    st = jax.ShapeDtypeStruct((n, SSM_LANES), F32)
    return pl.pallas_call(
        functools.partial(_ssm_sample_kernel, steps=steps),
        out_shape=[jax.ShapeDtypeStruct((n, steps * SSM_WIDTH), F32), st, st],
        compiler_params=pltpu.CompilerParams(vmem_limit_bytes=VMEM_LIMIT),
        name="ssm_sample",
    )(u, ab, bre, bim, cre, cim, d_skip, h0r, h0i)


_NT = (((1,), (1,)), ((), ()))


def _attn_prompt_kernel(q_ref, kp_ref, kc_ref, vp_ref, vc_ref, o_ref, lse_ref, *, dil, q_blocks):
    b = pl.program_id(2)
    qi = lax.broadcasted_iota(jnp.int32, (N_BACK, 2 * N_BACK), 0)
    kj = lax.broadcasted_iota(jnp.int32, (N_BACK, 2 * N_BACK), 1)
    band = (kj >= qi) & (kj <= qi + N_BACK)
    band_first = band & (kj >= jnp.where(b > 0, 0, N_BACK))
    lane = lax.broadcasted_iota(jnp.int32, (N_BACK, PAIR_WIDTH), 1)
    head0 = lane < HEAD_DIM

    def rows(j, r):
        start = j * N_BACK * dil + r
        return pl.ds(start, N_BACK, stride=dil) if dil > 1 else pl.ds(start, N_BACK)

    def one(j, r):
        cur = rows(j, r)
        q = (q_ref[cur, :] * (HEAD_DIM ** -0.5)).astype(BF16)
        k_prev = kp_ref[rows(0, r), :] if j == 0 else kc_ref[rows(j - 1, r), :]
        v_prev = vp_ref[rows(0, r), :] if j == 0 else vc_ref[rows(j - 1, r), :]
        kk = jnp.concatenate([k_prev, kc_ref[cur, :]], axis=0).astype(BF16)
        vv = jnp.concatenate([v_prev, vc_ref[cur, :]], axis=0).astype(BF16)
        valid = band_first if j == 0 else band
        outs, lses = [], []
        for sub in range(2):
            in_head = head0 if sub == 0 else jnp.logical_not(head0)
            qm = jnp.where(in_head, q, jnp.zeros_like(q))
            s = lax.dot_general(qm, kk, _NT, preferred_element_type=F32)
            s = jnp.where(valid, s, NEG_INF)
            m = jnp.max(s, axis=-1, keepdims=True)
            p = jnp.exp(s - m)
            den = jnp.sum(p, axis=-1, keepdims=True)
            outs.append(_dot(p.astype(BF16), vv) / den)
            lses.append(jnp.broadcast_to(m + jnp.log(den), (N_BACK, PAIR_WIDTH)))
        o_ref[cur, :] = jnp.where(head0, outs[0], outs[1])
        lse_ref[cur, :] = jnp.where(head0, lses[0], lses[1])

    def residue(r, carry):
        for j in range(q_blocks):
            one(j, r)
        return carry

    if dil <= ATTN_UNROLL:
        for r in range(dil):
            residue(r, 0)
    else:
        lax.fori_loop(0, dil, residue, 0, unroll=ATTN_UNROLL)


ATTN_UNROLL = 4
ATTN_Q_BLOCKS = {1: 8, 4: 2, 16: 1}


def _attn_prompt_call(qkv, n_seq, seq, dil):
    q_blocks = ATTN_Q_BLOCKS[dil]
    prev_rows = N_BACK * dil
    rows = prev_rows * q_blocks
    nbk = seq // rows
    cur = lambda a: pl.BlockSpec((None, None, rows, PAIR_WIDTH), lambda n, hp, b: (a, hp, n * nbk + b, 0))
    prev = lambda a: pl.BlockSpec(
        (None, None, prev_rows, PAIR_WIDTH),
        lambda n, hp, b: (a, hp, jnp.maximum((n * nbk + b) * q_blocks - 1, n * nbk * q_blocks), 0))
    out_spec = pl.BlockSpec((None, rows, PAIR_WIDTH), lambda n, hp, b: (hp, n * nbk + b, 0))
    out_shape = jax.ShapeDtypeStruct((2, n_seq * seq, PAIR_WIDTH), F32)
    return pl.pallas_call(
        functools.partial(_attn_prompt_kernel, dil=dil, q_blocks=q_blocks),
        grid=(n_seq, 2, nbk),
        in_specs=[cur(0), prev(1), cur(1), prev(2), cur(2)],
        out_specs=[out_spec, out_spec],
        out_shape=[out_shape, out_shape],
        compiler_params=_cparams(("parallel", "parallel", "parallel")),
        name=f"attn_prompt_d{dil}",
    )(qkv, qkv, qkv, qkv, qkv)


def _kv_tail_kernel(x_ref, o_ref):
    o_ref[...] = x_ref[...].T


def _kv_tail_call(qkv, n_seq, seq, keep):
    blocks_per_seq = seq // keep
    return pl.pallas_call(
        _kv_tail_kernel,
        grid=(2, n_seq, 2),
        in_specs=[pl.BlockSpec((None, None, keep, PAIR_WIDTH),
                               lambda a, n, hp: (a + 1, hp, (n + 1) * blocks_per_seq - 1, 0))],
        out_specs=pl.BlockSpec((None, None, None, PAIR_WIDTH, keep), lambda a, n, hp: (a, n, hp, 0, 0)),
        out_shape=jax.ShapeDtypeStruct((2, n_seq, 2, PAIR_WIDTH, keep), F32),
        compiler_params=_cparams(("parallel", "parallel", "parallel")),
        name="kv_tail",
    )(qkv)


SAMPLE_SEQ_BLOCK = {128: 8, 512: 4, 2048: 2}


def _attn_sample_kernel(qkv_ref, kt_ref, vt_ref, o_ref, lse_ref, *, dil, steps):
    nb, _, _, w = kt_ref.shape
    rows = nb * steps
    tok = lax.broadcasted_iota(jnp.int32, (2 * rows, w), 0) % rows
    pos = lax.broadcasted_iota(jnp.int32, (2 * rows, w), 1)
    tokn = lax.broadcasted_iota(jnp.int32, (2 * rows, rows), 0) % rows
    coln = lax.broadcasted_iota(jnp.int32, (2 * rows, rows), 1)
    rowo = lax.broadcasted_iota(jnp.int32, (rows, PAIR_WIDTH), 0)
    head0 = lax.broadcasted_iota(jnp.int32, (rows, PAIR_WIDTH), 1) < HEAD_DIM

    vns, s_buf, s_new = [], {}, []
    for hp in range(2):
        q2 = qkv_ref[0, hp] * (HEAD_DIM ** -0.5)
        q_st = jnp.concatenate([jnp.where(head0, q2, 0.0), jnp.where(head0, 0.0, q2)], axis=0).astype(BF16)
        kn = qkv_ref[1, hp].astype(BF16)
        s_new.append(lax.dot_general(q_st, kn, _NT, preferred_element_type=F32))
        vns.append(qkv_ref[2, hp].astype(BF16))
        for n in range(nb):
            kt = kt_ref[n, 2 * hp:2 * hp + 2].reshape(PAIR_WIDTH, w).astype(BF16)
            s_buf[hp, n] = _dot(q_st, kt)

    probs = {}
    for hp in range(2):
        for n in range(nb):
            t = tok - n * steps
            in_buf = (((pos - t) & (dil - 1)) == 0) & (pos >= t)
            tn = tokn - n * steps
            sn = coln - n * steps
            in_new = ((sn >= 0) & (sn <= tn)) if dil == 1 else (sn == tn)
            s = jnp.where(in_buf, s_buf[hp, n], NEG_INF)
            s2 = jnp.where(in_new, s_new[hp], NEG_INF)
            m = jnp.maximum(jnp.max(s, axis=-1, keepdims=True), jnp.max(s2, axis=-1, keepdims=True))
            p = jnp.exp(s - m)
            p2 = jnp.exp(s2 - m)
            den = jnp.sum(p, axis=-1, keepdims=True) + jnp.sum(p2, axis=-1, keepdims=True)
            probs[hp, n] = (p.astype(BF16), p2.astype(BF16), den, m + jnp.log(den))

    for hp in range(2):
        o_pair = jnp.zeros((rows, PAIR_WIDTH), F32)
        lse_pair = jnp.zeros((rows, PAIR_WIDTH), F32)
        for n in range(nb):
            p, p2, den, lse = probs[hp, n]
            vt = vt_ref[n, 2 * hp:2 * hp + 2].reshape(PAIR_WIDTH, w).astype(BF16)
            o = (lax.dot_general(p, vt, _NT, preferred_element_type=F32) + _dot(p2, vns[hp])) / den
            o = jnp.where(head0, o[:rows], o[rows:])
            lse = jnp.where(head0, jnp.broadcast_to(lse[:rows], (rows, PAIR_WIDTH)),
                            jnp.broadcast_to(lse[rows:], (rows, PAIR_WIDTH)))
            mine = (rowo >= n * steps) & (rowo < (n + 1) * steps)
            o_pair = jnp.where(mine, o, o_pair)
            lse_pair = jnp.where(mine, lse, lse_pair)
        o_ref[hp] = o_pair
        lse_ref[hp] = lse_pair


def _attn_sample_call(qkv, k_cache, v_cache, l, n_seq, steps, dil):
    w = k_cache.shape[2]
    kt = k_cache.transpose(0, 1, 3, 4, 2)
    vt = v_cache.transpose(0, 1, 3, 4, 2)
    nb = SAMPLE_SEQ_BLOCK[w]
    rows = nb * steps
    cache_spec = pl.BlockSpec((None, nb, HEADS, HEAD_DIM, w), lambda i: (l, i, 0, 0, 0))
    out_spec = pl.BlockSpec((2, rows, PAIR_WIDTH), lambda i: (0, i, 0))
    out_shape = jax.ShapeDtypeStruct((2, n_seq * steps, PAIR_WIDTH), F32)
    return pl.pallas_call(
        functools.partial(_attn_sample_kernel, dil=dil, steps=steps),
        grid=(n_seq // nb,),
        in_specs=[pl.BlockSpec((3, 2, rows, PAIR_WIDTH), lambda i: (0, 0, i, 0)), cache_spec, cache_spec],
        out_specs=[out_spec, out_spec],
        out_shape=[out_shape, out_shape],
        compiler_params=_cparams(("parallel",)),
        name=f"attn_sample_d{dil}",
    )(qkv, kt, vt)


def _merge_kernel(x_ref, mod_ref, y_ref, o0_ref, l0_ref, o1_ref, l1_ref, o2_ref, l2_ref,
                  gmix_ref, gffn_ref, wmg_ref, bmg_ref, wglu_ref, wbs_ref, wba_ref, wout_ref,
                  wrh_ref, wrl_ref, br_ref, x1_ref, h2_ref, comb_ref):
    x = x_ref[...]
    tm = x.shape[0]
    hb = (_rmsnorm(x, gmix_ref[...]) * (1.0 + mod_ref[1]) + mod_ref[0]).astype(BF16)

    y = _gelu_tanh(y_ref[...])
    yg = y * _sigmoid(_dot(y.astype(BF16), wglu_ref[...]))
    ssm_branch = _dot(yg.astype(BF16), wbs_ref[...])

    pairs = []
    for hp in range(2):
        l0, l1, l2 = l0_ref[hp], l1_ref[hp], l2_ref[hp]
        lm = jnp.maximum(jnp.maximum(l0, l1), l2)
        e0, e1, e2 = jnp.exp(l0 - lm), jnp.exp(l1 - lm), jnp.exp(l2 - lm)
        pairs.append((e0 * o0_ref[hp] + e1 * o1_ref[hp] + e2 * o2_ref[hp]) / (e0 + e1 + e2))
    attn = jnp.concatenate(pairs, axis=-1)
    attn_branch = _dot(attn.astype(BF16), wba_ref[...])

    g_ssm = _sigmoid(_dot(hb, wmg_ref[:, :D_MODEL]) + bmg_ref[:, :D_MODEL])
    g_attn = _sigmoid(_dot(hb, wmg_ref[:, D_MODEL:]) + bmg_ref[:, D_MODEL:])
    merged = g_ssm * ssm_branch + g_attn * attn_branch
    x1 = x + mod_ref[2] * _dot(merged.astype(BF16), wout_ref[...])
    x1_ref[...] = x1

    h2 = _rmsnorm(x1, gffn_ref[...]) * (1.0 + mod_ref[4]) + mod_ref[3]
    h2_hi = h2.astype(BF16)
    h2_ref[...] = h2_hi
    h2_lo = (h2 - h2_hi.astype(F32)).astype(BF16)
    logits = (_dot(h2_hi, wrh_ref[...]) + _dot(h2_hi, wrl_ref[...]) + _dot(h2_lo, wrh_ref[...])
              + br_ref[...])

    lane = lax.broadcasted_iota(jnp.int32, (tm, ROUTER_LANES), 1)
    lanef = lane.astype(F32)
    low = jnp.float32(-3.0e38)
    is_grp = (lane >= GROUP_LOGIT_LANE0) & (lane < GROUP_LOGIT_LANE0 + N_EXPERT_GROUPS)
    lg = jnp.where(is_grp, logits, low)
    mg = jnp.max(lg, axis=-1, keepdims=True)
    gsel = jnp.min(jnp.where(lg == mg, lanef - GROUP_LOGIT_LANE0, 1e9), axis=-1, keepdims=True)
    pg_sel = 1.0 / jnp.sum(jnp.exp(lg - mg), axis=-1, keepdims=True)
    in_grp = (lane < N_EXPERTS) & ((lane // EXPERTS_PER_GROUP).astype(F32) == gsel)
    le = jnp.where(in_grp, logits, low)
    m1 = jnp.max(le, axis=-1, keepdims=True)
    i1 = jnp.min(jnp.where(le == m1, lanef, 1e9), axis=-1, keepdims=True)
    le2 = jnp.where(lanef == i1, low, le)
    m2 = jnp.max(le2, axis=-1, keepdims=True)
    i2 = jnp.min(jnp.where(le2 == m2, lanef, 1e9), axis=-1, keepdims=True)
    r = jnp.exp(m2 - m1)
    w1 = 1.0 / (1.0 + r)
    w2 = r / (1.0 + r)
    comb = pg_sel * (jnp.where(lanef == i1, w1, 0.0) + jnp.where(lanef == i2, w2, 0.0))
    comb_ref[...] = comb + jnp.where(lane == GROUP_ID_LANE, gsel, 0.0)


def _merge_call(x, mod, y, attn_outs, p, l, tm, tiles_per_seq):
    ntok = x.shape[0]
    tok = lambda w: pl.BlockSpec((tm, w), lambda i: (i, 0))
    lay = lambda *shape: pl.BlockSpec((None,) + shape, lambda i: (l,) + (0,) * len(shape))
    attn_specs = [pl.BlockSpec((2, tm, PAIR_WIDTH), lambda i: (0, i, 0))] * 6
    return pl.pallas_call(
        _merge_kernel,
        grid=(ntok // tm,),
        in_specs=[tok(D_MODEL), _mod_spec(mod, tm, tiles_per_seq), tok(SSM_WIDTH)] + attn_specs + [
            lay(1, D_MODEL), lay(1, D_MODEL),
            lay(D_MODEL, 2 * D_MODEL), lay(1, 2 * D_MODEL),
            lay(SSM_WIDTH, SSM_WIDTH), lay(SSM_WIDTH, D_MODEL), lay(GROUP_WIDTH, D_MODEL),
            lay(D_MODEL, D_MODEL),
            lay(D_MODEL, ROUTER_LANES), lay(D_MODEL, ROUTER_LANES), lay(1, ROUTER_LANES),
        ],
        out_specs=[tok(D_MODEL), tok(D_MODEL), tok(ROUTER_LANES)],
        out_shape=[jax.ShapeDtypeStruct((ntok, D_MODEL), F32),
                   jax.ShapeDtypeStruct((ntok, D_MODEL), BF16),
                   jax.ShapeDtypeStruct((ntok, ROUTER_LANES), F32)],
        compiler_params=_cparams(("parallel",)),
        name="merge_router",
    )(x, mod, y, *attn_outs, p['g_mix'], p['g_ffn'], p['w_mgate'], p['b_mgate'], p['w_glu'],
      p['w_br_ssm'], p['w_br_attn'], p['w_out'], p['w_router_hi'], p['w_router_lo'], p['b_router'])


MOE_EXPERT_BLOCK = 4
MOE_ROW_CHUNK = 128
GROUP_ID_LANE = 64
_TN = (((0,), (0,)), ((), ()))


def _split3(x):
    a = x.astype(BF16)
    r = x - a.astype(F32)
    b = r.astype(BF16)
    return a, b, (r - b.astype(F32)).astype(BF16)


def _moe_kernel(h2_ref, comb_ref, x1_ref, mod_ref, wg_ref, wu_ref, wd_ref, gfin_ref, out_ref,
                perm, hs, combs, ys, bounds, *, final):
    s = pl.program_id(1)
    tm = h2_ref.shape[0]
    ch = MOE_ROW_CHUNK

    @pl.when(s == 0)
    def _():
        comb = comb_ref[...]
        lane = lax.broadcasted_iota(jnp.int32, (tm, ROUTER_LANES), 1)
        gid = jnp.sum(jnp.where(lane == GROUP_ID_LANE, comb, 0.0), axis=-1, keepdims=True)
        onehot = jnp.where(lane.astype(F32) == gid, 1.0, 0.0)
        row = lax.broadcasted_iota(jnp.int32, (tm, tm), 0)
        col = lax.broadcasted_iota(jnp.int32, (tm, tm), 1)
        earlier = jnp.where(col < row, 1.0, 0.0).astype(BF16)
        rank = _dot(earlier, onehot.astype(BF16))
        tot = jnp.sum(onehot, axis=0, keepdims=True)
        lane1 = lax.broadcasted_iota(jnp.int32, (1, ROUTER_LANES), 1)
        start = jnp.zeros((1, ROUTER_LANES), F32)
        run = jnp.zeros((1, 1), F32)
        for g in range(N_EXPERT_GROUPS):
            bounds[2 * g] = run[0, 0].astype(jnp.int32)
            start = start + jnp.where(lane1 == g, run, 0.0)
            run = run + jnp.sum(jnp.where(lane1 == g, tot, 0.0), axis=-1, keepdims=True)
            bounds[2 * g + 1] = run[0, 0].astype(jnp.int32)
        pos = jnp.sum(onehot * (rank + start), axis=-1, keepdims=True)
        perm[...] = jnp.where(col.astype(F32) == pos, 1.0, 0.0).astype(BF16)
        hs[...] = lax.dot_general(perm[...], h2_ref[...], _TN, preferred_element_type=F32).astype(BF16)
        parts = lax.dot_general(perm[...], jnp.concatenate(_split3(comb), axis=-1), _TN,
                                preferred_element_type=F32)
        combs[...] = (parts[:, :ROUTER_LANES] + parts[:, ROUTER_LANES:2 * ROUTER_LANES]) + parts[:, 2 * ROUTER_LANES:]
        ys[...] = jnp.zeros_like(ys)

    g = s // (EXPERTS_PER_GROUP // MOE_EXPERT_BLOCK)
    lo = bounds[2 * g] // ch
    hi = (bounds[2 * g + 1] + (ch - 1)) // ch
    lane = lax.broadcasted_iota(jnp.int32, (ch, ROUTER_LANES), 1)

    def chunk(c, carry):
        r0 = pl.multiple_of(c * ch, ch)
        h = hs[pl.ds(r0, ch), :]
        comb = combs[pl.ds(r0, ch), :]
        hids = []
        for j in range(MOE_EXPERT_BLOCK):
            gt = _dot(h, wg_ref[j])
            w = jnp.sum(jnp.where(lane == s * MOE_EXPERT_BLOCK + j, comb, 0.0), axis=-1, keepdims=True)
            hids.append((gt * _sigmoid(gt) * _dot(h, wu_ref[j]) * w).astype(BF16))
        ys[pl.ds(r0, ch), :] += _dot(jnp.concatenate(hids, axis=-1), wd_ref[...])
        return carry

    lax.fori_loop(lo, hi, chunk, 0)

    @pl.when(s == pl.num_programs(1) - 1)
    def _():
        x2 = x1_ref[...] + mod_ref[5] * _dot(perm[...], ys[...].astype(BF16))
        out_ref[...] = _rmsnorm(x2, gfin_ref[...]) if final else x2


def _moe_call(h2, comb, x1, mod, p, l, tm, tiles_per_seq, final):
    ntok = h2.shape[0]
    eb = MOE_EXPERT_BLOCK
    tok = lambda w: pl.BlockSpec((tm, w), lambda i, s: (i, 0))
    if mod.ndim == 4:
        mod_spec = pl.BlockSpec((None, 6, 1, D_MODEL), lambda i, s: (i // tiles_per_seq, 0, 0, 0))
    else:
        mod_spec = pl.BlockSpec((6, tm, D_MODEL), lambda i, s: (0, i, 0))
    w_down = p['w_exp_down'].reshape(DEPTH, N_EXPERTS // eb, eb * EXPERT_FF, D_MODEL)
    return pl.pallas_call(
        functools.partial(_moe_kernel, final=final),
        grid=(ntok // tm, N_EXPERTS // eb),
        in_specs=[tok(D_MODEL), tok(ROUTER_LANES), tok(D_MODEL), mod_spec,
                  pl.BlockSpec((None, eb, D_MODEL, EXPERT_FF), lambda i, s: (l, s, 0, 0)),
                  pl.BlockSpec((None, eb, D_MODEL, EXPERT_FF), lambda i, s: (l, s, 0, 0)),
                  pl.BlockSpec((None, None, eb * EXPERT_FF, D_MODEL), lambda i, s: (l, s, 0, 0)),
                  pl.BlockSpec((1, D_MODEL), lambda i, s: (0, 0))],
        out_specs=tok(D_MODEL),
        out_shape=jax.ShapeDtypeStruct((ntok, D_MODEL), F32),
        scratch_shapes=[pltpu.VMEM((tm, tm), BF16), pltpu.VMEM((tm, D_MODEL), BF16),
                        pltpu.VMEM((tm, ROUTER_LANES), F32), pltpu.VMEM((tm, D_MODEL), F32),
                        pltpu.SMEM((2 * N_EXPERT_GROUPS,), jnp.int32)],
        compiler_params=_cparams(("parallel", "arbitrary")),
        name="moe_experts",
    )(h2, comb, x1, mod, p['w_exp_gate'], p['w_exp_up'], w_down, p['g_final'])


def _run_trunk(x, mod_layers, p, ssm, n_seq, seq, caches, h0):
    ntok = n_seq * seq
    prompt = caches is None
    tm_proj = 512 if prompt else 256
    tm_merge = 512 if prompt else 256
    tm_moe = 1024 if prompt else 256
    states = []
    for l in range(DEPTH):
        mod = mod_layers[l]
        u, *qkv = _inproj_call(x, mod, p['g_mix'], p['w_in'], l, tm_proj, seq // tm_proj if prompt else 1)
        ab, bre, bim, cre, cim, d_skip = ssm[l]
        if prompt:
            y, hre, him = _ssm_prompt_call(u, ab, bre, bim, cre, cim, d_skip, n_seq, seq)
            hre, him = hre[:, 0], him[:, 0]
            attn = []
            for g in range(N_GROUPS):
                attn += list(_attn_prompt_call(qkv[g], n_seq, seq, DILATIONS[g]))
        else:
            y, hre, him = _ssm_sample_call(u.reshape(n_seq, seq * SSM_WIDTH), ab, bre, bim, cre, cim,
                                           d_skip, h0[0][l], h0[1][l], seq)
            y = y.reshape(ntok, SSM_WIDTH)
            attn = []
            for g in range(N_GROUPS):
                attn += list(_attn_sample_call(qkv[g], caches[2 * g], caches[2 * g + 1], l, n_seq, seq,
                                               DILATIONS[g]))
        x1, h2, comb = _merge_call(x, mod, y, attn, p, l, tm_merge, seq // tm_merge if prompt else 1)
        x = _moe_call(h2, comb, x1, mod, p, l, tm_moe, seq // tm_moe if prompt else 1,
                      final=(l == DEPTH - 1))
        kv = []
        for g in range(N_GROUPS):
            keep = min(WINDOWS[g], seq)
            if prompt:
                tails = _kv_tail_call(qkv[g], n_seq, seq, keep).reshape(2, n_seq, HEADS, HEAD_DIM, keep)
                kv += [tails[0].transpose(0, 3, 1, 2), tails[1].transpose(0, 3, 1, 2)]
            else:
                for a in (1, 2):
                    new = qkv[g][a].reshape(2, n_seq, seq, 2, HEAD_DIM)
                    kv.append(new.transpose(1, 2, 0, 3, 4).reshape(n_seq, seq, HEADS, HEAD_DIM))
        states.append(kv + [hre.reshape(n_seq, N_SSM_GROUPS, SSM_STATE),
                            him.reshape(n_seq, N_SSM_GROUPS, SSM_STATE)])
    new_state = [jnp.stack([states[l][i] for l in range(DEPTH)], axis=0) for i in range(len(states[0]))]
    return x, new_state


def kernel(x_prompt, x_sample, cache_k_w128, cache_v_w128, cache_k_w512, cache_v_w512,
           cache_k_w2048, cache_v_w2048, state_ssm_re, state_ssm_im, c_prompt, c_sample,
           w_ada, b_ada, g_norm_mix, g_norm_ffn, g_final, w_in, ssm_a_re, ssm_a_im, ssm_b_re,
           ssm_b_im, ssm_c_re, ssm_c_im, ssm_log_step, ssm_d, w_glu, w_br_ssm, w_br_attn,
           w_mgate, b_mgate, w_out, w_router_grp, b_router_grp, w_router_exp, b_router_exp,
           w_exp_gate, w_exp_up, w_exp_down):
    batch, seq, _ = x_prompt.shape
    dec_batch, dec_seq, _ = x_sample.shape

    n_c = batch + dec_batch
    rows = -(-n_c // SUBLANES) * SUBLANES
    c_all = jnp.concatenate([c_prompt, c_sample, jnp.zeros((rows - n_c, D_MODEL), F32)], axis=0)
    mod_all = _ada_call(c_all, w_ada, b_ada)
    mod_prompt = [mod_all[l, :batch].reshape(batch, 6, 1, D_MODEL) for l in range(DEPTH)]
    mod_sample = [jnp.repeat(mod_all[l, batch:n_c].reshape(dec_batch, 6, D_MODEL), dec_seq, axis=0)
                  .transpose(1, 0, 2) for l in range(DEPTH)]

    abr, abi, bbr, bbi = _ssm_param_call(ssm_a_re, ssm_a_im, ssm_log_step, ssm_b_re, ssm_b_im)
    ssm = []
    for l in range(DEPTH):
        pick = lambda a: a[l].reshape(N_SSM_GROUPS, SSM_STATE, SSM_GROUP_CH)
        ab = jnp.stack([pick(abr)[:, :, 0].reshape(SSM_LANES), pick(abi)[:, :, 0].reshape(SSM_LANES)])
        bre = _block_diag_halves(pick(bbr).transpose(0, 2, 1), BF16)
        bim = _block_diag_halves(pick(bbi).transpose(0, 2, 1), BF16)
        cre = _block_diag_halves(ssm_c_re[l].transpose(0, 2, 1), BF16)
        cim = _block_diag_halves(ssm_c_im[l].transpose(0, 2, 1), BF16)
        ssm.append((ab, bre, bim, cre, cim, ssm_d[l].reshape(1, SSM_WIDTH)))

    w_router = jnp.concatenate(
        [w_router_exp, w_router_grp,
         jnp.zeros((DEPTH, D_MODEL, ROUTER_LANES - N_EXPERTS - N_EXPERT_GROUPS), F32)], axis=-1)
    b_router = jnp.concatenate(
        [b_router_exp, b_router_grp,
         jnp.zeros((DEPTH, ROUTER_LANES - N_EXPERTS - N_EXPERT_GROUPS), F32)], axis=-1)
    w_router_hi = w_router.astype(BF16)
    p = dict(
        g_mix=g_norm_mix.reshape(DEPTH, 1, D_MODEL), g_ffn=g_norm_ffn.reshape(DEPTH, 1, D_MODEL),
        g_final=g_final.reshape(1, D_MODEL), w_in=w_in.astype(BF16), w_glu=w_glu.astype(BF16),
        w_br_ssm=w_br_ssm.astype(BF16), w_br_attn=w_br_attn.astype(BF16),
        w_mgate=w_mgate.astype(BF16), b_mgate=b_mgate.reshape(DEPTH, 1, 2 * D_MODEL),
        w_out=w_out.astype(BF16), w_router_hi=w_router_hi,
        w_router_lo=(w_router - w_router_hi.astype(F32)).astype(BF16),
        b_router=b_router.reshape(DEPTH, 1, ROUTER_LANES),
        w_exp_gate=w_exp_gate.astype(BF16), w_exp_up=w_exp_up.astype(BF16),
        w_exp_down=w_exp_down.astype(BF16))

    y_prompt, pstate = _run_trunk(x_prompt.reshape(batch * seq, D_MODEL), mod_prompt, p, ssm,
                                  batch, seq, None, None)
    caches = [cache_k_w128, cache_v_w128, cache_k_w512, cache_v_w512, cache_k_w2048, cache_v_w2048]
    h0 = (state_ssm_re.reshape(DEPTH, dec_batch, SSM_LANES), state_ssm_im.reshape(DEPTH, dec_batch, SSM_LANES))
    y_sample, sstate = _run_trunk(x_sample.reshape(dec_batch * dec_seq, D_MODEL), mod_sample, p, ssm,
                                  dec_batch, dec_seq, caches, h0)
    return (y_prompt.reshape(batch, seq, D_MODEL), y_sample.reshape(dec_batch, dec_seq, D_MODEL),
            *pstate, *sstate)
```

```python
import functools
import math

import jax
import jax.numpy as jnp
from jax import lax
from jax.experimental import pallas as pl
from jax.experimental.pallas import tpu as pltpu

F32 = jnp.float32
BF16 = jnp.bfloat16

D_MODEL = 1024
DEPTH = 2
SSM_WIDTH = 512
SSM_GROUP_CH = 16
N_SSM_GROUPS = 32
SSM_STATE = 64
SSM_LANES = N_SSM_GROUPS * SSM_STATE
HEAD_DIM = 64
HEADS = 4
GROUP_WIDTH = HEADS * HEAD_DIM
PAIR_WIDTH = 2 * HEAD_DIM
WINDOWS = (128, 512, 2048)
DILATIONS = (1, 4, 16)
N_BACK = 128
N_GROUPS = 3
QKV_WIDTH = N_GROUPS * GROUP_WIDTH
IN_PROJ_WIDTH = SSM_WIDTH + 3 * QKV_WIDTH
N_EXPERT_GROUPS = 4
EXPERTS_PER_GROUP = 8
N_EXPERTS = 32
EXPERT_FF = 256
RMS_EPS = 1e-6
NEG_INF = -1e30
ROUTER_LANES = 128
GROUP_LOGIT_LANE0 = N_EXPERTS

SUBLANES = 8
VMEM_LIMIT = 56 * 1024 * 1024


def _cparams(sem):
    return pltpu.CompilerParams(dimension_semantics=sem, vmem_limit_bytes=VMEM_LIMIT)


def _sigmoid(x):
    return 1.0 / (1.0 + jnp.exp(-x))


def _gelu_tanh(x):
    return 0.5 * x * (1.0 + jnp.tanh(math.sqrt(2.0 / math.pi) * (x + 0.044715 * (x * x * x))))


def _dot(a, b):
    return jnp.dot(a, b, preferred_element_type=F32)


def _rmsnorm(x, g):
    ms = jnp.mean(x * x, axis=-1, keepdims=True)
    return x * lax.rsqrt(ms + RMS_EPS) * g


def _ada_kernel(cp_ref, cs_ref, w_ref, b_ref, op_ref, os_ref):
    w = w_ref[...].astype(BF16)
    for c_ref, o_ref in ((cp_ref, op_ref), (cs_ref, os_ref)):
        c = c_ref[...]
        o_ref[...] = _dot((c * _sigmoid(c)).astype(BF16), w) + b_ref[...]


def _ada_call(c_prompt_rows, c_sample_rows, w_ada, b_ada):
    rp, rs = c_prompt_rows.shape[0], c_sample_rows.shape[0]
    out = lambda r: pl.BlockSpec((None, None, r, D_MODEL), lambda l, j: (l, j, 0, 0))
    return pl.pallas_call(
        _ada_kernel,
        grid=(DEPTH, 6),
        in_specs=[
            pl.BlockSpec((rp, D_MODEL), lambda l, j: (0, 0)),
            pl.BlockSpec((rs, D_MODEL), lambda l, j: (0, 0)),
            pl.BlockSpec((None, D_MODEL, D_MODEL), lambda l, j: (l, 0, j)),
            pl.BlockSpec((None, 1, D_MODEL), lambda l, j: (l, 0, j)),
        ],
        out_specs=[out(rp), out(rs)],
        out_shape=[jax.ShapeDtypeStruct((DEPTH, 6, rp, D_MODEL), F32),
                   jax.ShapeDtypeStruct((DEPTH, 6, rs, D_MODEL), F32)],
        compiler_params=_cparams(("parallel", "parallel")),
        name="ada_mod",
    )(c_prompt_rows, c_sample_rows, w_ada, b_ada.reshape(DEPTH, 1, 6 * D_MODEL))


def _ssm_param_kernel(ar_ref, ai_ref, ls_ref, br_ref, bi_ref, abr_ref, abi_ref, bbr_ref, bbi_ref):
    lr, li = ar_ref[...], ai_ref[...]
    dt = jnp.exp(ls_ref[...])
    mag = jnp.exp(lr * dt)
    abr = mag * jnp.cos(li * dt)
    abi = mag * jnp.sin(li * dt)
    den = lr * lr + li * li
    er, ei = abr - 1.0, abi
    fr = (er * lr + ei * li) / den
    fi = (ei * lr - er * li) / den
    br, bi = br_ref[...], bi_ref[...]
    abr_ref[...] = abr
    abi_ref[...] = abi
    bbr_ref[...] = fr * br - fi * bi
    bbi_ref[...] = fr * bi + fi * br


def _ssm_param_call(a_re, a_im, log_step, b_re, b_im):
    w = SSM_STATE * SSM_GROUP_CH
    rep = lambda a: jnp.repeat(a, SSM_GROUP_CH, axis=-1)
    big = pl.BlockSpec((None, N_SSM_GROUPS, w), lambda l: (l, 0, 0))
    shp = jax.ShapeDtypeStruct((DEPTH, N_SSM_GROUPS, w), F32)
    return pl.pallas_call(
        _ssm_param_kernel,
        grid=(DEPTH,),
        in_specs=[big, big, pl.BlockSpec((None, N_SSM_GROUPS, 1), lambda l: (l, 0, 0)), big, big],
        out_specs=[big, big, big, big],
        out_shape=[shp, shp, shp, shp],
        compiler_params=_cparams(("parallel",)),
        name="ssm_params",
    )(rep(a_re), rep(a_im), log_step.reshape(DEPTH, N_SSM_GROUPS, 1),
      b_re.reshape(DEPTH, N_SSM_GROUPS, w), b_im.reshape(DEPTH, N_SSM_GROUPS, w))


def _block_diag_halves(blocks, dtype):
    g, r, c = blocks.shape
    half = g // 2
    eye = jnp.eye(half, dtype=bool)

    def one(b):
        m = jnp.where(eye[:, None, :, None], b[:, :, None, :], 0.0)
        return m.reshape(half * r, half * c)

    return jnp.stack([one(blocks[:half]), one(blocks[half:])]).astype(dtype)


def _inproj_kernel(x_ref, mod_ref, g_ref, w_ref, u_ref, a0_ref, a1_ref, a2_ref):
    h = _rmsnorm(x_ref[...], g_ref[...]) * (1.0 + mod_ref[1]) + mod_ref[0]
    hb = h.astype(BF16)
    u_ref[...] = _dot(hb, w_ref[:, :SSM_WIDTH])
    for g, ref in enumerate((a0_ref, a1_ref, a2_ref)):
        for a in range(3):
            c0 = SSM_WIDTH + a * QKV_WIDTH + g * GROUP_WIDTH
            val = _dot(hb, w_ref[:, c0:c0 + GROUP_WIDTH])
            ref[a, 0] = val[:, :PAIR_WIDTH]
            ref[a, 1] = val[:, PAIR_WIDTH:]


def _mod_spec(mod, tm, tiles_per_seq):
    if mod.ndim == 4:
        return pl.BlockSpec((6, None, 1, D_MODEL), lambda i: (0, i // tiles_per_seq, 0, 0))
    return pl.BlockSpec((6, tm, D_MODEL), lambda i: (0, i, 0))


def _inproj_call(x, mod, g_mix, w_in, l, tm, tiles_per_seq):
    ntok = x.shape[0]
    grp = jax.ShapeDtypeStruct((3, 2, ntok, PAIR_WIDTH), F32)
    grp_spec = pl.BlockSpec((3, 2, tm, PAIR_WIDTH), lambda i: (0, 0, i, 0))
    return pl.pallas_call(
        _inproj_kernel,
        grid=(ntok // tm,),
        in_specs=[
            pl.BlockSpec((tm, D_MODEL), lambda i: (i, 0)),
            _mod_spec(mod, tm, tiles_per_seq),
            pl.BlockSpec((None, 1, D_MODEL), lambda i: (l, 0, 0)),
            pl.BlockSpec((None, D_MODEL, IN_PROJ_WIDTH), lambda i: (l, 0, 0)),
        ],
        out_specs=[pl.BlockSpec((tm, SSM_WIDTH), lambda i: (i, 0)), grp_spec, grp_spec, grp_spec],
        out_shape=[jax.ShapeDtypeStruct((ntok, SSM_WIDTH), F32), grp, grp, grp],
        compiler_params=_cparams(("parallel",)),
        name="in_proj",
    )(x, mod, g_mix, w_in)


LANE_TILE = 128
SCAN_CHUNK = 256
SEG_LEN = SCAN_CHUNK // SUBLANES
SCAN_TILE_GROUP = 4


def _cmul(ar, ai, br, bi):
    return ar * br - ai * bi, ar * bi + ai * br


def _ssm_prompt_kernel(u_ref, pm_ref, pmt_ref, ab_ref, bre_ref, bim_ref, cre_ref, cim_ref, d_ref,
                       y_ref, hre_ref, him_ref, bur, bui, tab, carry):
    c = pl.program_id(1)
    half = SSM_LANES // 2
    tiles_per_half = half // LANE_TILE

    @pl.when(c == 0)
    def _():
        shp = (SUBLANES, SSM_LANES)
        sub = lax.broadcasted_iota(jnp.int32, shp, 0)
        b_r = jnp.broadcast_to(ab_ref[0:1, :], shp)
        b_i = jnp.broadcast_to(ab_ref[1:2, :], shp)
        for _ in range(SEG_LEN.bit_length() - 1):
            b_r, b_i = _cmul(b_r, b_i, b_r, b_i)
        p_r, p_i = b_r, b_i
        ps_r, ps_i = jnp.ones(shp, F32), jnp.zeros(shp, F32)
        for s in range(1, SUBLANES):
            if s in (1, 2, 4):
                k = {1: 0, 2: 2, 4: 4}[s]
                tab[k] = jnp.where(sub >= s, p_r, 0.0)
                tab[k + 1] = jnp.where(sub >= s, p_i, 0.0)
            ps_r = jnp.where(sub >= s, p_r, ps_r)
            ps_i = jnp.where(sub >= s, p_i, ps_i)
            p_r, p_i = _cmul(p_r, p_i, b_r, b_i)
        tab[6] = ps_r
        tab[7] = ps_i
        carry[...] = jnp.zeros_like(carry)

    u = u_ref[...]
    ub = _dot(pm_ref[...], u.astype(BF16)).astype(BF16)
    for h in range(2):
        uh = ub[:, h * 256:(h + 1) * 256]
        res_r = _dot(uh, bre_ref[h])
        res_i = _dot(uh, bim_ref[h])
        for j in range(tiles_per_half):
            bur[h * tiles_per_half + j] = res_r[:, j * LANE_TILE:(j + 1) * LANE_TILE]
            bui[h * tiles_per_half + j] = res_i[:, j * LANE_TILE:(j + 1) * LANE_TILE]

    vshape = (SUBLANES, LANE_TILE)
    sub1 = lax.broadcasted_iota(jnp.int32, vshape, 0) >= 1
    for g0 in range(0, SSM_LANES // LANE_TILE, SCAN_TILE_GROUP):
        tiles = list(range(g0, g0 + SCAN_TILE_GROUP))
        lanes = [slice(lt * LANE_TILE, (lt + 1) * LANE_TILE) for lt in tiles]
        a_r = [jnp.broadcast_to(ab_ref[0:1, sl], vshape) for sl in lanes]
        a_i = [jnp.broadcast_to(ab_ref[1:2, sl], vshape) for sl in lanes]

        def step(t, hs, store):
            out = []
            for j, lt in enumerate(tiles):
                rows = pl.ds(pl.multiple_of(t * SUBLANES, SUBLANES), SUBLANES)
                hr, hi = hs[2 * j], hs[2 * j + 1]
                nr = (a_r[j] * hr - a_i[j] * hi) + bur[lt, rows, :]
                ni = (a_r[j] * hi + a_i[j] * hr) + bui[lt, rows, :]
                if store:
                    bur[lt, rows, :] = nr
                    bui[lt, rows, :] = ni
                out += [nr, ni]
            return tuple(out)

        zeros = tuple(jnp.zeros(vshape, F32) for _ in range(2 * SCAN_TILE_GROUP))
        ends = lax.fori_loop(0, SEG_LEN, lambda t, hs: step(t, hs, False), zeros, unroll=True)

        starts = []
        for j, sl in enumerate(lanes):
            fr, fi = ends[2 * j], ends[2 * j + 1]
            for k in (1, 2, 4):
                ti = {1: 0, 2: 2, 4: 4}[k]
                mr, mi = tab[ti, :, sl], tab[ti + 1, :, sl]
                rr = pltpu.roll(fr, k, axis=0)
                ri = pltpu.roll(fi, k, axis=0)
                fr, fi = fr + (mr * rr - mi * ri), fi + (mr * ri + mi * rr)
            cr, ci = carry[0, :, sl], carry[1, :, sl]
            ps_r, ps_i = tab[6, :, sl], tab[7, :, sl]
            sr = (ps_r * cr - ps_i * ci) + jnp.where(sub1, pltpu.roll(fr, 1, axis=0), 0.0)
            si = (ps_r * ci + ps_i * cr) + jnp.where(sub1, pltpu.roll(fi, 1, axis=0), 0.0)
            starts += [sr, si]

        last = lax.fori_loop(0, SEG_LEN, lambda t, hs: step(t, hs, True), tuple(starts), unroll=True)
        for j, sl in enumerate(lanes):
            carry[0, :, sl] = jnp.broadcast_to(last[2 * j][SUBLANES - 1:SUBLANES, :], vshape)
            carry[1, :, sl] = jnp.broadcast_to(last[2 * j + 1][SUBLANES - 1:SUBLANES, :], vshape)

    for h in range(2):
        cs = slice(h * 256, (h + 1) * 256)
        tl = range(h * tiles_per_half, (h + 1) * tiles_per_half)
        hr = jnp.concatenate([bur[lt] for lt in tl], axis=-1).astype(BF16)
        hi = jnp.concatenate([bui[lt] for lt in tl], axis=-1).astype(BF16)
        y = _dot(hr, cre_ref[h]) - _dot(hi, cim_ref[h])
        y_hi = y.astype(BF16)
        y_lo = (y - y_hi.astype(F32)).astype(BF16)
        y = _dot(pmt_ref[...], y_hi) + _dot(pmt_ref[...], y_lo)
        y_ref[:, cs] = y + d_ref[:, cs] * u[:, cs]

    @pl.when(c == pl.num_programs(1) - 1)
    def _():
        hre_ref[...] = carry[0]
        him_ref[...] = carry[1]


def _ssm_prompt_call(u, ab, bre, bim, cre, cim, d_skip, n_seq, seq):
    chunks = seq // SCAN_CHUNK
    half = SSM_LANES // 2
    wspec_b = pl.BlockSpec((2, 256, half), lambda n, c: (0, 0, 0))
    wspec_c = pl.BlockSpec((2, half, 256), lambda n, c: (0, 0, 0))
    st_spec = pl.BlockSpec((None, SUBLANES, SSM_LANES), lambda n, c: (n, 0, 0))
    st_shape = jax.ShapeDtypeStruct((n_seq, SUBLANES, SSM_LANES), F32)
    r = jnp.arange(SCAN_CHUNK)
    pm = (r[None, :] == ((r % SUBLANES) * SEG_LEN + r // SUBLANES)[:, None]).astype(BF16)
    perm_spec = pl.BlockSpec((SCAN_CHUNK, SCAN_CHUNK), lambda n, c: (0, 0))
    return pl.pallas_call(
        _ssm_prompt_kernel,
        grid=(n_seq, chunks),
        in_specs=[
            pl.BlockSpec((SCAN_CHUNK, SSM_WIDTH), lambda n, c: (n * chunks + c, 0)),
            perm_spec, perm_spec,
            pl.BlockSpec((2, SSM_LANES), lambda n, c: (0, 0)),
            wspec_b, wspec_b, wspec_c, wspec_c,
            pl.BlockSpec((1, SSM_WIDTH), lambda n, c: (0, 0)),
        ],
        out_specs=[pl.BlockSpec((SCAN_CHUNK, SSM_WIDTH), lambda n, c: (n * chunks + c, 0)),
                   st_spec, st_spec],
        out_shape=[jax.ShapeDtypeStruct((n_seq * seq, SSM_WIDTH), F32), st_shape, st_shape],
        scratch_shapes=[pltpu.VMEM((SSM_LANES // LANE_TILE, SCAN_CHUNK, LANE_TILE), F32),
                        pltpu.VMEM((SSM_LANES // LANE_TILE, SCAN_CHUNK, LANE_TILE), F32),
                        pltpu.VMEM((8, SUBLANES, SSM_LANES), F32),
                        pltpu.VMEM((2, SUBLANES, SSM_LANES), F32)],
        compiler_params=_cparams(("parallel", "arbitrary")),
        name="ssm_prompt",
    )(u, pm, pm.T, ab, bre, bim, cre, cim, d_skip)


def _ssm_sample_kernel(u_ref, ab_ref, bre_ref, bim_ref, cre_ref, cim_ref, d_ref, h0r_ref, h0i_ref,
                       y_ref, hr_ref, hi_ref, *, steps):
    half = SSM_LANES // 2
    ar, ai = ab_ref[0:1, :], ab_ref[1:2, :]
    hr_ref[...] = h0r_ref[...]
    hi_ref[...] = h0i_ref[...]
    for t in range(steps):
        u = u_ref[:, t * SSM_WIDTH:(t + 1) * SSM_WIDTH]
        ub = u.astype(BF16)
        for h in range(2):
            hs = slice(h * half, (h + 1) * half)
            cs = slice(h * 256, (h + 1) * 256)
            uh = ub[:, cs]
            hr, hi = hr_ref[:, hs], hi_ref[:, hs]
            a_r, a_i = ar[:, hs], ai[:, hs]
            nr = (a_r * hr - a_i * hi) + _dot(uh, bre_ref[h])
            ni = (a_r * hi + a_i * hr) + _dot(uh, bim_ref[h])
            hr_ref[:, hs] = nr
            hi_ref[:, hs] = ni
            y = _dot(nr.astype(BF16), cre_ref[h]) - _dot(ni.astype(BF16), cim_ref[h])
            y_ref[:, t * SSM_WIDTH + h * 256:t * SSM_WIDTH + (h + 1) * 256] = y + d_ref[:, cs] * u[:, cs]


def _ssm_sample_call(u, ab, bre, bim, cre, cim, d_skip, h0r, h0i, steps):
    n = u.shape[0]
    st = jax.ShapeDtypeStruct((n, SSM_LANES), F32)
    return pl.pallas_call(
        functools.partial(_ssm_sample_kernel, steps=steps),
        out_shape=[jax.ShapeDtypeStruct((n, steps * SSM_WIDTH), F32), st, st],
        compiler_params=pltpu.CompilerParams(vmem_limit_bytes=VMEM_LIMIT),
        name="ssm_sample",
    )(u, ab, bre, bim, cre, cim, d_skip, h0r, h0i)


_NT = (((1,), (1,)), ((), ()))


def _attn_prompt_kernel(q_ref, kp_ref, kc_ref, vp_ref, vc_ref, o_ref, lse_ref, *, dil, q_blocks):
    b = pl.program_id(2)
    qi = lax.broadcasted_iota(jnp.int32, (N_BACK, 2 * N_BACK), 0)
    kj = lax.broadcasted_iota(jnp.int32, (N_BACK, 2 * N_BACK), 1)
    band = (kj >= qi) & (kj <= qi + N_BACK)
    band_first = band & (kj >= jnp.where(b > 0, 0, N_BACK))
    lane = lax.broadcasted_iota(jnp.int32, (N_BACK, PAIR_WIDTH), 1)
    head0 = lane < HEAD_DIM

    def rows(j, r):
        start = j * N_BACK * dil + r
        return pl.ds(start, N_BACK, stride=dil) if dil > 1 else pl.ds(start, N_BACK)

    def one(j, r):
        cur = rows(j, r)
        q = (q_ref[cur, :] * (HEAD_DIM ** -0.5)).astype(BF16)
        k_prev = kp_ref[rows(0, r), :] if j == 0 else kc_ref[rows(j - 1, r), :]
        v_prev = vp_ref[rows(0, r), :] if j == 0 else vc_ref[rows(j - 1, r), :]
        kk = jnp.concatenate([k_prev, kc_ref[cur, :]], axis=0).astype(BF16)
        vv = jnp.concatenate([v_prev, vc_ref[cur, :]], axis=0).astype(BF16)
        valid = band_first if j == 0 else band
        outs, lses = [], []
        for sub in range(2):
            in_head = head0 if sub == 0 else jnp.logical_not(head0)
            qm = jnp.where(in_head, q, jnp.zeros_like(q))
            s = lax.dot_general(qm, kk, _NT, preferred_element_type=F32)
            s = jnp.where(valid, s, NEG_INF)
            m = jnp.max(s, axis=-1, keepdims=True)
            p = jnp.exp(s - m)
            den = jnp.sum(p, axis=-1, keepdims=True)
            outs.append(_dot(p.astype(BF16), vv) / den)
            lses.append(jnp.broadcast_to(m + jnp.log(den), (N_BACK, PAIR_WIDTH)))
        o_ref[cur, :] = jnp.where(head0, outs[0], outs[1])
        lse_ref[cur, :] = jnp.where(head0, lses[0], lses[1])

    def residue(r, carry):
        for j in range(q_blocks):
            one(j, r)
        return carry

    if dil <= ATTN_UNROLL:
        for r in range(dil):
            residue(r, 0)
    else:
        lax.fori_loop(0, dil, residue, 0, unroll=ATTN_UNROLL)


ATTN_UNROLL = 4
ATTN_Q_BLOCKS = {1: 8, 4: 2, 16: 1}


def _attn_prompt_call(qkv, n_seq, seq, dil):
    q_blocks = ATTN_Q_BLOCKS[dil]
    prev_rows = N_BACK * dil
    rows = prev_rows * q_blocks
    nbk = seq // rows
    cur = lambda a: pl.BlockSpec((None, None, rows, PAIR_WIDTH), lambda n, hp, b: (a, hp, n * nbk + b, 0))
    prev = lambda a: pl.BlockSpec(
        (None, None, prev_rows, PAIR_WIDTH),
        lambda n, hp, b: (a, hp, jnp.maximum((n * nbk + b) * q_blocks - 1, n * nbk * q_blocks), 0))
    out_spec = pl.BlockSpec((None, rows, PAIR_WIDTH), lambda n, hp, b: (hp, n * nbk + b, 0))
    out_shape = jax.ShapeDtypeStruct((2, n_seq * seq, PAIR_WIDTH), F32)
    return pl.pallas_call(
        functools.partial(_attn_prompt_kernel, dil=dil, q_blocks=q_blocks),
        grid=(n_seq, 2, nbk),
        in_specs=[cur(0), prev(1), cur(1), prev(2), cur(2)],
        out_specs=[out_spec, out_spec],
        out_shape=[out_shape, out_shape],
        compiler_params=_cparams(("parallel", "parallel", "parallel")),
        name=f"attn_prompt_d{dil}",
    )(qkv, qkv, qkv, qkv, qkv)


def _kv_tail_kernel(x_ref, o_ref):
    o_ref[...] = x_ref[...].T


def _kv_tail_call(qkv, n_seq, seq, keep):
    blocks_per_seq = seq // keep
    return pl.pallas_call(
        _kv_tail_kernel,
        grid=(2, n_seq, 2),
        in_specs=[pl.BlockSpec((None, None, keep, PAIR_WIDTH),
                               lambda a, n, hp: (a + 1, hp, (n + 1) * blocks_per_seq - 1, 0))],
        out_specs=pl.BlockSpec((None, None, None, PAIR_WIDTH, keep), lambda a, n, hp: (a, n, hp, 0, 0)),
        out_shape=jax.ShapeDtypeStruct((2, n_seq, 2, PAIR_WIDTH, keep), F32),
        compiler_params=_cparams(("parallel", "parallel", "parallel")),
        name="kv_tail",
    )(qkv)


SAMPLE_SEQ_BLOCK = {128: 8, 512: 4, 2048: 2}


def _attn_sample_kernel(qkv_ref, kt_ref, vt_ref, o_ref, lse_ref, *, dil, steps):
    nb, _, _, w = kt_ref.shape
    rows = nb * steps
    tok = lax.broadcasted_iota(jnp.int32, (2 * rows, w), 0) % rows
    pos = lax.broadcasted_iota(jnp.int32, (2 * rows, w), 1)
    tokn = lax.broadcasted_iota(jnp.int32, (2 * rows, rows), 0) % rows
    coln = lax.broadcasted_iota(jnp.int32, (2 * rows, rows), 1)
    rowo = lax.broadcasted_iota(jnp.int32, (rows, PAIR_WIDTH), 0)
    head0 = lax.broadcasted_iota(jnp.int32, (rows, PAIR_WIDTH), 1) < HEAD_DIM

    vns, s_buf, s_new = [], {}, []
    for hp in range(2):
        q2 = qkv_ref[0, hp] * (HEAD_DIM ** -0.5)
        q_st = jnp.concatenate([jnp.where(head0, q2, 0.0), jnp.where(head0, 0.0, q2)], axis=0).astype(BF16)
        kn = qkv_ref[1, hp].astype(BF16)
        s_new.append(lax.dot_general(q_st, kn, _NT, preferred_element_type=F32))
        vns.append(qkv_ref[2, hp].astype(BF16))
        for n in range(nb):
            kt = kt_ref[n, 2 * hp:2 * hp + 2].reshape(PAIR_WIDTH, w).astype(BF16)
            s_buf[hp, n] = _dot(q_st, kt)

    probs = {}
    for hp in range(2):
        for n in range(nb):
            t = tok - n * steps
            in_buf = (((pos - t) & (dil - 1)) == 0) & (pos >= t)
            tn = tokn - n * steps
            sn = coln - n * steps
            in_new = ((sn >= 0) & (sn <= tn)) if dil == 1 else (sn == tn)
            s = jnp.where(in_buf, s_buf[hp, n], NEG_INF)
            s2 = jnp.where(in_new, s_new[hp], NEG_INF)
            m = jnp.maximum(jnp.max(s, axis=-1, keepdims=True), jnp.max(s2, axis=-1, keepdims=True))
            p = jnp.exp(s - m)
            p2 = jnp.exp(s2 - m)
            den = jnp.sum(p, axis=-1, keepdims=True) + jnp.sum(p2, axis=-1, keepdims=True)
            probs[hp, n] = (p.astype(BF16), p2.astype(BF16), den, m + jnp.log(den))

    for hp in range(2):
        o_pair = jnp.zeros((rows, PAIR_WIDTH), F32)
        lse_pair = jnp.zeros((rows, PAIR_WIDTH), F32)
        for n in range(nb):
            p, p2, den, lse = probs[hp, n]
            vt = vt_ref[n, 2 * hp:2 * hp + 2].reshape(PAIR_WIDTH, w).astype(BF16)
            o = (lax.dot_general(p, vt, _NT, preferred_element_type=F32) + _dot(p2, vns[hp])) / den
            o = jnp.where(head0, o[:rows], o[rows:])
            lse = jnp.where(head0, jnp.broadcast_to(lse[:rows], (rows, PAIR_WIDTH)),
                            jnp.broadcast_to(lse[rows:], (rows, PAIR_WIDTH)))
            mine = (rowo >= n * steps) & (rowo < (n + 1) * steps)
            o_pair = jnp.where(mine, o, o_pair)
            lse_pair = jnp.where(mine, lse, lse_pair)
        o_ref[hp] = o_pair
        lse_ref[hp] = lse_pair


def _attn_sample_call(qkv, k_cache, v_cache, l, n_seq, steps, dil):
    w = k_cache.shape[2]
    kt = k_cache.transpose(0, 1, 3, 4, 2)
    vt = v_cache.transpose(0, 1, 3, 4, 2)
    nb = SAMPLE_SEQ_BLOCK[w]
    rows = nb * steps
    cache_spec = pl.BlockSpec((None, nb, HEADS, HEAD_DIM, w), lambda i: (l, i, 0, 0, 0))
    out_spec = pl.BlockSpec((2, rows, PAIR_WIDTH), lambda i: (0, i, 0))
    out_shape = jax.ShapeDtypeStruct((2, n_seq * steps, PAIR_WIDTH), F32)
    return pl.pallas_call(
        functools.partial(_attn_sample_kernel, dil=dil, steps=steps),
        grid=(n_seq // nb,),
        in_specs=[pl.BlockSpec((3, 2, rows, PAIR_WIDTH), lambda i: (0, 0, i, 0)), cache_spec, cache_spec],
        out_specs=[out_spec, out_spec],
        out_shape=[out_shape, out_shape],
        compiler_params=_cparams(("parallel",)),
        name=f"attn_sample_d{dil}",
    )(qkv, kt, vt)


def _merge_kernel(x_ref, mod_ref, y_ref, o0_ref, l0_ref, o1_ref, l1_ref, o2_ref, l2_ref,
                  gmix_ref, gffn_ref, wmg_ref, bmg_ref, wglu_ref, wbs_ref, wba_ref, wout_ref,
                  wrh_ref, wrl_ref, br_ref, x1_ref, h2_ref, comb_ref):
    x = x_ref[...]
    tm = x.shape[0]
    hb = (_rmsnorm(x, gmix_ref[...]) * (1.0 + mod_ref[1]) + mod_ref[0]).astype(BF16)

    y = _gelu_tanh(y_ref[...])
    yg = y * _sigmoid(_dot(y.astype(BF16), wglu_ref[...]))
    ssm_branch = _dot(yg.astype(BF16), wbs_ref[...])

    pairs = []
    for hp in range(2):
        l0, l1, l2 = l0_ref[hp], l1_ref[hp], l2_ref[hp]
        lm = jnp.maximum(jnp.maximum(l0, l1), l2)
        e0, e1, e2 = jnp.exp(l0 - lm), jnp.exp(l1 - lm), jnp.exp(l2 - lm)
        pairs.append((e0 * o0_ref[hp] + e1 * o1_ref[hp] + e2 * o2_ref[hp]) / (e0 + e1 + e2))
    attn = jnp.concatenate(pairs, axis=-1)
    attn_branch = _dot(attn.astype(BF16), wba_ref[...])

    g_ssm = _sigmoid(_dot(hb, wmg_ref[:, :D_MODEL]) + bmg_ref[:, :D_MODEL])
    g_attn = _sigmoid(_dot(hb, wmg_ref[:, D_MODEL:]) + bmg_ref[:, D_MODEL:])
    merged = g_ssm * ssm_branch + g_attn * attn_branch
    x1 = x + mod_ref[2] * _dot(merged.astype(BF16), wout_ref[...])
    x1_ref[...] = x1

    h2 = _rmsnorm(x1, gffn_ref[...]) * (1.0 + mod_ref[4]) + mod_ref[3]
    h2_hi = h2.astype(BF16)
    h2_ref[...] = h2_hi
    h2_lo = (h2 - h2_hi.astype(F32)).astype(BF16)
    logits = (_dot(h2_hi, wrh_ref[...]) + _dot(h2_hi, wrl_ref[...]) + _dot(h2_lo, wrh_ref[...])
              + br_ref[...])

    lane = lax.broadcasted_iota(jnp.int32, (tm, ROUTER_LANES), 1)
    lanef = lane.astype(F32)
    low = jnp.float32(-3.0e38)
    is_grp = (lane >= GROUP_LOGIT_LANE0) & (lane < GROUP_LOGIT_LANE0 + N_EXPERT_GROUPS)
    lg = jnp.where(is_grp, logits, low)
    mg = jnp.max(lg, axis=-1, keepdims=True)
    gsel = jnp.min(jnp.where(lg == mg, lanef - GROUP_LOGIT_LANE0, 1e9), axis=-1, keepdims=True)
    pg_sel = 1.0 / jnp.sum(jnp.exp(lg - mg), axis=-1, keepdims=True)
    in_grp = (lane < N_EXPERTS) & ((lane // EXPERTS_PER_GROUP).astype(F32) == gsel)
    le = jnp.where(in_grp, logits, low)
    m1 = jnp.max(le, axis=-1, keepdims=True)
    i1 = jnp.min(jnp.where(le == m1, lanef, 1e9), axis=-1, keepdims=True)
    le2 = jnp.where(lanef == i1, low, le)
    m2 = jnp.max(le2, axis=-1, keepdims=True)
    i2 = jnp.min(jnp.where(le2 == m2, lanef, 1e9), axis=-1, keepdims=True)
    r = jnp.exp(m2 - m1)
    w1 = 1.0 / (1.0 + r)
    w2 = r / (1.0 + r)
    comb = pg_sel * (jnp.where(lanef == i1, w1, 0.0) + jnp.where(lanef == i2, w2, 0.0))
    comb_ref[...] = comb + jnp.where(lane == GROUP_ID_LANE, gsel, 0.0)


def _merge_call(x, mod, y, attn_outs, p, l, tm, tiles_per_seq):
    ntok = x.shape[0]
    tok = lambda w: pl.BlockSpec((tm, w), lambda i: (i, 0))
    lay = lambda *shape: pl.BlockSpec((None,) + shape, lambda i: (l,) + (0,) * len(shape))
    attn_specs = [pl.BlockSpec((2, tm, PAIR_WIDTH), lambda i: (0, i, 0))] * 6
    return pl.pallas_call(
        _merge_kernel,
        grid=(ntok // tm,),
        in_specs=[tok(D_MODEL), _mod_spec(mod, tm, tiles_per_seq), tok(SSM_WIDTH)] + attn_specs + [
            lay(1, D_MODEL), lay(1, D_MODEL),
            lay(D_MODEL, 2 * D_MODEL), lay(1, 2 * D_MODEL),
            lay(SSM_WIDTH, SSM_WIDTH), lay(SSM_WIDTH, D_MODEL), lay(GROUP_WIDTH, D_MODEL),
            lay(D_MODEL, D_MODEL),
            lay(D_MODEL, ROUTER_LANES), lay(D_MODEL, ROUTER_LANES), lay(1, ROUTER_LANES),
        ],
        out_specs=[tok(D_MODEL), tok(D_MODEL), tok(ROUTER_LANES)],
        out_shape=[jax.ShapeDtypeStruct((ntok, D_MODEL), F32),
                   jax.ShapeDtypeStruct((ntok, D_MODEL), BF16),
                   jax.ShapeDtypeStruct((ntok, ROUTER_LANES), F32)],
        compiler_params=_cparams(("parallel",)),
        name="merge_router",
    )(x, mod, y, *attn_outs, p['g_mix'], p['g_ffn'], p['w_mgate'], p['b_mgate'], p['w_glu'],
      p['w_br_ssm'], p['w_br_attn'], p['w_out'], p['w_router_hi'], p['w_router_lo'], p['b_router'])


MOE_EXPERT_BLOCK = 4
MOE_ROW_CHUNK = 128
GROUP_ID_LANE = 64
_TN = (((0,), (0,)), ((), ()))


def _split3(x):
    a = x.astype(BF16)
    r = x - a.astype(F32)
    b = r.astype(BF16)
    return a, b, (r - b.astype(F32)).astype(BF16)


def _moe_kernel(h2_ref, comb_ref, x1_ref, mod_ref, wg_ref, wu_ref, wd_ref, gfin_ref, out_ref,
                perm, hs, combs, ys, bounds, *, final):
    s = pl.program_id(1)
    tm = h2_ref.shape[0]
    ch = MOE_ROW_CHUNK

    @pl.when(s == 0)
    def _():
        comb = comb_ref[...]
        lane = lax.broadcasted_iota(jnp.int32, (tm, ROUTER_LANES), 1)
        gid = jnp.sum(jnp.where(lane == GROUP_ID_LANE, comb, 0.0), axis=-1, keepdims=True)
        onehot = jnp.where(lane.astype(F32) == gid, 1.0, 0.0)
        row = lax.broadcasted_iota(jnp.int32, (tm, tm), 0)
        col = lax.broadcasted_iota(jnp.int32, (tm, tm), 1)
        earlier = jnp.where(col < row, 1.0, 0.0).astype(BF16)
        rank = _dot(earlier, onehot.astype(BF16))
        tot = jnp.sum(onehot, axis=0, keepdims=True)
        lane1 = lax.broadcasted_iota(jnp.int32, (1, ROUTER_LANES), 1)
        start = jnp.zeros((1, ROUTER_LANES), F32)
        run = jnp.zeros((1, 1), F32)
        for g in range(N_EXPERT_GROUPS):
            bounds[2 * g] = run[0, 0].astype(jnp.int32)
            start = start + jnp.where(lane1 == g, run, 0.0)
            run = run + jnp.sum(jnp.where(lane1 == g, tot, 0.0), axis=-1, keepdims=True)
            bounds[2 * g + 1] = run[0, 0].astype(jnp.int32)
        pos = jnp.sum(onehot * (rank + start), axis=-1, keepdims=True)
        perm[...] = jnp.where(col.astype(F32) == pos, 1.0, 0.0).astype(BF16)
        hs[...] = lax.dot_general(perm[...], h2_ref[...], _TN, preferred_element_type=F32).astype(BF16)
        parts = lax.dot_general(perm[...], jnp.concatenate(_split3(comb), axis=-1), _TN,
                                preferred_element_type=F32)
        combs[...] = (parts[:, :ROUTER_LANES] + parts[:, ROUTER_LANES:2 * ROUTER_LANES]) + parts[:, 2 * ROUTER_LANES:]
        ys[...] = jnp.zeros_like(ys)

    g = s // (EXPERTS_PER_GROUP // MOE_EXPERT_BLOCK)
    lo = bounds[2 * g] // ch
    hi = (bounds[2 * g + 1] + (ch - 1)) // ch
    lane = lax.broadcasted_iota(jnp.int32, (ch, ROUTER_LANES), 1)

    def chunk(c, carry):
        r0 = pl.multiple_of(c * ch, ch)
        h = hs[pl.ds(r0, ch), :]
        comb = combs[pl.ds(r0, ch), :]
        hids = []
        for j in range(MOE_EXPERT_BLOCK):
            gt = _dot(h, wg_ref[j])
            w = jnp.sum(jnp.where(lane == s * MOE_EXPERT_BLOCK + j, comb, 0.0), axis=-1, keepdims=True)
            hids.append((gt * _sigmoid(gt) * _dot(h, wu_ref[j]) * w).astype(BF16))
        ys[pl.ds(r0, ch), :] += _dot(jnp.concatenate(hids, axis=-1), wd_ref[...])
        return carry

    lax.fori_loop(lo, hi, chunk, 0)

    @pl.when(s == pl.num_programs(1) - 1)
    def _():
        x2 = x1_ref[...] + mod_ref[...] * _dot(perm[...], ys[...].astype(BF16))
        out_ref[...] = _rmsnorm(x2, gfin_ref[...]) if final else x2


def _moe_call(h2, comb, x1, mod, p, l, tm, tiles_per_seq, final):
    ntok = h2.shape[0]
    eb = MOE_EXPERT_BLOCK
    tok = lambda w: pl.BlockSpec((tm, w), lambda i, s: (i, 0))
    gate_piece = 5
    if mod.ndim == 4:
        mod_spec = pl.BlockSpec((None, None, 1, D_MODEL), lambda i, s: (gate_piece, i // tiles_per_seq, 0, 0))
    else:
        mod_spec = pl.BlockSpec((None, tm, D_MODEL), lambda i, s: (gate_piece, i, 0))
    w_down = p['w_exp_down'].reshape(DEPTH, N_EXPERTS // eb, eb * EXPERT_FF, D_MODEL)
    return pl.pallas_call(
        functools.partial(_moe_kernel, final=final),
        grid=(ntok // tm, N_EXPERTS // eb),
        in_specs=[tok(D_MODEL), tok(ROUTER_LANES), tok(D_MODEL), mod_spec,
                  pl.BlockSpec((None, eb, D_MODEL, EXPERT_FF), lambda i, s: (l, s, 0, 0)),
                  pl.BlockSpec((None, eb, D_MODEL, EXPERT_FF), lambda i, s: (l, s, 0, 0)),
                  pl.BlockSpec((None, None, eb * EXPERT_FF, D_MODEL), lambda i, s: (l, s, 0, 0)),
                  pl.BlockSpec((1, D_MODEL), lambda i, s: (0, 0))],
        out_specs=tok(D_MODEL),
        out_shape=jax.ShapeDtypeStruct((ntok, D_MODEL), F32),
        scratch_shapes=[pltpu.VMEM((tm, tm), BF16), pltpu.VMEM((tm, D_MODEL), BF16),
                        pltpu.VMEM((tm, ROUTER_LANES), F32), pltpu.VMEM((tm, D_MODEL), F32),
                        pltpu.SMEM((2 * N_EXPERT_GROUPS,), jnp.int32)],
        compiler_params=_cparams(("parallel", "arbitrary")),
        name="moe_experts",
    )(h2, comb, x1, mod, p['w_exp_gate'], p['w_exp_up'], w_down, p['g_final'])


def _run_trunk(x, mod_layers, p, ssm, n_seq, seq, caches, h0):
    ntok = n_seq * seq
    prompt = caches is None
    tm_proj = 512 if prompt else 256
    tm_merge = 512 if prompt else 256
    tm_moe = 1024 if prompt else 512
    states = []
    for l in range(DEPTH):
        mod = mod_layers[l]
        u, *qkv = _inproj_call(x, mod, p['g_mix'], p['w_in'], l, tm_proj, seq // tm_proj if prompt else 1)
        ab, bre, bim, cre, cim, d_skip = ssm[l]
        if prompt:
            y, hre, him = _ssm_prompt_call(u, ab, bre, bim, cre, cim, d_skip, n_seq, seq)
            hre, him = hre[:, 0], him[:, 0]
            attn = []
            for g in range(N_GROUPS):
                attn += list(_attn_prompt_call(qkv[g], n_seq, seq, DILATIONS[g]))
        else:
            y, hre, him = _ssm_sample_call(u.reshape(n_seq, seq * SSM_WIDTH), ab, bre, bim, cre, cim,
                                           d_skip, h0[0][l], h0[1][l], seq)
            y = y.reshape(ntok, SSM_WIDTH)
            attn = []
            for g in range(N_GROUPS):
                attn += list(_attn_sample_call(qkv[g], caches[2 * g], caches[2 * g + 1], l, n_seq, seq,
                                               DILATIONS[g]))
        x1, h2, comb = _merge_call(x, mod, y, attn, p, l, tm_merge, seq // tm_merge if prompt else 1)
        x = _moe_call(h2, comb, x1, mod, p, l, tm_moe, seq // tm_moe if prompt else 1,
                      final=(l == DEPTH - 1))
        kv = []
        for g in range(N_GROUPS):
            keep = min(WINDOWS[g], seq)
            if prompt:
                tails = _kv_tail_call(qkv[g], n_seq, seq, keep).reshape(2, n_seq, HEADS, HEAD_DIM, keep)
                kv += [tails[0].transpose(0, 3, 1, 2), tails[1].transpose(0, 3, 1, 2)]
            else:
                for a in (1, 2):
                    new = qkv[g][a].reshape(2, n_seq, seq, 2, HEAD_DIM)
                    kv.append(new.transpose(1, 2, 0, 3, 4).reshape(n_seq, seq, HEADS, HEAD_DIM))
        states.append(kv + [hre.reshape(n_seq, N_SSM_GROUPS, SSM_STATE),
                            him.reshape(n_seq, N_SSM_GROUPS, SSM_STATE)])
    new_state = [jnp.stack([states[l][i] for l in range(DEPTH)], axis=0) for i in range(len(states[0]))]
    return x, new_state


def kernel(x_prompt, x_sample, cache_k_w128, cache_v_w128, cache_k_w512, cache_v_w512,
           cache_k_w2048, cache_v_w2048, state_ssm_re, state_ssm_im, c_prompt, c_sample,
           w_ada, b_ada, g_norm_mix, g_norm_ffn, g_final, w_in, ssm_a_re, ssm_a_im, ssm_b_re,
           ssm_b_im, ssm_c_re, ssm_c_im, ssm_log_step, ssm_d, w_glu, w_br_ssm, w_br_attn,
           w_mgate, b_mgate, w_out, w_router_grp, b_router_grp, w_router_exp, b_router_exp,
           w_exp_gate, w_exp_up, w_exp_down):
    batch, seq, _ = x_prompt.shape
    dec_batch, dec_seq, _ = x_sample.shape

    pad = -batch % SUBLANES
    c_p = jnp.concatenate([c_prompt, jnp.zeros((pad, D_MODEL), F32)], axis=0)
    mod_p, mod_s = _ada_call(c_p, jnp.repeat(c_sample, dec_seq, axis=0), w_ada, b_ada)
    mod_prompt = [mod_p[l, :, :batch].reshape(6, batch, 1, D_MODEL) for l in range(DEPTH)]
    mod_sample = [mod_s[l] for l in range(DEPTH)]

    abr, abi, bbr, bbi = _ssm_param_call(ssm_a_re, ssm_a_im, ssm_log_step, ssm_b_re, ssm_b_im)
    ssm = []
    for l in range(DEPTH):
        pick = lambda a: a[l].reshape(N_SSM_GROUPS, SSM_STATE, SSM_GROUP_CH)
        ab = jnp.stack([pick(abr)[:, :, 0].reshape(SSM_LANES), pick(abi)[:, :, 0].reshape(SSM_LANES)])
        bre = _block_diag_halves(pick(bbr).transpose(0, 2, 1), BF16)
        bim = _block_diag_halves(pick(bbi).transpose(0, 2, 1), BF16)
        cre = _block_diag_halves(ssm_c_re[l].transpose(0, 2, 1), BF16)
        cim = _block_diag_halves(ssm_c_im[l].transpose(0, 2, 1), BF16)
        ssm.append((ab, bre, bim, cre, cim, ssm_d[l].reshape(1, SSM_WIDTH)))

    w_router = jnp.concatenate(
        [w_router_exp, w_router_grp,
         jnp.zeros((DEPTH, D_MODEL, ROUTER_LANES - N_EXPERTS - N_EXPERT_GROUPS), F32)], axis=-1)
    b_router = jnp.concatenate(
        [b_router_exp, b_router_grp,
         jnp.zeros((DEPTH, ROUTER_LANES - N_EXPERTS - N_EXPERT_GROUPS), F32)], axis=-1)
    w_router_hi = w_router.astype(BF16)
    p = dict(
        g_mix=g_norm_mix.reshape(DEPTH, 1, D_MODEL), g_ffn=g_norm_ffn.reshape(DEPTH, 1, D_MODEL),
        g_final=g_final.reshape(1, D_MODEL), w_in=w_in.astype(BF16), w_glu=w_glu.astype(BF16),
        w_br_ssm=w_br_ssm.astype(BF16), w_br_attn=w_br_attn.astype(BF16),
        w_mgate=w_mgate.astype(BF16), b_mgate=b_mgate.reshape(DEPTH, 1, 2 * D_MODEL),
        w_out=w_out.astype(BF16), w_router_hi=w_router_hi,
        w_router_lo=(w_router - w_router_hi.astype(F32)).astype(BF16),
        b_router=b_router.reshape(DEPTH, 1, ROUTER_LANES),
        w_exp_gate=w_exp_gate.astype(BF16), w_exp_up=w_exp_up.astype(BF16),
        w_exp_down=w_exp_down.astype(BF16))

    y_prompt, pstate = _run_trunk(x_prompt.reshape(batch * seq, D_MODEL), mod_prompt, p, ssm,
                                  batch, seq, None, None)
    caches = [cache_k_w128, cache_v_w128, cache_k_w512, cache_v_w512, cache_k_w2048, cache_v_w2048]
    h0 = (state_ssm_re.reshape(DEPTH, dec_batch, SSM_LANES), state_ssm_im.reshape(DEPTH, dec_batch, SSM_LANES))
    y_sample, sstate = _run_trunk(x_sample.reshape(dec_batch * dec_seq, D_MODEL), mod_sample, p, ssm,
                                  dec_batch, dec_seq, caches, h0)
    return (y_prompt.reshape(batch, seq, D_MODEL), y_sample.reshape(dec_batch, dec_seq, D_MODEL),
            *pstate, *sstate)
```

```python
import functools
import math

import jax
import jax.numpy as jnp
from jax import lax
from jax.experimental import pallas as pl
from jax.experimental.pallas import tpu as pltpu

F32 = jnp.float32
BF16 = jnp.bfloat16

D_MODEL = 1024
DEPTH = 2
SSM_WIDTH = 512
SSM_GROUP_CH = 16
N_SSM_GROUPS = 32
SSM_STATE = 64
SSM_LANES = N_SSM_GROUPS * SSM_STATE
HEAD_DIM = 64
HEADS = 4
GROUP_WIDTH = HEADS * HEAD_DIM
PAIR_WIDTH = 2 * HEAD_DIM
WINDOWS = (128, 512, 2048)
DILATIONS = (1, 4, 16)
N_BACK = 128
N_GROUPS = 3
QKV_WIDTH = N_GROUPS * GROUP_WIDTH
IN_PROJ_WIDTH = SSM_WIDTH + 3 * QKV_WIDTH
N_EXPERT_GROUPS = 4
EXPERTS_PER_GROUP = 8
N_EXPERTS = 32
EXPERT_FF = 256
RMS_EPS = 1e-6
NEG_INF = -1e30
ROUTER_LANES = 128
GROUP_LOGIT_LANE0 = N_EXPERTS

SUBLANES = 8
VMEM_LIMIT = 56 * 1024 * 1024


def _cparams(sem):
    return pltpu.CompilerParams(dimension_semantics=sem, vmem_limit_bytes=VMEM_LIMIT)


def _sigmoid(x):
    return 1.0 / (1.0 + jnp.exp(-x))


def _gelu_tanh(x):
    return 0.5 * x * (1.0 + jnp.tanh(math.sqrt(2.0 / math.pi) * (x + 0.044715 * (x * x * x))))


def _dot(a, b):
    return jnp.dot(a, b, preferred_element_type=F32)


def _rmsnorm(x, g):
    ms = jnp.mean(x * x, axis=-1, keepdims=True)
    return x * lax.rsqrt(ms + RMS_EPS) * g


def _ada_kernel(cp_ref, cs_ref, w_ref, b_ref, op_ref, os_ref):
    w = w_ref[...].astype(BF16)
    for c_ref, o_ref in ((cp_ref, op_ref), (cs_ref, os_ref)):
        c = c_ref[...]
        o_ref[...] = _dot((c * _sigmoid(c)).astype(BF16), w) + b_ref[...]


def _ada_call(c_prompt_rows, c_sample_rows, w_ada, b_ada):
    rp, rs = c_prompt_rows.shape[0], c_sample_rows.shape[0]
    out = lambda r: pl.BlockSpec((None, None, r, D_MODEL), lambda l, j: (l, j, 0, 0))
    return pl.pallas_call(
        _ada_kernel,
        grid=(DEPTH, 6),
        in_specs=[
            pl.BlockSpec((rp, D_MODEL), lambda l, j: (0, 0)),
            pl.BlockSpec((rs, D_MODEL), lambda l, j: (0, 0)),
            pl.BlockSpec((None, D_MODEL, D_MODEL), lambda l, j: (l, 0, j)),
            pl.BlockSpec((None, 1, D_MODEL), lambda l, j: (l, 0, j)),
        ],
        out_specs=[out(rp), out(rs)],
        out_shape=[jax.ShapeDtypeStruct((DEPTH, 6, rp, D_MODEL), F32),
                   jax.ShapeDtypeStruct((DEPTH, 6, rs, D_MODEL), F32)],
        compiler_params=_cparams(("parallel", "parallel")),
        name="ada_mod",
    )(c_prompt_rows, c_sample_rows, w_ada, b_ada.reshape(DEPTH, 1, 6 * D_MODEL))


def _ssm_param_kernel(ar_ref, ai_ref, ls_ref, br_ref, bi_ref, abr_ref, abi_ref, bbr_ref, bbi_ref):
    lr, li = ar_ref[...], ai_ref[...]
    dt = jnp.exp(ls_ref[...])
    mag = jnp.exp(lr * dt)
    abr = mag * jnp.cos(li * dt)
    abi = mag * jnp.sin(li * dt)
    den = lr * lr + li * li
    er, ei = abr - 1.0, abi
    fr = (er * lr + ei * li) / den
    fi = (ei * lr - er * li) / den
    br, bi = br_ref[...], bi_ref[...]
    abr_ref[...] = abr
    abi_ref[...] = abi
    bbr_ref[...] = fr * br - fi * bi
    bbi_ref[...] = fr * bi + fi * br


def _ssm_param_call(a_re, a_im, log_step, b_re, b_im):
    w = SSM_STATE * SSM_GROUP_CH
    rep = lambda a: jnp.repeat(a, SSM_GROUP_CH, axis=-1)
    big = pl.BlockSpec((None, N_SSM_GROUPS, w), lambda l: (l, 0, 0))
    shp = jax.ShapeDtypeStruct((DEPTH, N_SSM_GROUPS, w), F32)
    return pl.pallas_call(
        _ssm_param_kernel,
        grid=(DEPTH,),
        in_specs=[big, big, pl.BlockSpec((None, N_SSM_GROUPS, 1), lambda l: (l, 0, 0)), big, big],
        out_specs=[big, big, big, big],
        out_shape=[shp, shp, shp, shp],
        compiler_params=_cparams(("parallel",)),
        name="ssm_params",
    )(rep(a_re), rep(a_im), log_step.reshape(DEPTH, N_SSM_GROUPS, 1),
      b_re.reshape(DEPTH, N_SSM_GROUPS, w), b_im.reshape(DEPTH, N_SSM_GROUPS, w))


def _block_diag_halves(blocks, dtype):
    g, r, c = blocks.shape
    half = g // 2
    eye = jnp.eye(half, dtype=bool)

    def one(b):
        m = jnp.where(eye[:, None, :, None], b[:, :, None, :], 0.0)
        return m.reshape(half * r, half * c)

    return jnp.stack([one(blocks[:half]), one(blocks[half:])]).astype(dtype)


def _inproj_kernel(x_ref, mod_ref, g_ref, w_ref, u_ref, a0_ref, a1_ref, a2_ref):
    h = _rmsnorm(x_ref[...], g_ref[...]) * (1.0 + mod_ref[1]) + mod_ref[0]
    hb = h.astype(BF16)
    u_ref[...] = _dot(hb, w_ref[:, :SSM_WIDTH])
    for g, ref in enumerate((a0_ref, a1_ref, a2_ref)):
        for a in range(3):
            c0 = SSM_WIDTH + a * QKV_WIDTH + g * GROUP_WIDTH
            val = _dot(hb, w_ref[:, c0:c0 + GROUP_WIDTH])
            ref[a, 0] = val[:, :PAIR_WIDTH]
            ref[a, 1] = val[:, PAIR_WIDTH:]


def _mod_spec(mod, tm, tiles_per_seq):
    if mod.ndim == 4:
        return pl.BlockSpec((6, None, 1, D_MODEL), lambda i: (0, i // tiles_per_seq, 0, 0))
    return pl.BlockSpec((6, tm, D_MODEL), lambda i: (0, i, 0))


def _inproj_call(x, mod, g_mix, w_in, l, tm, tiles_per_seq):
    ntok = x.shape[0]
    grp = jax.ShapeDtypeStruct((3, 2, ntok, PAIR_WIDTH), F32)
    grp_spec = pl.BlockSpec((3, 2, tm, PAIR_WIDTH), lambda i: (0, 0, i, 0))
    return pl.pallas_call(
        _inproj_kernel,
        grid=(ntok // tm,),
        in_specs=[
            pl.BlockSpec((tm, D_MODEL), lambda i: (i, 0)),
            _mod_spec(mod, tm, tiles_per_seq),
            pl.BlockSpec((None, 1, D_MODEL), lambda i: (l, 0, 0)),
            pl.BlockSpec((None, D_MODEL, IN_PROJ_WIDTH), lambda i: (l, 0, 0)),
        ],
        out_specs=[pl.BlockSpec((tm, SSM_WIDTH), lambda i: (i, 0)), grp_spec, grp_spec, grp_spec],
        out_shape=[jax.ShapeDtypeStruct((ntok, SSM_WIDTH), F32), grp, grp, grp],
        compiler_params=_cparams(("parallel",)),
        name="in_proj",
    )(x, mod, g_mix, w_in)


LANE_TILE = 128
SCAN_CHUNK = 256
SEG_LEN = SCAN_CHUNK // SUBLANES
SCAN_TILE_GROUP = 4


def _cmul(ar, ai, br, bi):
    return ar * br - ai * bi, ar * bi + ai * br


def _ssm_prompt_kernel(u_ref, pm_ref, pmt_ref, ab_ref, bre_ref, bim_ref, cre_ref, cim_ref, d_ref,
                       y_ref, hre_ref, him_ref, bur, bui, tab, carry):
    c = pl.program_id(1)
    half = SSM_LANES // 2
    tiles_per_half = half // LANE_TILE

    @pl.when(c == 0)
    def _():
        shp = (SUBLANES, SSM_LANES)
        sub = lax.broadcasted_iota(jnp.int32, shp, 0)
        b_r = jnp.broadcast_to(ab_ref[0:1, :], shp)
        b_i = jnp.broadcast_to(ab_ref[1:2, :], shp)
        for _ in range(SEG_LEN.bit_length() - 1):
            b_r, b_i = _cmul(b_r, b_i, b_r, b_i)
        p_r, p_i = b_r, b_i
        ps_r, ps_i = jnp.ones(shp, F32), jnp.zeros(shp, F32)
        for s in range(1, SUBLANES):
            if s in (1, 2, 4):
                k = {1: 0, 2: 2, 4: 4}[s]
                tab[k] = jnp.where(sub >= s, p_r, 0.0)
                tab[k + 1] = jnp.where(sub >= s, p_i, 0.0)
            ps_r = jnp.where(sub >= s, p_r, ps_r)
            ps_i = jnp.where(sub >= s, p_i, ps_i)
            p_r, p_i = _cmul(p_r, p_i, b_r, b_i)
        tab[6] = ps_r
        tab[7] = ps_i
        carry[...] = jnp.zeros_like(carry)

    u = u_ref[...]
    ub = _dot(pm_ref[...], u.astype(BF16)).astype(BF16)
    for h in range(2):
        uh = ub[:, h * 256:(h + 1) * 256]
        res_r = _dot(uh, bre_ref[h])
        res_i = _dot(uh, bim_ref[h])
        for j in range(tiles_per_half):
            bur[h * tiles_per_half + j] = res_r[:, j * LANE_TILE:(j + 1) * LANE_TILE]
            bui[h * tiles_per_half + j] = res_i[:, j * LANE_TILE:(j + 1) * LANE_TILE]

    vshape = (SUBLANES, LANE_TILE)
    sub1 = lax.broadcasted_iota(jnp.int32, vshape, 0) >= 1
    for g0 in range(0, SSM_LANES // LANE_TILE, SCAN_TILE_GROUP):
        tiles = list(range(g0, g0 + SCAN_TILE_GROUP))
        lanes = [slice(lt * LANE_TILE, (lt + 1) * LANE_TILE) for lt in tiles]
        a_r = [jnp.broadcast_to(ab_ref[0:1, sl], vshape) for sl in lanes]
        a_i = [jnp.broadcast_to(ab_ref[1:2, sl], vshape) for sl in lanes]

        def step(t, hs, store):
            out = []
            for j, lt in enumerate(tiles):
                rows = pl.ds(pl.multiple_of(t * SUBLANES, SUBLANES), SUBLANES)
                hr, hi = hs[2 * j], hs[2 * j + 1]
                nr = (a_r[j] * hr - a_i[j] * hi) + bur[lt, rows, :]
                ni = (a_r[j] * hi + a_i[j] * hr) + bui[lt, rows, :]
                if store:
                    bur[lt, rows, :] = nr
                    bui[lt, rows, :] = ni
                out += [nr, ni]
            return tuple(out)

        zeros = tuple(jnp.zeros(vshape, F32) for _ in range(2 * SCAN_TILE_GROUP))
        ends = lax.fori_loop(0, SEG_LEN, lambda t, hs: step(t, hs, False), zeros, unroll=True)

        starts = []
        for j, sl in enumerate(lanes):
            fr, fi = ends[2 * j], ends[2 * j + 1]
            for k in (1, 2, 4):
                ti = {1: 0, 2: 2, 4: 4}[k]
                mr, mi = tab[ti, :, sl], tab[ti + 1, :, sl]
                rr = pltpu.roll(fr, k, axis=0)
                ri = pltpu.roll(fi, k, axis=0)
                fr, fi = fr + (mr * rr - mi * ri), fi + (mr * ri + mi * rr)
            cr, ci = carry[0, :, sl], carry[1, :, sl]
            ps_r, ps_i = tab[6, :, sl], tab[7, :, sl]
            sr = (ps_r * cr - ps_i * ci) + jnp.where(sub1, pltpu.roll(fr, 1, axis=0), 0.0)
            si = (ps_r * ci + ps_i * cr) + jnp.where(sub1, pltpu.roll(fi, 1, axis=0), 0.0)
            starts += [sr, si]

        last = lax.fori_loop(0, SEG_LEN, lambda t, hs: step(t, hs, True), tuple(starts), unroll=True)
        for j, sl in enumerate(lanes):
            carry[0, :, sl] = jnp.broadcast_to(last[2 * j][SUBLANES - 1:SUBLANES, :], vshape)
            carry[1, :, sl] = jnp.broadcast_to(last[2 * j + 1][SUBLANES - 1:SUBLANES, :], vshape)

    for h in range(2):
        cs = slice(h * 256, (h + 1) * 256)
        tl = range(h * tiles_per_half, (h + 1) * tiles_per_half)
        hr = jnp.concatenate([bur[lt] for lt in tl], axis=-1).astype(BF16)
        hi = jnp.concatenate([bui[lt] for lt in tl], axis=-1).astype(BF16)
        y = _dot(hr, cre_ref[h]) - _dot(hi, cim_ref[h])
        y_hi = y.astype(BF16)
        y_lo = (y - y_hi.astype(F32)).astype(BF16)
        y = _dot(pmt_ref[...], y_hi) + _dot(pmt_ref[...], y_lo)
        y_ref[:, cs] = y + d_ref[:, cs] * u[:, cs]

    @pl.when(c == pl.num_programs(1) - 1)
    def _():
        hre_ref[...] = carry[0]
        him_ref[...] = carry[1]


def _ssm_sample_kernel(u_ref, ab_ref, bre_ref, bim_ref, cre_ref, cim_ref, d_ref, h0r_ref, h0i_ref,
                       y_ref, hr_ref, hi_ref, *, steps):
    half = SSM_LANES // 2
    ar, ai = ab_ref[0:1, :], ab_ref[1:2, :]
    hr_ref[...] = h0r_ref[...]
    hi_ref[...] = h0i_ref[...]
    for t in range(steps):
        u = u_ref[:, t * SSM_WIDTH:(t + 1) * SSM_WIDTH]
        ub = u.astype(BF16)
        for h in range(2):
            hs = slice(h * half, (h + 1) * half)
            cs = slice(h * 256, (h + 1) * 256)
            uh = ub[:, cs]
            hr, hi = hr_ref[:, hs], hi_ref[:, hs]
            a_r, a_i = ar[:, hs], ai[:, hs]
            nr = (a_r * hr - a_i * hi) + _dot(uh, bre_ref[h])
            ni = (a_r * hi + a_i * hr) + _dot(uh, bim_ref[h])
            hr_ref[:, hs] = nr
            hi_ref[:, hs] = ni
            y = _dot(nr.astype(BF16), cre_ref[h]) - _dot(ni.astype(BF16), cim_ref[h])
            y_ref[:, t * SSM_WIDTH + h * 256:t * SSM_WIDTH + (h + 1) * 256] = y + d_ref[:, cs] * u[:, cs]


def _ssm_sample_call(u, ab, bre, bim, cre, cim, d_skip, h0r, h0i, steps):
    n = u.shape[0]
    st = jax.ShapeDtypeStruct((n, SSM_LANES), F32)
    return pl.pallas_call(
        functools.partial(_ssm_sample_kernel, steps=steps),
        out_shape=[jax.ShapeDtypeStruct((n, steps * SSM_WIDTH), F32), st, st],
        compiler_params=pltpu.CompilerParams(vmem_limit_bytes=VMEM_LIMIT),
        name="ssm_sample",
    )(u, ab, bre, bim, cre, cim, d_skip, h0r, h0i)


_NT = (((1,), (1,)), ((), ()))


def _attn_prompt_kernel(q_ref, kp_ref, kc_ref, vp_ref, vc_ref, o_ref, lse_ref, *, dil, q_blocks):
    b = pl.program_id(2)
    qi = lax.broadcasted_iota(jnp.int32, (N_BACK, 2 * N_BACK), 0)
    kj = lax.broadcasted_iota(jnp.int32, (N_BACK, 2 * N_BACK), 1)
    band = (kj >= qi) & (kj <= qi + N_BACK)
    band_first = band & (kj >= jnp.where(b > 0, 0, N_BACK))
    lane = lax.broadcasted_iota(jnp.int32, (N_BACK, PAIR_WIDTH), 1)
    head0 = lane < HEAD_DIM

    def rows(j, r):
        start = j * N_BACK * dil + r
        return pl.ds(start, N_BACK, stride=dil) if dil > 1 else pl.ds(start, N_BACK)

    def one(j, r):
        cur = rows(j, r)
        q = (q_ref[cur, :] * (HEAD_DIM ** -0.5)).astype(BF16)
        k_prev = kp_ref[rows(0, r), :] if j == 0 else kc_ref[rows(j - 1, r), :]
        v_prev = vp_ref[rows(0, r), :] if j == 0 else vc_ref[rows(j - 1, r), :]
        kk = jnp.concatenate([k_prev, kc_ref[cur, :]], axis=0).astype(BF16)
        vv = jnp.concatenate([v_prev, vc_ref[cur, :]], axis=0).astype(BF16)
        valid = band_first if j == 0 else band
        outs, lses = [], []
        for sub in range(2):
            in_head = head0 if sub == 0 else jnp.logical_not(head0)
            qm = jnp.where(in_head, q, jnp.zeros_like(q))
            s = lax.dot_general(qm, kk, _NT, preferred_element_type=F32)
            s = jnp.where(valid, s, NEG_INF)
            m = jnp.max(s, axis=-1, keepdims=True)
            p = jnp.exp(s - m)
            den = jnp.sum(p, axis=-1, keepdims=True)
            outs.append(_dot(p.astype(BF16), vv) / den)
            lses.append(jnp.broadcast_to(m + jnp.log(den), (N_BACK, PAIR_WIDTH)))
        o_ref[cur, :] = jnp.where(head0, outs[0], outs[1])
        lse_ref[cur, :] = jnp.where(head0, lses[0], lses[1])

    def residue(r, carry):
        for j in range(q_blocks):
            one(j, r)
        return carry

    if dil <= ATTN_UNROLL:
        for r in range(dil):
            residue(r, 0)
    else:
        lax.fori_loop(0, dil, residue, 0, unroll=ATTN_UNROLL)


ATTN_UNROLL = 4
ATTN_Q_BLOCKS = {1: 8, 4: 2, 16: 1}


def _attn_pair_kernel(q_ref, kp_ref, kc_ref, vp_ref, vc_ref, qkv_s_ref, kt_ref, vt_ref,
                      o_ref, lse_ref, o_s_ref, lse_s_ref, *, dil, q_blocks, steps):
    _attn_sample_kernel(qkv_s_ref, kt_ref, vt_ref, o_s_ref, lse_s_ref, dil=dil, steps=steps)
    _attn_prompt_kernel(q_ref, kp_ref, kc_ref, vp_ref, vc_ref, o_ref, lse_ref, dil=dil, q_blocks=q_blocks)


def _attn_prompt_call(qkv, n_seq, seq, dil, sample=None):
    q_blocks = ATTN_Q_BLOCKS[dil]
    prev_rows = N_BACK * dil
    rows = prev_rows * q_blocks
    nbk = seq // rows
    cur = lambda a: pl.BlockSpec((None, None, rows, PAIR_WIDTH), lambda n, hp, b: (a, hp, n * nbk + b, 0))
    prev = lambda a: pl.BlockSpec(
        (None, None, prev_rows, PAIR_WIDTH),
        lambda n, hp, b: (a, hp, jnp.maximum((n * nbk + b) * q_blocks - 1, n * nbk * q_blocks), 0))
    out_spec = pl.BlockSpec((None, rows, PAIR_WIDTH), lambda n, hp, b: (hp, n * nbk + b, 0))
    out_shape = jax.ShapeDtypeStruct((2, n_seq * seq, PAIR_WIDTH), F32)
    in_specs = [cur(0), prev(1), cur(1), prev(2), cur(2)]
    operands = [qkv, qkv, qkv, qkv, qkv]
    out_specs, out_shapes = [out_spec, out_spec], [out_shape, out_shape]
    if sample is None:
        body = functools.partial(_attn_prompt_kernel, dil=dil, q_blocks=q_blocks)
    else:
        qkv_s, k_cache, v_cache, l, n_dec, steps = sample
        w = k_cache.shape[2]
        nb = SAMPLE_SEQ_BLOCK[w]
        assert n_seq * 2 * nbk * nb == n_dec, "sample blocks must match the prompt attention grid"
        step = lambda n, hp, b: (n * 2 + hp) * nbk + b
        cache_spec = pl.BlockSpec((None, nb, HEADS, HEAD_DIM, w), lambda n, hp, b: (l, step(n, hp, b), 0, 0, 0))
        s_out = pl.BlockSpec((2, nb * steps, PAIR_WIDTH), lambda n, hp, b: (0, step(n, hp, b), 0))
        s_shape = jax.ShapeDtypeStruct((2, n_dec * steps, PAIR_WIDTH), F32)
        in_specs += [pl.BlockSpec((3, 2, nb * steps, PAIR_WIDTH), lambda n, hp, b: (0, 0, step(n, hp, b), 0)),
                     cache_spec, cache_spec]
        operands += [qkv_s, k_cache.transpose(0, 1, 3, 4, 2), v_cache.transpose(0, 1, 3, 4, 2)]
        out_specs += [s_out, s_out]
        out_shapes += [s_shape, s_shape]
        body = functools.partial(_attn_pair_kernel, dil=dil, q_blocks=q_blocks, steps=steps)
    outs = pl.pallas_call(
        body,
        grid=(n_seq, 2, nbk),
        in_specs=in_specs,
        out_specs=out_specs,
        out_shape=out_shapes,
        compiler_params=_cparams(("parallel", "parallel", "parallel")),
        name=f"attn_prompt_d{dil}",
    )(*operands)
    return outs if sample is None else (outs[:2], outs[2:])


def _kv_tail_kernel(x_ref, o_ref):
    o_ref[...] = x_ref[...].T


def _kv_tail_call(qkv, n_seq, seq, keep):
    blocks_per_seq = seq // keep
    return pl.pallas_call(
        _kv_tail_kernel,
        grid=(2, n_seq, 2),
        in_specs=[pl.BlockSpec((None, None, keep, PAIR_WIDTH),
                               lambda a, n, hp: (a + 1, hp, (n + 1) * blocks_per_seq - 1, 0))],
        out_specs=pl.BlockSpec((None, None, None, PAIR_WIDTH, keep), lambda a, n, hp: (a, n, hp, 0, 0)),
        out_shape=jax.ShapeDtypeStruct((2, n_seq, 2, PAIR_WIDTH, keep), F32),
        compiler_params=_cparams(("parallel", "parallel", "parallel")),
        name="kv_tail",
    )(qkv)


SAMPLE_SEQ_BLOCK = {128: 4, 512: 4, 2048: 2}


def _attn_sample_kernel(qkv_ref, kt_ref, vt_ref, o_ref, lse_ref, *, dil, steps):
    nb, _, _, w = kt_ref.shape
    rows = nb * steps
    tok = lax.broadcasted_iota(jnp.int32, (2 * rows, w), 0) % rows
    pos = lax.broadcasted_iota(jnp.int32, (2 * rows, w), 1)
    tokn = lax.broadcasted_iota(jnp.int32, (2 * rows, rows), 0) % rows
    coln = lax.broadcasted_iota(jnp.int32, (2 * rows, rows), 1)
    rowo = lax.broadcasted_iota(jnp.int32, (rows, PAIR_WIDTH), 0)
    head0 = lax.broadcasted_iota(jnp.int32, (rows, PAIR_WIDTH), 1) < HEAD_DIM

    vns, s_buf, s_new = [], {}, []
    for hp in range(2):
        q2 = qkv_ref[0, hp] * (HEAD_DIM ** -0.5)
        q_st = jnp.concatenate([jnp.where(head0, q2, 0.0), jnp.where(head0, 0.0, q2)], axis=0).astype(BF16)
        kn = qkv_ref[1, hp].astype(BF16)
        s_new.append(lax.dot_general(q_st, kn, _NT, preferred_element_type=F32))
        vns.append(qkv_ref[2, hp].astype(BF16))
        for n in range(nb):
            kt = kt_ref[n, 2 * hp:2 * hp + 2].reshape(PAIR_WIDTH, w).astype(BF16)
            s_buf[hp, n] = _dot(q_st, kt)

    probs = {}
    for hp in range(2):
        for n in range(nb):
            t = tok - n * steps
            in_buf = (((pos - t) & (dil - 1)) == 0) & (pos >= t)
            tn = tokn - n * steps
            sn = coln - n * steps
            in_new = ((sn >= 0) & (sn <= tn)) if dil == 1 else (sn == tn)
            s = jnp.where(in_buf, s_buf[hp, n], NEG_INF)
            s2 = jnp.where(in_new, s_new[hp], NEG_INF)
            m = jnp.maximum(jnp.max(s, axis=-1, keepdims=True), jnp.max(s2, axis=-1, keepdims=True))
            p = jnp.exp(s - m)
            p2 = jnp.exp(s2 - m)
            den = jnp.sum(p, axis=-1, keepdims=True) + jnp.sum(p2, axis=-1, keepdims=True)
            probs[hp, n] = (p.astype(BF16), p2.astype(BF16), den, m + jnp.log(den))

    for hp in range(2):
        o_pair = jnp.zeros((rows, PAIR_WIDTH), F32)
        lse_pair = jnp.zeros((rows, PAIR_WIDTH), F32)
        for n in range(nb):
            p, p2, den, lse = probs[hp, n]
            vt = vt_ref[n, 2 * hp:2 * hp + 2].reshape(PAIR_WIDTH, w).astype(BF16)
            o = (lax.dot_general(p, vt, _NT, preferred_element_type=F32) + _dot(p2, vns[hp])) / den
            o = jnp.where(head0, o[:rows], o[rows:])
            lse = jnp.where(head0, jnp.broadcast_to(lse[:rows], (rows, PAIR_WIDTH)),
                            jnp.broadcast_to(lse[rows:], (rows, PAIR_WIDTH)))
            mine = (rowo >= n * steps) & (rowo < (n + 1) * steps)
            o_pair = jnp.where(mine, o, o_pair)
            lse_pair = jnp.where(mine, lse, lse_pair)
        o_ref[hp] = o_pair
        lse_ref[hp] = lse_pair


def _scan_attn_kernel(u_ref, pm_ref, pmt_ref, ab_ref, bre_ref, bim_ref, cre_ref, cim_ref, d_ref,
                      qkv_ref, kt_ref, vt_ref, y_ref, hre_ref, him_ref, o_ref, lse_ref,
                      bur, bui, tab, carry, *, dil, steps):
    _attn_sample_kernel(qkv_ref, kt_ref, vt_ref, o_ref, lse_ref, dil=dil, steps=steps)
    _ssm_prompt_kernel(u_ref, pm_ref, pmt_ref, ab_ref, bre_ref, bim_ref, cre_ref, cim_ref, d_ref,
                       y_ref, hre_ref, him_ref, bur, bui, tab, carry)


def _scan_attn_call(u, ab, bre, bim, cre, cim, d_skip, n_seq, seq,
                    qkv_s, k_cache, v_cache, l, n_dec, steps, dil):
    chunks = seq // SCAN_CHUNK
    w = k_cache.shape[2]
    nb = SAMPLE_SEQ_BLOCK[w]
    rows = nb * steps
    assert n_seq * chunks * nb == n_dec, "sample blocks must match the scan grid"
    kt = k_cache.transpose(0, 1, 3, 4, 2)
    vt = v_cache.transpose(0, 1, 3, 4, 2)
    half = SSM_LANES // 2
    wspec_b = pl.BlockSpec((2, 256, half), lambda n, c: (0, 0, 0))
    wspec_c = pl.BlockSpec((2, half, 256), lambda n, c: (0, 0, 0))
    st_spec = pl.BlockSpec((None, SUBLANES, SSM_LANES), lambda n, c: (n, 0, 0))
    st_shape = jax.ShapeDtypeStruct((n_seq, SUBLANES, SSM_LANES), F32)
    r = jnp.arange(SCAN_CHUNK)
    pm = (r[None, :] == ((r % SUBLANES) * SEG_LEN + r // SUBLANES)[:, None]).astype(BF16)
    perm_spec = pl.BlockSpec((SCAN_CHUNK, SCAN_CHUNK), lambda n, c: (0, 0))
    cache_spec = pl.BlockSpec((None, nb, HEADS, HEAD_DIM, w), lambda n, c: (l, n * chunks + c, 0, 0, 0))
    attn_out = pl.BlockSpec((2, rows, PAIR_WIDTH), lambda n, c: (0, n * chunks + c, 0))
    attn_shape = jax.ShapeDtypeStruct((2, n_dec * steps, PAIR_WIDTH), F32)
    y, hre, him, o, lse = pl.pallas_call(
        functools.partial(_scan_attn_kernel, dil=dil, steps=steps),
        grid=(n_seq, chunks),
        in_specs=[
            pl.BlockSpec((SCAN_CHUNK, SSM_WIDTH), lambda n, c: (n * chunks + c, 0)),
            perm_spec, perm_spec,
            pl.BlockSpec((2, SSM_LANES), lambda n, c: (0, 0)),
            wspec_b, wspec_b, wspec_c, wspec_c,
            pl.BlockSpec((1, SSM_WIDTH), lambda n, c: (0, 0)),
            pl.BlockSpec((3, 2, rows, PAIR_WIDTH), lambda n, c: (0, 0, n * chunks + c, 0)),
            cache_spec, cache_spec,
        ],
        out_specs=[pl.BlockSpec((SCAN_CHUNK, SSM_WIDTH), lambda n, c: (n * chunks + c, 0)),
                   st_spec, st_spec, attn_out, attn_out],
        out_shape=[jax.ShapeDtypeStruct((n_seq * seq, SSM_WIDTH), F32), st_shape, st_shape,
                   attn_shape, attn_shape],
        scratch_shapes=[pltpu.VMEM((SSM_LANES // LANE_TILE, SCAN_CHUNK, LANE_TILE), F32),
                        pltpu.VMEM((SSM_LANES // LANE_TILE, SCAN_CHUNK, LANE_TILE), F32),
                        pltpu.VMEM((8, SUBLANES, SSM_LANES), F32),
                        pltpu.VMEM((2, SUBLANES, SSM_LANES), F32)],
        compiler_params=_cparams(("parallel", "arbitrary")),
        name="scan_prompt_attn_sample",
    )(u, pm, pm.T, ab, bre, bim, cre, cim, d_skip, qkv_s, kt, vt)
    return (y, hre, him), (o, lse)


def _merge_kernel(x_ref, mod_ref, y_ref, o0_ref, l0_ref, o1_ref, l1_ref, o2_ref, l2_ref,
                  gmix_ref, gffn_ref, wmg_ref, bmg_ref, wglu_ref, wbs_ref, wba_ref, wout_ref,
                  wrh_ref, wrl_ref, br_ref, x1_ref, h2_ref, comb_ref):
    x = x_ref[...]
    tm = x.shape[0]
    hb = (_rmsnorm(x, gmix_ref[...]) * (1.0 + mod_ref[1]) + mod_ref[0]).astype(BF16)

    y = _gelu_tanh(y_ref[...])
    yg = y * _sigmoid(_dot(y.astype(BF16), wglu_ref[...]))
    ssm_branch = _dot(yg.astype(BF16), wbs_ref[...])

    pairs = []
    for hp in range(2):
        l0, l1, l2 = l0_ref[hp], l1_ref[hp], l2_ref[hp]
        lm = jnp.maximum(jnp.maximum(l0, l1), l2)
        e0, e1, e2 = jnp.exp(l0 - lm), jnp.exp(l1 - lm), jnp.exp(l2 - lm)
        pairs.append((e0 * o0_ref[hp] + e1 * o1_ref[hp] + e2 * o2_ref[hp]) / (e0 + e1 + e2))
    attn = jnp.concatenate(pairs, axis=-1)
    attn_branch = _dot(attn.astype(BF16), wba_ref[...])

    g_ssm = _sigmoid(_dot(hb, wmg_ref[:, :D_MODEL]) + bmg_ref[:, :D_MODEL])
    g_attn = _sigmoid(_dot(hb, wmg_ref[:, D_MODEL:]) + bmg_ref[:, D_MODEL:])
    merged = g_ssm * ssm_branch + g_attn * attn_branch
    x1 = x + mod_ref[2] * _dot(merged.astype(BF16), wout_ref[...])
    x1_ref[...] = x1

    h2 = _rmsnorm(x1, gffn_ref[...]) * (1.0 + mod_ref[4]) + mod_ref[3]
    h2_hi = h2.astype(BF16)
    h2_ref[...] = h2_hi
    h2_lo = (h2 - h2_hi.astype(F32)).astype(BF16)
    logits = (_dot(h2_hi, wrh_ref[...]) + _dot(h2_hi, wrl_ref[...]) + _dot(h2_lo, wrh_ref[...])
              + br_ref[...])

    lane = lax.broadcasted_iota(jnp.int32, (tm, ROUTER_LANES), 1)
    lanef = lane.astype(F32)
    low = jnp.float32(-3.0e38)
    is_grp = (lane >= GROUP_LOGIT_LANE0) & (lane < GROUP_LOGIT_LANE0 + N_EXPERT_GROUPS)
    lg = jnp.where(is_grp, logits, low)
    mg = jnp.max(lg, axis=-1, keepdims=True)
    gsel = jnp.min(jnp.where(lg == mg, lanef - GROUP_LOGIT_LANE0, 1e9), axis=-1, keepdims=True)
    pg_sel = 1.0 / jnp.sum(jnp.exp(lg - mg), axis=-1, keepdims=True)
    in_grp = (lane < N_EXPERTS) & ((lane // EXPERTS_PER_GROUP).astype(F32) == gsel)
    le = jnp.where(in_grp, logits, low)
    m1 = jnp.max(le, axis=-1, keepdims=True)
    i1 = jnp.min(jnp.where(le == m1, lanef, 1e9), axis=-1, keepdims=True)
    le2 = jnp.where(lanef == i1, low, le)
    m2 = jnp.max(le2, axis=-1, keepdims=True)
    i2 = jnp.min(jnp.where(le2 == m2, lanef, 1e9), axis=-1, keepdims=True)
    r = jnp.exp(m2 - m1)
    w1 = 1.0 / (1.0 + r)
    w2 = r / (1.0 + r)
    comb = pg_sel * (jnp.where(lanef == i1, w1, 0.0) + jnp.where(lanef == i2, w2, 0.0))
    comb_ref[...] = comb + jnp.where(lane == GROUP_ID_LANE, gsel, 0.0)


def _merge_call(x, mod, y, attn_outs, p, l, tm, tiles_per_seq):
    ntok = x.shape[0]
    tok = lambda w: pl.BlockSpec((tm, w), lambda i: (i, 0))
    lay = lambda *shape: pl.BlockSpec((None,) + shape, lambda i: (l,) + (0,) * len(shape))
    attn_specs = [pl.BlockSpec((2, tm, PAIR_WIDTH), lambda i: (0, i, 0))] * 6
    return pl.pallas_call(
        _merge_kernel,
        grid=(ntok // tm,),
        in_specs=[tok(D_MODEL), _mod_spec(mod, tm, tiles_per_seq), tok(SSM_WIDTH)] + attn_specs + [
            lay(1, D_MODEL), lay(1, D_MODEL),
            lay(D_MODEL, 2 * D_MODEL), lay(1, 2 * D_MODEL),
            lay(SSM_WIDTH, SSM_WIDTH), lay(SSM_WIDTH, D_MODEL), lay(GROUP_WIDTH, D_MODEL),
            lay(D_MODEL, D_MODEL),
            lay(D_MODEL, ROUTER_LANES), lay(D_MODEL, ROUTER_LANES), lay(1, ROUTER_LANES),
        ],
        out_specs=[tok(D_MODEL), tok(D_MODEL), tok(ROUTER_LANES)],
        out_shape=[jax.ShapeDtypeStruct((ntok, D_MODEL), F32),
                   jax.ShapeDtypeStruct((ntok, D_MODEL), BF16),
                   jax.ShapeDtypeStruct((ntok, ROUTER_LANES), F32)],
        compiler_params=_cparams(("parallel",)),
        name="merge_router",
    )(x, mod, y, *attn_outs, p['g_mix'], p['g_ffn'], p['w_mgate'], p['b_mgate'], p['w_glu'],
      p['w_br_ssm'], p['w_br_attn'], p['w_out'], p['w_router_hi'], p['w_router_lo'], p['b_router'])


MOE_EXPERT_BLOCK = 4
MOE_ROW_CHUNK = 128
GROUP_ID_LANE = 64
_TN = (((0,), (0,)), ((), ()))


def _split3(x):
    a = x.astype(BF16)
    r = x - a.astype(F32)
    b = r.astype(BF16)
    return a, b, (r - b.astype(F32)).astype(BF16)


def _moe_kernel(h2_ref, comb_ref, x1_ref, mod_ref, wg_ref, wu_ref, wd_ref, gfin_ref, out_ref,
                perm, hs, combs, ys, bounds, *, final):
    s = pl.program_id(1)
    tm = h2_ref.shape[0]
    ch = MOE_ROW_CHUNK

    @pl.when(s == 0)
    def _():
        comb = comb_ref[...]
        lane = lax.broadcasted_iota(jnp.int32, (tm, ROUTER_LANES), 1)
        gid = jnp.sum(jnp.where(lane == GROUP_ID_LANE, comb, 0.0), axis=-1, keepdims=True)
        onehot = jnp.where(lane.astype(F32) == gid, 1.0, 0.0)
        row = lax.broadcasted_iota(jnp.int32, (tm, tm), 0)
        col = lax.broadcasted_iota(jnp.int32, (tm, tm), 1)
        earlier = jnp.where(col < row, 1.0, 0.0).astype(BF16)
        rank = _dot(earlier, onehot.astype(BF16))
        tot = jnp.sum(onehot, axis=0, keepdims=True)
        lane1 = lax.broadcasted_iota(jnp.int32, (1, ROUTER_LANES), 1)
        start = jnp.zeros((1, ROUTER_LANES), F32)
        run = jnp.zeros((1, 1), F32)
        for g in range(N_EXPERT_GROUPS):
            bounds[2 * g] = run[0, 0].astype(jnp.int32)
            start = start + jnp.where(lane1 == g, run, 0.0)
            run = run + jnp.sum(jnp.where(lane1 == g, tot, 0.0), axis=-1, keepdims=True)
            bounds[2 * g + 1] = run[0, 0].astype(jnp.int32)
        pos = jnp.sum(onehot * (rank + start), axis=-1, keepdims=True)
        perm[...] = jnp.where(col.astype(F32) == pos, 1.0, 0.0).astype(BF16)
        hs[...] = lax.dot_general(perm[...], h2_ref[...], _TN, preferred_element_type=F32).astype(BF16)
        parts = lax.dot_general(perm[...], jnp.concatenate(_split3(comb), axis=-1), _TN,
                                preferred_element_type=F32)
        combs[...] = (parts[:, :ROUTER_LANES] + parts[:, ROUTER_LANES:2 * ROUTER_LANES]) + parts[:, 2 * ROUTER_LANES:]
        ys[...] = jnp.zeros_like(ys)

    g = s // (EXPERTS_PER_GROUP // MOE_EXPERT_BLOCK)
    lo = bounds[2 * g] // ch
    hi = (bounds[2 * g + 1] + (ch - 1)) // ch
    lane = lax.broadcasted_iota(jnp.int32, (ch, ROUTER_LANES), 1)

    def chunk(c, carry):
        r0 = pl.multiple_of(c * ch, ch)
        h = hs[pl.ds(r0, ch), :]
        comb = combs[pl.ds(r0, ch), :]
        hids = []
        for j in range(MOE_EXPERT_BLOCK):
            gt = _dot(h, wg_ref[j])
            w = jnp.sum(jnp.where(lane == s * MOE_EXPERT_BLOCK + j, comb, 0.0), axis=-1, keepdims=True)
            hids.append((gt * _sigmoid(gt) * _dot(h, wu_ref[j]) * w).astype(BF16))
        ys[pl.ds(r0, ch), :] += _dot(jnp.concatenate(hids, axis=-1), wd_ref[...])
        return carry

    lax.fori_loop(lo, hi, chunk, 0)

    @pl.when(s == pl.num_programs(1) - 1)
    def _():
        x2 = x1_ref[...] + mod_ref[...] * _dot(perm[...], ys[...].astype(BF16))
        out_ref[...] = _rmsnorm(x2, gfin_ref[...]) if final else x2


def _moe_call(h2, comb, x1, mod, p, l, tm, tiles_per_seq, final):
    ntok = h2.shape[0]
    eb = MOE_EXPERT_BLOCK
    tok = lambda w: pl.BlockSpec((tm, w), lambda i, s: (i, 0))
    gate_piece = 5
    if mod.ndim == 4:
        mod_spec = pl.BlockSpec((None, None, 1, D_MODEL), lambda i, s: (gate_piece, i // tiles_per_seq, 0, 0))
    else:
        mod_spec = pl.BlockSpec((None, tm, D_MODEL), lambda i, s: (gate_piece, i, 0))
    w_down = p['w_exp_down'].reshape(DEPTH, N_EXPERTS // eb, eb * EXPERT_FF, D_MODEL)
    return pl.pallas_call(
        functools.partial(_moe_kernel, final=final),
        grid=(ntok // tm, N_EXPERTS // eb),
        in_specs=[tok(D_MODEL), tok(ROUTER_LANES), tok(D_MODEL), mod_spec,
                  pl.BlockSpec((None, eb, D_MODEL, EXPERT_FF), lambda i, s: (l, s, 0, 0)),
                  pl.BlockSpec((None, eb, D_MODEL, EXPERT_FF), lambda i, s: (l, s, 0, 0)),
                  pl.BlockSpec((None, None, eb * EXPERT_FF, D_MODEL), lambda i, s: (l, s, 0, 0)),
                  pl.BlockSpec((1, D_MODEL), lambda i, s: (0, 0))],
        out_specs=tok(D_MODEL),
        out_shape=jax.ShapeDtypeStruct((ntok, D_MODEL), F32),
        scratch_shapes=[pltpu.VMEM((tm, tm), BF16), pltpu.VMEM((tm, D_MODEL), BF16),
                        pltpu.VMEM((tm, ROUTER_LANES), F32), pltpu.VMEM((tm, D_MODEL), F32),
                        pltpu.SMEM((2 * N_EXPERT_GROUPS,), jnp.int32)],
        compiler_params=_cparams(("parallel", "arbitrary")),
        name="moe_experts",
    )(h2, comb, x1, mod, p['w_exp_gate'], p['w_exp_up'], w_down, p['g_final'])


def _kv_state(qkv, n_seq, seq, prompt):
    kv = []
    for g in range(N_GROUPS):
        keep = min(WINDOWS[g], seq)
        if prompt:
            tails = _kv_tail_call(qkv[g], n_seq, seq, keep).reshape(2, n_seq, HEADS, HEAD_DIM, keep)
            kv += [tails[0].transpose(0, 3, 1, 2), tails[1].transpose(0, 3, 1, 2)]
        else:
            for a in (1, 2):
                new = qkv[g][a].reshape(2, n_seq, seq, 2, HEAD_DIM)
                kv.append(new.transpose(1, 2, 0, 3, 4).reshape(n_seq, seq, HEADS, HEAD_DIM))
    return kv


def _run_trunks(xp, xs, mod_p, mod_s, p, ssm, batch, seq, dec_batch, dec_seq, caches, h0):
    tm_p, tm_s = 512, 256
    tm_moe_p, tm_moe_s = 1024, 512
    wide = N_GROUPS - 1
    states_p, states_s = [], []
    for l in range(DEPTH):
        ab, bre, bim, cre, cim, d_skip = ssm[l]
        up, *qkv_p = _inproj_call(xp, mod_p[l], p['g_mix'], p['w_in'], l, tm_p, seq // tm_p)
        us, *qkv_s = _inproj_call(xs, mod_s[l], p['g_mix'], p['w_in'], l, tm_s, 1)

        (yp, hre_p, him_p), attn_wide = _scan_attn_call(
            up, ab, bre, bim, cre, cim, d_skip, batch, seq,
            qkv_s[wide], caches[2 * wide], caches[2 * wide + 1], l, dec_batch, dec_seq, DILATIONS[wide])
        attn_p, attn_s = [], []
        for g in range(N_GROUPS - 1):
            o_p, o_s = _attn_prompt_call(qkv_p[g], batch, seq, DILATIONS[g],
                                         sample=(qkv_s[g], caches[2 * g], caches[2 * g + 1], l, dec_batch, dec_seq))
            attn_p += list(o_p)
            attn_s += list(o_s)
        attn_p += list(_attn_prompt_call(qkv_p[wide], batch, seq, DILATIONS[wide]))
        attn_s += list(attn_wide)
        x1, h2, comb = _merge_call(xp, mod_p[l], yp, attn_p, p, l, tm_p, seq // tm_p)
        xp = _moe_call(h2, comb, x1, mod_p[l], p, l, tm_moe_p, seq // tm_moe_p, final=(l == DEPTH - 1))
        states_p.append(_kv_state(qkv_p, batch, seq, True)
                        + [hre_p[:, 0].reshape(batch, N_SSM_GROUPS, SSM_STATE),
                           him_p[:, 0].reshape(batch, N_SSM_GROUPS, SSM_STATE)])

        ys, hre_s, him_s = _ssm_sample_call(us.reshape(dec_batch, dec_seq * SSM_WIDTH), ab, bre, bim, cre, cim,
                                            d_skip, h0[0][l], h0[1][l], dec_seq)
        x1, h2, comb = _merge_call(xs, mod_s[l], ys.reshape(dec_batch * dec_seq, SSM_WIDTH), attn_s, p, l, tm_s, 1)
        xs = _moe_call(h2, comb, x1, mod_s[l], p, l, tm_moe_s, 1, final=(l == DEPTH - 1))
        states_s.append(_kv_state(qkv_s, dec_batch, dec_seq, False)
                        + [hre_s.reshape(dec_batch, N_SSM_GROUPS, SSM_STATE),
                           him_s.reshape(dec_batch, N_SSM_GROUPS, SSM_STATE)])
    stack = lambda states: [jnp.stack([states[l][i] for l in range(DEPTH)], axis=0)
                            for i in range(len(states[0]))]
    return xp, xs, stack(states_p), stack(states_s)


def kernel(x_prompt, x_sample, cache_k_w128, cache_v_w128, cache_k_w512, cache_v_w512,
           cache_k_w2048, cache_v_w2048, state_ssm_re, state_ssm_im, c_prompt, c_sample,
           w_ada, b_ada, g_norm_mix, g_norm_ffn, g_final, w_in, ssm_a_re, ssm_a_im, ssm_b_re,
           ssm_b_im, ssm_c_re, ssm_c_im, ssm_log_step, ssm_d, w_glu, w_br_ssm, w_br_attn,
           w_mgate, b_mgate, w_out, w_router_grp, b_router_grp, w_router_exp, b_router_exp,
           w_exp_gate, w_exp_up, w_exp_down):
    batch, seq, _ = x_prompt.shape
    dec_batch, dec_seq, _ = x_sample.shape

    pad = -batch % SUBLANES
    c_p = jnp.concatenate([c_prompt, jnp.zeros((pad, D_MODEL), F32)], axis=0)
    mod_p, mod_s = _ada_call(c_p, jnp.repeat(c_sample, dec_seq, axis=0), w_ada, b_ada)
    mod_prompt = [mod_p[l, :, :batch].reshape(6, batch, 1, D_MODEL) for l in range(DEPTH)]
    mod_sample = [mod_s[l] for l in range(DEPTH)]

    abr, abi, bbr, bbi = _ssm_param_call(ssm_a_re, ssm_a_im, ssm_log_step, ssm_b_re, ssm_b_im)
    ssm = []
    for l in range(DEPTH):
        pick = lambda a: a[l].reshape(N_SSM_GROUPS, SSM_STATE, SSM_GROUP_CH)
        ab = jnp.stack([pick(abr)[:, :, 0].reshape(SSM_LANES), pick(abi)[:, :, 0].reshape(SSM_LANES)])
        bre = _block_diag_halves(pick(bbr).transpose(0, 2, 1), BF16)
        bim = _block_diag_halves(pick(bbi).transpose(0, 2, 1), BF16)
        cre = _block_diag_halves(ssm_c_re[l].transpose(0, 2, 1), BF16)
        cim = _block_diag_halves(ssm_c_im[l].transpose(0, 2, 1), BF16)
        ssm.append((ab, bre, bim, cre, cim, ssm_d[l].reshape(1, SSM_WIDTH)))

    w_router = jnp.concatenate(
        [w_router_exp, w_router_grp,
         jnp.zeros((DEPTH, D_MODEL, ROUTER_LANES - N_EXPERTS - N_EXPERT_GROUPS), F32)], axis=-1)
    b_router = jnp.concatenate(
        [b_router_exp, b_router_grp,
         jnp.zeros((DEPTH, ROUTER_LANES - N_EXPERTS - N_EXPERT_GROUPS), F32)], axis=-1)
    w_router_hi = w_router.astype(BF16)
    p = dict(
        g_mix=g_norm_mix.reshape(DEPTH, 1, D_MODEL), g_ffn=g_norm_ffn.reshape(DEPTH, 1, D_MODEL),
        g_final=g_final.reshape(1, D_MODEL), w_in=w_in.astype(BF16), w_glu=w_glu.astype(BF16),
        w_br_ssm=w_br_ssm.astype(BF16), w_br_attn=w_br_attn.astype(BF16),
        w_mgate=w_mgate.astype(BF16), b_mgate=b_mgate.reshape(DEPTH, 1, 2 * D_MODEL),
        w_out=w_out.astype(BF16), w_router_hi=w_router_hi,
        w_router_lo=(w_router - w_router_hi.astype(F32)).astype(BF16),
        b_router=b_router.reshape(DEPTH, 1, ROUTER_LANES),
        w_exp_gate=w_exp_gate.astype(BF16), w_exp_up=w_exp_up.astype(BF16),
        w_exp_down=w_exp_down.astype(BF16))

    caches = [cache_k_w128, cache_v_w128, cache_k_w512, cache_v_w512, cache_k_w2048, cache_v_w2048]
    h0 = (state_ssm_re.reshape(DEPTH, dec_batch, SSM_LANES), state_ssm_im.reshape(DEPTH, dec_batch, SSM_LANES))
    y_prompt, y_sample, pstate, sstate = _run_trunks(
        x_prompt.reshape(batch * seq, D_MODEL), x_sample.reshape(dec_batch * dec_seq, D_MODEL),
        mod_prompt, mod_sample, p, ssm, batch, seq, dec_batch, dec_seq, caches, h0)
    return (y_prompt.reshape(batch, seq, D_MODEL), y_sample.reshape(dec_batch, dec_seq, D_MODEL),
            *pstate, *sstate)
```

```python
import functools
import math

import jax
import jax.numpy as jnp
from jax import lax
from jax.experimental import pallas as pl
from jax.experimental.pallas import tpu as pltpu

F32 = jnp.float32
BF16 = jnp.bfloat16

D_MODEL = 1024
DEPTH = 2
SSM_WIDTH = 512
SSM_GROUP_CH = 16
N_SSM_GROUPS = 32
SSM_STATE = 64
SSM_LANES = N_SSM_GROUPS * SSM_STATE
HEAD_DIM = 64
HEADS = 4
GROUP_WIDTH = HEADS * HEAD_DIM
PAIR_WIDTH = 2 * HEAD_DIM
WINDOWS = (128, 512, 2048)
DILATIONS = (1, 4, 16)
N_BACK = 128
N_GROUPS = 3
QKV_WIDTH = N_GROUPS * GROUP_WIDTH
IN_PROJ_WIDTH = SSM_WIDTH + 3 * QKV_WIDTH
N_EXPERT_GROUPS = 4
EXPERTS_PER_GROUP = 8
N_EXPERTS = 32
EXPERT_FF = 256
RMS_EPS = 1e-6
NEG_INF = -1e30
ROUTER_LANES = 128
GROUP_LOGIT_LANE0 = N_EXPERTS

SUBLANES = 8
VMEM_LIMIT = 56 * 1024 * 1024


def _cparams(sem):
    return pltpu.CompilerParams(dimension_semantics=sem, vmem_limit_bytes=VMEM_LIMIT)


def _sigmoid(x):
    return 1.0 / (1.0 + jnp.exp(-x))


def _gelu_tanh(x):
    return 0.5 * x * (1.0 + jnp.tanh(math.sqrt(2.0 / math.pi) * (x + 0.044715 * (x * x * x))))


def _dot(a, b):
    return jnp.dot(a, b, preferred_element_type=F32)


def _rmsnorm(x, g):
    ms = jnp.mean(x * x, axis=-1, keepdims=True)
    return x * lax.rsqrt(ms + RMS_EPS) * g


def _ada_kernel(cp_ref, cs_ref, w_ref, b_ref, op_ref, os_ref):
    w = w_ref[...].astype(BF16)
    for c_ref, o_ref in ((cp_ref, op_ref), (cs_ref, os_ref)):
        c = c_ref[...]
        o_ref[...] = _dot((c * _sigmoid(c)).astype(BF16), w) + b_ref[...]


def _ada_call(c_prompt_rows, c_sample_rows, w_ada, b_ada):
    rp, rs = c_prompt_rows.shape[0], c_sample_rows.shape[0]
    out = lambda r: pl.BlockSpec((None, None, r, D_MODEL), lambda l, j: (l, j, 0, 0))
    return pl.pallas_call(
        _ada_kernel,
        grid=(DEPTH, 6),
        in_specs=[
            pl.BlockSpec((rp, D_MODEL), lambda l, j: (0, 0)),
            pl.BlockSpec((rs, D_MODEL), lambda l, j: (0, 0)),
            pl.BlockSpec((None, D_MODEL, D_MODEL), lambda l, j: (l, 0, j)),
            pl.BlockSpec((None, 1, D_MODEL), lambda l, j: (l, 0, j)),
        ],
        out_specs=[out(rp), out(rs)],
        out_shape=[jax.ShapeDtypeStruct((DEPTH, 6, rp, D_MODEL), F32),
                   jax.ShapeDtypeStruct((DEPTH, 6, rs, D_MODEL), F32)],
        compiler_params=_cparams(("parallel", "parallel")),
        name="ada_mod",
    )(c_prompt_rows, c_sample_rows, w_ada, b_ada.reshape(DEPTH, 1, 6 * D_MODEL))


def _ssm_param_kernel(ar_ref, ai_ref, ls_ref, br_ref, bi_ref, abr_ref, abi_ref, bbr_ref, bbi_ref):
    lr, li = ar_ref[...], ai_ref[...]
    dt = jnp.exp(ls_ref[...])
    mag = jnp.exp(lr * dt)
    abr = mag * jnp.cos(li * dt)
    abi = mag * jnp.sin(li * dt)
    den = lr * lr + li * li
    er, ei = abr - 1.0, abi
    fr = (er * lr + ei * li) / den
    fi = (ei * lr - er * li) / den
    br, bi = br_ref[...], bi_ref[...]
    abr_ref[...] = abr
    abi_ref[...] = abi
    bbr_ref[...] = fr * br - fi * bi
    bbi_ref[...] = fr * bi + fi * br


def _ssm_param_call(a_re, a_im, log_step, b_re, b_im):
    w = SSM_STATE * SSM_GROUP_CH
    rep = lambda a: jnp.repeat(a, SSM_GROUP_CH, axis=-1)
    big = pl.BlockSpec((None, N_SSM_GROUPS, w), lambda l: (l, 0, 0))
    shp = jax.ShapeDtypeStruct((DEPTH, N_SSM_GROUPS, w), F32)
    return pl.pallas_call(
        _ssm_param_kernel,
        grid=(DEPTH,),
        in_specs=[big, big, pl.BlockSpec((None, N_SSM_GROUPS, 1), lambda l: (l, 0, 0)), big, big],
        out_specs=[big, big, big, big],
        out_shape=[shp, shp, shp, shp],
        compiler_params=_cparams(("parallel",)),
        name="ssm_params",
    )(rep(a_re), rep(a_im), log_step.reshape(DEPTH, N_SSM_GROUPS, 1),
      b_re.reshape(DEPTH, N_SSM_GROUPS, w), b_im.reshape(DEPTH, N_SSM_GROUPS, w))


def _block_diag_halves(blocks, dtype):
    g, r, c = blocks.shape
    half = g // 2
    eye = jnp.eye(half, dtype=bool)

    def one(b):
        m = jnp.where(eye[:, None, :, None], b[:, :, None, :], 0.0)
        return m.reshape(half * r, half * c)

    return jnp.stack([one(blocks[:half]), one(blocks[half:])]).astype(dtype)


def _inproj_kernel(x_ref, mod_ref, g_ref, w_ref, u_ref, a0_ref, a1_ref, a2_ref):
    h = _rmsnorm(x_ref[...], g_ref[...]) * (1.0 + mod_ref[1]) + mod_ref[0]
    hb = h.astype(BF16)
    u_ref[...] = _dot(hb, w_ref[:, :SSM_WIDTH])
    for g, ref in enumerate((a0_ref, a1_ref, a2_ref)):
        for a in range(3):
            c0 = SSM_WIDTH + a * QKV_WIDTH + g * GROUP_WIDTH
            val = _dot(hb, w_ref[:, c0:c0 + GROUP_WIDTH])
            ref[a, 0] = val[:, :PAIR_WIDTH]
            ref[a, 1] = val[:, PAIR_WIDTH:]


def _mod_spec(mod, tm, tiles_per_seq):
    if mod.ndim == 4:
        return pl.BlockSpec((6, None, 1, D_MODEL), lambda i: (0, i // tiles_per_seq, 0, 0))
    return pl.BlockSpec((6, tm, D_MODEL), lambda i: (0, i, 0))


def _inproj_call(x, mod, g_mix, w_in, l, tm, tiles_per_seq):
    ntok = x.shape[0]
    grp = jax.ShapeDtypeStruct((3, 2, ntok, PAIR_WIDTH), F32)
    grp_spec = pl.BlockSpec((3, 2, tm, PAIR_WIDTH), lambda i: (0, 0, i, 0))
    return pl.pallas_call(
        _inproj_kernel,
        grid=(ntok // tm,),
        in_specs=[
            pl.BlockSpec((tm, D_MODEL), lambda i: (i, 0)),
            _mod_spec(mod, tm, tiles_per_seq),
            pl.BlockSpec((None, 1, D_MODEL), lambda i: (l, 0, 0)),
            pl.BlockSpec((None, D_MODEL, IN_PROJ_WIDTH), lambda i: (l, 0, 0)),
        ],
        out_specs=[pl.BlockSpec((tm, SSM_WIDTH), lambda i: (i, 0)), grp_spec, grp_spec, grp_spec],
        out_shape=[jax.ShapeDtypeStruct((ntok, SSM_WIDTH), F32), grp, grp, grp],
        compiler_params=_cparams(("parallel",)),
        name="in_proj",
    )(x, mod, g_mix, w_in)


LANE_TILE = 128
SCAN_CHUNK = 256
SEG_LEN = SCAN_CHUNK // SUBLANES
SCAN_TILE_GROUP = 4


def _cmul(ar, ai, br, bi):
    return ar * br - ai * bi, ar * bi + ai * br


def _ssm_prompt_kernel(u_ref, pm_ref, pmt_ref, ab_ref, bre_ref, bim_ref, cre_ref, cim_ref, d_ref,
                       y_ref, hre_ref, him_ref, bur, bui, tab, carry):
    c = pl.program_id(1)
    half = SSM_LANES // 2
    tiles_per_half = half // LANE_TILE

    @pl.when(c == 0)
    def _():
        shp = (SUBLANES, SSM_LANES)
        sub = lax.broadcasted_iota(jnp.int32, shp, 0)
        b_r = jnp.broadcast_to(ab_ref[0:1, :], shp)
        b_i = jnp.broadcast_to(ab_ref[1:2, :], shp)
        for _ in range(SEG_LEN.bit_length() - 1):
            b_r, b_i = _cmul(b_r, b_i, b_r, b_i)
        p_r, p_i = b_r, b_i
        ps_r, ps_i = jnp.ones(shp, F32), jnp.zeros(shp, F32)
        for s in range(1, SUBLANES):
            if s in (1, 2, 4):
                k = {1: 0, 2: 2, 4: 4}[s]
                tab[k] = jnp.where(sub >= s, p_r, 0.0)
                tab[k + 1] = jnp.where(sub >= s, p_i, 0.0)
            ps_r = jnp.where(sub >= s, p_r, ps_r)
            ps_i = jnp.where(sub >= s, p_i, ps_i)
            p_r, p_i = _cmul(p_r, p_i, b_r, b_i)
        tab[6] = ps_r
        tab[7] = ps_i
        carry[...] = jnp.zeros_like(carry)

    u = u_ref[...]
    ub = _dot(pm_ref[...], u.astype(BF16)).astype(BF16)
    for h in range(2):
        uh = ub[:, h * 256:(h + 1) * 256]
        res_r = _dot(uh, bre_ref[h])
        res_i = _dot(uh, bim_ref[h])
        for j in range(tiles_per_half):
            bur[h * tiles_per_half + j] = res_r[:, j * LANE_TILE:(j + 1) * LANE_TILE]
            bui[h * tiles_per_half + j] = res_i[:, j * LANE_TILE:(j + 1) * LANE_TILE]

    vshape = (SUBLANES, LANE_TILE)
    sub1 = lax.broadcasted_iota(jnp.int32, vshape, 0) >= 1
    for g0 in range(0, SSM_LANES // LANE_TILE, SCAN_TILE_GROUP):
        tiles = list(range(g0, g0 + SCAN_TILE_GROUP))
        lanes = [slice(lt * LANE_TILE, (lt + 1) * LANE_TILE) for lt in tiles]
        a_r = [jnp.broadcast_to(ab_ref[0:1, sl], vshape) for sl in lanes]
        a_i = [jnp.broadcast_to(ab_ref[1:2, sl], vshape) for sl in lanes]

        def step(t, hs, store):
            out = []
            for j, lt in enumerate(tiles):
                rows = pl.ds(pl.multiple_of(t * SUBLANES, SUBLANES), SUBLANES)
                hr, hi = hs[2 * j], hs[2 * j + 1]
                nr = (a_r[j] * hr - a_i[j] * hi) + bur[lt, rows, :]
                ni = (a_r[j] * hi + a_i[j] * hr) + bui[lt, rows, :]
                if store:
                    bur[lt, rows, :] = nr
                    bui[lt, rows, :] = ni
                out += [nr, ni]
            return tuple(out)

        zeros = tuple(jnp.zeros(vshape, F32) for _ in range(2 * SCAN_TILE_GROUP))
        ends = lax.fori_loop(0, SEG_LEN, lambda t, hs: step(t, hs, False), zeros, unroll=True)

        starts = []
        for j, sl in enumerate(lanes):
            fr, fi = ends[2 * j], ends[2 * j + 1]
            for k in (1, 2, 4):
                ti = {1: 0, 2: 2, 4: 4}[k]
                mr, mi = tab[ti, :, sl], tab[ti + 1, :, sl]
                rr = pltpu.roll(fr, k, axis=0)
                ri = pltpu.roll(fi, k, axis=0)
                fr, fi = fr + (mr * rr - mi * ri), fi + (mr * ri + mi * rr)
            cr, ci = carry[0, :, sl], carry[1, :, sl]
            ps_r, ps_i = tab[6, :, sl], tab[7, :, sl]
            sr = (ps_r * cr - ps_i * ci) + jnp.where(sub1, pltpu.roll(fr, 1, axis=0), 0.0)
            si = (ps_r * ci + ps_i * cr) + jnp.where(sub1, pltpu.roll(fi, 1, axis=0), 0.0)
            starts += [sr, si]

        last = lax.fori_loop(0, SEG_LEN, lambda t, hs: step(t, hs, True), tuple(starts), unroll=True)
        for j, sl in enumerate(lanes):
            carry[0, :, sl] = jnp.broadcast_to(last[2 * j][SUBLANES - 1:SUBLANES, :], vshape)
            carry[1, :, sl] = jnp.broadcast_to(last[2 * j + 1][SUBLANES - 1:SUBLANES, :], vshape)

    for h in range(2):
        cs = slice(h * 256, (h + 1) * 256)
        tl = range(h * tiles_per_half, (h + 1) * tiles_per_half)
        hr = jnp.concatenate([bur[lt] for lt in tl], axis=-1).astype(BF16)
        hi = jnp.concatenate([bui[lt] for lt in tl], axis=-1).astype(BF16)
        y = _dot(hr, cre_ref[h]) - _dot(hi, cim_ref[h])
        y_hi = y.astype(BF16)
        y_lo = (y - y_hi.astype(F32)).astype(BF16)
        y = _dot(pmt_ref[...], y_hi) + _dot(pmt_ref[...], y_lo)
        y_ref[:, cs] = y + d_ref[:, cs] * u[:, cs]

    @pl.when(c == pl.num_programs(1) - 1)
    def _():
        hre_ref[...] = carry[0]
        him_ref[...] = carry[1]


def _ssm_sample_kernel(u_ref, ab_ref, bre_ref, bim_ref, cre_ref, cim_ref, d_ref, h0r_ref, h0i_ref,
                       y_ref, hr_ref, hi_ref, *, steps):
    half = SSM_LANES // 2
    ar, ai = ab_ref[0:1, :], ab_ref[1:2, :]
    hr_ref[...] = h0r_ref[...]
    hi_ref[...] = h0i_ref[...]
    for t in range(steps):
        u = u_ref[:, t * SSM_WIDTH:(t + 1) * SSM_WIDTH]
        ub = u.astype(BF16)
        for h in range(2):
            hs = slice(h * half, (h + 1) * half)
            cs = slice(h * 256, (h + 1) * 256)
            uh = ub[:, cs]
            hr, hi = hr_ref[:, hs], hi_ref[:, hs]
            a_r, a_i = ar[:, hs], ai[:, hs]
            nr = (a_r * hr - a_i * hi) + _dot(uh, bre_ref[h])
            ni = (a_r * hi + a_i * hr) + _dot(uh, bim_ref[h])
            hr_ref[:, hs] = nr
            hi_ref[:, hs] = ni
            y = _dot(nr.astype(BF16), cre_ref[h]) - _dot(ni.astype(BF16), cim_ref[h])
            y_ref[:, t * SSM_WIDTH + h * 256:t * SSM_WIDTH + (h + 1) * 256] = y + d_ref[:, cs] * u[:, cs]


def _ssm_sample_call(u, ab, bre, bim, cre, cim, d_skip, h0r, h0i, steps):
    n = u.shape[0]
    st = jax.ShapeDtypeStruct((n, SSM_LANES), F32)
    return pl.pallas_call(
        functools.partial(_ssm_sample_kernel, steps=steps),
        out_shape=[jax.ShapeDtypeStruct((n, steps * SSM_WIDTH), F32), st, st],
        compiler_params=pltpu.CompilerParams(vmem_limit_bytes=VMEM_LIMIT),
        name="ssm_sample",
    )(u, ab, bre, bim, cre, cim, d_skip, h0r, h0i)


_NT = (((1,), (1,)), ((), ()))


def _attn_prompt_kernel(q_ref, kp_ref, kc_ref, vp_ref, vc_ref, o_ref, lse_ref, *, dil, q_blocks):
    b = pl.program_id(2)
    qi = lax.broadcasted_iota(jnp.int32, (N_BACK, 2 * N_BACK), 0)
    kj = lax.broadcasted_iota(jnp.int32, (N_BACK, 2 * N_BACK), 1)
    band = (kj >= qi) & (kj <= qi + N_BACK)
    band_first = band & (kj >= jnp.where(b > 0, 0, N_BACK))
    lane = lax.broadcasted_iota(jnp.int32, (N_BACK, PAIR_WIDTH), 1)
    head0 = lane < HEAD_DIM

    def rows(j, r):
        start = j * N_BACK * dil + r
        return pl.ds(start, N_BACK, stride=dil) if dil > 1 else pl.ds(start, N_BACK)

    def one(j, r):
        cur = rows(j, r)
        q = (q_ref[cur, :] * (HEAD_DIM ** -0.5)).astype(BF16)
        k_prev = kp_ref[rows(0, r), :] if j == 0 else kc_ref[rows(j - 1, r), :]
        v_prev = vp_ref[rows(0, r), :] if j == 0 else vc_ref[rows(j - 1, r), :]
        kk = jnp.concatenate([k_prev, kc_ref[cur, :]], axis=0).astype(BF16)
        vv = jnp.concatenate([v_prev, vc_ref[cur, :]], axis=0).astype(BF16)
        valid = band_first if j == 0 else band
        outs, lses = [], []
        for sub in range(2):
            in_head = head0 if sub == 0 else jnp.logical_not(head0)
            qm = jnp.where(in_head, q, jnp.zeros_like(q))
            s = lax.dot_general(qm, kk, _NT, preferred_element_type=F32)
            s = jnp.where(valid, s, NEG_INF)
            m = jnp.max(s, axis=-1, keepdims=True)
            p = jnp.exp(s - m)
            den = jnp.sum(p, axis=-1, keepdims=True)
            outs.append(_dot(p.astype(BF16), vv) / den)
            lses.append(jnp.broadcast_to(m + jnp.log(den), (N_BACK, PAIR_WIDTH)))
        o_ref[cur, :] = jnp.where(head0, outs[0], outs[1])
        lse_ref[cur, :] = jnp.where(head0, lses[0], lses[1])

    def residue(r, carry):
        for j in range(q_blocks):
            one(j, r)
        return carry

    if dil <= ATTN_UNROLL:
        for r in range(dil):
            residue(r, 0)
    else:
        lax.fori_loop(0, dil, residue, 0, unroll=ATTN_UNROLL)


ATTN_UNROLL = 4
ATTN_Q_BLOCKS = {1: 8, 4: 2, 16: 1}


def _attn_pair_kernel(q_ref, kp_ref, kc_ref, vp_ref, vc_ref, qkv_s_ref, kt_ref, vt_ref,
                      o_ref, lse_ref, o_s_ref, lse_s_ref, *, dil, q_blocks, steps):
    _attn_sample_kernel(qkv_s_ref, kt_ref, vt_ref, o_s_ref, lse_s_ref, dil=dil, steps=steps)
    _attn_prompt_kernel(q_ref, kp_ref, kc_ref, vp_ref, vc_ref, o_ref, lse_ref, dil=dil, q_blocks=q_blocks)


def _attn_prompt_call(qkv, n_seq, seq, dil, sample=None):
    q_blocks = ATTN_Q_BLOCKS[dil]
    prev_rows = N_BACK * dil
    rows = prev_rows * q_blocks
    nbk = seq // rows
    cur = lambda a: pl.BlockSpec((None, None, rows, PAIR_WIDTH), lambda n, hp, b: (a, hp, n * nbk + b, 0))
    prev = lambda a: pl.BlockSpec(
        (None, None, prev_rows, PAIR_WIDTH),
        lambda n, hp, b: (a, hp, jnp.maximum((n * nbk + b) * q_blocks - 1, n * nbk * q_blocks), 0))
    out_spec = pl.BlockSpec((None, rows, PAIR_WIDTH), lambda n, hp, b: (hp, n * nbk + b, 0))
    out_shape = jax.ShapeDtypeStruct((2, n_seq * seq, PAIR_WIDTH), F32)
    in_specs = [cur(0), prev(1), cur(1), prev(2), cur(2)]
    operands = [qkv, qkv, qkv, qkv, qkv]
    out_specs, out_shapes = [out_spec, out_spec], [out_shape, out_shape]
    if sample is None:
        body = functools.partial(_attn_prompt_kernel, dil=dil, q_blocks=q_blocks)
    else:
        qkv_s, k_cache, v_cache, l, n_dec, steps = sample
        w = k_cache.shape[2]
        nb = SAMPLE_SEQ_BLOCK[w]
        assert n_seq * 2 * nbk * nb == n_dec, "sample blocks must match the prompt attention grid"
        step = lambda n, hp, b: (n * 2 + hp) * nbk + b
        cache_spec = pl.BlockSpec((None, nb, HEADS, HEAD_DIM, w), lambda n, hp, b: (l, step(n, hp, b), 0, 0, 0))
        s_out = pl.BlockSpec((2, nb * steps, PAIR_WIDTH), lambda n, hp, b: (0, step(n, hp, b), 0))
        s_shape = jax.ShapeDtypeStruct((2, n_dec * steps, PAIR_WIDTH), F32)
        in_specs += [pl.BlockSpec((3, 2, nb * steps, PAIR_WIDTH), lambda n, hp, b: (0, 0, step(n, hp, b), 0)),
                     cache_spec, cache_spec]
        operands += [qkv_s, k_cache.transpose(0, 1, 3, 4, 2), v_cache.transpose(0, 1, 3, 4, 2)]
        out_specs += [s_out, s_out]
        out_shapes += [s_shape, s_shape]
        body = functools.partial(_attn_pair_kernel, dil=dil, q_blocks=q_blocks, steps=steps)
    outs = pl.pallas_call(
        body,
        grid=(n_seq, 2, nbk),
        in_specs=in_specs,
        out_specs=out_specs,
        out_shape=out_shapes,
        compiler_params=_cparams(("parallel", "parallel", "parallel")),
        name=f"attn_prompt_d{dil}",
    )(*operands)
    return outs if sample is None else (outs[:2], outs[2:])


def _kv_tail_kernel(x_ref, o_ref):
    o_ref[...] = x_ref[...].T


def _kv_tail_call(qkv, n_seq, seq, keep):
    blocks_per_seq = seq // keep
    return pl.pallas_call(
        _kv_tail_kernel,
        grid=(2, n_seq, 2),
        in_specs=[pl.BlockSpec((None, None, keep, PAIR_WIDTH),
                               lambda a, n, hp: (a + 1, hp, (n + 1) * blocks_per_seq - 1, 0))],
        out_specs=pl.BlockSpec((None, None, None, PAIR_WIDTH, keep), lambda a, n, hp: (a, n, hp, 0, 0)),
        out_shape=jax.ShapeDtypeStruct((2, n_seq, 2, PAIR_WIDTH, keep), F32),
        compiler_params=_cparams(("parallel", "parallel", "parallel")),
        name="kv_tail",
    )(qkv)


SAMPLE_SEQ_BLOCK = {128: 4, 512: 4, 2048: 2}


def _attn_sample_kernel(qkv_ref, kt_ref, vt_ref, o_ref, lse_ref, *, dil, steps):
    nb, _, _, w = kt_ref.shape
    rows = nb * steps
    tok = lax.broadcasted_iota(jnp.int32, (2 * rows, w), 0) % rows
    pos = lax.broadcasted_iota(jnp.int32, (2 * rows, w), 1)
    tokn = lax.broadcasted_iota(jnp.int32, (2 * rows, rows), 0) % rows
    coln = lax.broadcasted_iota(jnp.int32, (2 * rows, rows), 1)
    rowo = lax.broadcasted_iota(jnp.int32, (rows, PAIR_WIDTH), 0)
    head0 = lax.broadcasted_iota(jnp.int32, (rows, PAIR_WIDTH), 1) < HEAD_DIM

    vns, s_buf, s_new = [], {}, []
    for hp in range(2):
        q2 = qkv_ref[0, hp] * (HEAD_DIM ** -0.5)
        q_st = jnp.concatenate([jnp.where(head0, q2, 0.0), jnp.where(head0, 0.0, q2)], axis=0).astype(BF16)
        kn = qkv_ref[1, hp].astype(BF16)
        s_new.append(lax.dot_general(q_st, kn, _NT, preferred_element_type=F32))
        vns.append(qkv_ref[2, hp].astype(BF16))
        for n in range(nb):
            kt = kt_ref[n, 2 * hp:2 * hp + 2].reshape(PAIR_WIDTH, w).astype(BF16)
            s_buf[hp, n] = _dot(q_st, kt)

    probs = {}
    for hp in range(2):
        for n in range(nb):
            t = tok - n * steps
            in_buf = (((pos - t) & (dil - 1)) == 0) & (pos >= t)
            tn = tokn - n * steps
            sn = coln - n * steps
            in_new = ((sn >= 0) & (sn <= tn)) if dil == 1 else (sn == tn)
            s = jnp.where(in_buf, s_buf[hp, n], NEG_INF)
            s2 = jnp.where(in_new, s_new[hp], NEG_INF)
            m = jnp.maximum(jnp.max(s, axis=-1, keepdims=True), jnp.max(s2, axis=-1, keepdims=True))
            p = jnp.exp(s - m)
            p2 = jnp.exp(s2 - m)
            den = jnp.sum(p, axis=-1, keepdims=True) + jnp.sum(p2, axis=-1, keepdims=True)
            probs[hp, n] = (p.astype(BF16), p2.astype(BF16), den, m + jnp.log(den))

    for hp in range(2):
        o_pair = jnp.zeros((rows, PAIR_WIDTH), F32)
        lse_pair = jnp.zeros((rows, PAIR_WIDTH), F32)
        for n in range(nb):
            p, p2, den, lse = probs[hp, n]
            vt = vt_ref[n, 2 * hp:2 * hp + 2].reshape(PAIR_WIDTH, w).astype(BF16)
            o = (lax.dot_general(p, vt, _NT, preferred_element_type=F32) + _dot(p2, vns[hp])) / den
            o = jnp.where(head0, o[:rows], o[rows:])
            lse = jnp.where(head0, jnp.broadcast_to(lse[:rows], (rows, PAIR_WIDTH)),
                            jnp.broadcast_to(lse[rows:], (rows, PAIR_WIDTH)))
            mine = (rowo >= n * steps) & (rowo < (n + 1) * steps)
            o_pair = jnp.where(mine, o, o_pair)
            lse_pair = jnp.where(mine, lse, lse_pair)
        o_ref[hp] = o_pair
        lse_ref[hp] = lse_pair


def _scan_attn_kernel(u_ref, pm_ref, pmt_ref, ab_ref, bre_ref, bim_ref, cre_ref, cim_ref, d_ref,
                      qkv_ref, kt_ref, vt_ref, y_ref, hre_ref, him_ref, o_ref, lse_ref,
                      bur, bui, tab, carry, *, dil, steps):
    _attn_sample_kernel(qkv_ref, kt_ref, vt_ref, o_ref, lse_ref, dil=dil, steps=steps)
    _ssm_prompt_kernel(u_ref, pm_ref, pmt_ref, ab_ref, bre_ref, bim_ref, cre_ref, cim_ref, d_ref,
                       y_ref, hre_ref, him_ref, bur, bui, tab, carry)


def _scan_attn_call(u, ab, bre, bim, cre, cim, d_skip, n_seq, seq,
                    qkv_s, k_cache, v_cache, l, n_dec, steps, dil):
    chunks = seq // SCAN_CHUNK
    w = k_cache.shape[2]
    nb = SAMPLE_SEQ_BLOCK[w]
    rows = nb * steps
    assert n_seq * chunks * nb == n_dec, "sample blocks must match the scan grid"
    kt = k_cache.transpose(0, 1, 3, 4, 2)
    vt = v_cache.transpose(0, 1, 3, 4, 2)
    half = SSM_LANES // 2
    wspec_b = pl.BlockSpec((2, 256, half), lambda n, c: (0, 0, 0))
    wspec_c = pl.BlockSpec((2, half, 256), lambda n, c: (0, 0, 0))
    st_spec = pl.BlockSpec((None, SUBLANES, SSM_LANES), lambda n, c: (n, 0, 0))
    st_shape = jax.ShapeDtypeStruct((n_seq, SUBLANES, SSM_LANES), F32)
    r = jnp.arange(SCAN_CHUNK)
    pm = (r[None, :] == ((r % SUBLANES) * SEG_LEN + r // SUBLANES)[:, None]).astype(BF16)
    perm_spec = pl.BlockSpec((SCAN_CHUNK, SCAN_CHUNK), lambda n, c: (0, 0))
    cache_spec = pl.BlockSpec((None, nb, HEADS, HEAD_DIM, w), lambda n, c: (l, n * chunks + c, 0, 0, 0))
    attn_out = pl.BlockSpec((2, rows, PAIR_WIDTH), lambda n, c: (0, n * chunks + c, 0))
    attn_shape = jax.ShapeDtypeStruct((2, n_dec * steps, PAIR_WIDTH), F32)
    y, hre, him, o, lse = pl.pallas_call(
        functools.partial(_scan_attn_kernel, dil=dil, steps=steps),
        grid=(n_seq, chunks),
        in_specs=[
            pl.BlockSpec((SCAN_CHUNK, SSM_WIDTH), lambda n, c: (n * chunks + c, 0)),
            perm_spec, perm_spec,
            pl.BlockSpec((2, SSM_LANES), lambda n, c: (0, 0)),
            wspec_b, wspec_b, wspec_c, wspec_c,
            pl.BlockSpec((1, SSM_WIDTH), lambda n, c: (0, 0)),
            pl.BlockSpec((3, 2, rows, PAIR_WIDTH), lambda n, c: (0, 0, n * chunks + c, 0)),
            cache_spec, cache_spec,
        ],
        out_specs=[pl.BlockSpec((SCAN_CHUNK, SSM_WIDTH), lambda n, c: (n * chunks + c, 0)),
                   st_spec, st_spec, attn_out, attn_out],
        out_shape=[jax.ShapeDtypeStruct((n_seq * seq, SSM_WIDTH), F32), st_shape, st_shape,
                   attn_shape, attn_shape],
        scratch_shapes=[pltpu.VMEM((SSM_LANES // LANE_TILE, SCAN_CHUNK, LANE_TILE), F32),
                        pltpu.VMEM((SSM_LANES // LANE_TILE, SCAN_CHUNK, LANE_TILE), F32),
                        pltpu.VMEM((8, SUBLANES, SSM_LANES), F32),
                        pltpu.VMEM((2, SUBLANES, SSM_LANES), F32)],
        compiler_params=_cparams(("parallel", "arbitrary")),
        name="scan_prompt_attn_sample",
    )(u, pm, pm.T, ab, bre, bim, cre, cim, d_skip, qkv_s, kt, vt)
    return (y, hre, him), (o, lse)


def _merge_kernel(x_ref, mod_ref, y_ref, o0_ref, l0_ref, o1_ref, l1_ref, o2_ref, l2_ref,
                  gmix_ref, gffn_ref, wmg_ref, bmg_ref, wglu_ref, wbs_ref, wba_ref, wout_ref,
                  wrh_ref, wrl_ref, br_ref, x1_ref, h2_ref, comb_ref):
    x = x_ref[...]
    tm = x.shape[0]
    hb = (_rmsnorm(x, gmix_ref[...]) * (1.0 + mod_ref[1]) + mod_ref[0]).astype(BF16)

    y = _gelu_tanh(y_ref[...])
    yg = y * _sigmoid(_dot(y.astype(BF16), wglu_ref[...]))
    ssm_branch = _dot(yg.astype(BF16), wbs_ref[...])

    pairs = []
    for hp in range(2):
        l0, l1, l2 = l0_ref[hp], l1_ref[hp], l2_ref[hp]
        lm = jnp.maximum(jnp.maximum(l0, l1), l2)
        e0, e1, e2 = jnp.exp(l0 - lm), jnp.exp(l1 - lm), jnp.exp(l2 - lm)
        pairs.append((e0 * o0_ref[hp] + e1 * o1_ref[hp] + e2 * o2_ref[hp]) / (e0 + e1 + e2))
    attn = jnp.concatenate(pairs, axis=-1)
    attn_branch = _dot(attn.astype(BF16), wba_ref[...])

    g_ssm = _sigmoid(_dot(hb, wmg_ref[:, :D_MODEL]) + bmg_ref[:, :D_MODEL])
    g_attn = _sigmoid(_dot(hb, wmg_ref[:, D_MODEL:]) + bmg_ref[:, D_MODEL:])
    merged = g_ssm * ssm_branch + g_attn * attn_branch
    x1 = x + mod_ref[2] * _dot(merged.astype(BF16), wout_ref[...])
    x1_ref[...] = x1

    h2 = _rmsnorm(x1, gffn_ref[...]) * (1.0 + mod_ref[4]) + mod_ref[3]
    h2_hi = h2.astype(BF16)
    h2_ref[...] = h2_hi
    h2_lo = (h2 - h2_hi.astype(F32)).astype(BF16)
    logits = (_dot(h2_hi, wrh_ref[...]) + _dot(h2_hi, wrl_ref[...]) + _dot(h2_lo, wrh_ref[...])
              + br_ref[...])

    lane = lax.broadcasted_iota(jnp.int32, (tm, ROUTER_LANES), 1)
    lanef = lane.astype(F32)
    low = jnp.float32(-3.0e38)
    is_grp = (lane >= GROUP_LOGIT_LANE0) & (lane < GROUP_LOGIT_LANE0 + N_EXPERT_GROUPS)
    lg = jnp.where(is_grp, logits, low)
    mg = jnp.max(lg, axis=-1, keepdims=True)
    gsel = jnp.min(jnp.where(lg == mg, lanef - GROUP_LOGIT_LANE0, 1e9), axis=-1, keepdims=True)
    pg_sel = 1.0 / jnp.sum(jnp.exp(lg - mg), axis=-1, keepdims=True)
    in_grp = (lane < N_EXPERTS) & ((lane // EXPERTS_PER_GROUP).astype(F32) == gsel)
    le = jnp.where(in_grp, logits, low)
    m1 = jnp.max(le, axis=-1, keepdims=True)
    i1 = jnp.min(jnp.where(le == m1, lanef, 1e9), axis=-1, keepdims=True)
    le2 = jnp.where(lanef == i1, low, le)
    m2 = jnp.max(le2, axis=-1, keepdims=True)
    i2 = jnp.min(jnp.where(le2 == m2, lanef, 1e9), axis=-1, keepdims=True)
    r = jnp.exp(m2 - m1)
    w1 = 1.0 / (1.0 + r)
    w2 = r / (1.0 + r)
    comb = pg_sel * (jnp.where(lanef == i1, w1, 0.0) + jnp.where(lanef == i2, w2, 0.0))
    comb_ref[...] = comb + jnp.where(lane == GROUP_ID_LANE, gsel, 0.0)


def _merge_call(x, mod, y, attn_outs, p, l, tm, tiles_per_seq):
    ntok = x.shape[0]
    tok = lambda w: pl.BlockSpec((tm, w), lambda i: (i, 0))
    lay = lambda *shape: pl.BlockSpec((None,) + shape, lambda i: (l,) + (0,) * len(shape))
    attn_specs = [pl.BlockSpec((2, tm, PAIR_WIDTH), lambda i: (0, i, 0))] * 6
    return pl.pallas_call(
        _merge_kernel,
        grid=(ntok // tm,),
        in_specs=[tok(D_MODEL), _mod_spec(mod, tm, tiles_per_seq), tok(SSM_WIDTH)] + attn_specs + [
            lay(1, D_MODEL), lay(1, D_MODEL),
            lay(D_MODEL, 2 * D_MODEL), lay(1, 2 * D_MODEL),
            lay(SSM_WIDTH, SSM_WIDTH), lay(SSM_WIDTH, D_MODEL), lay(GROUP_WIDTH, D_MODEL),
            lay(D_MODEL, D_MODEL),
            lay(D_MODEL, ROUTER_LANES), lay(D_MODEL, ROUTER_LANES), lay(1, ROUTER_LANES),
        ],
        out_specs=[tok(D_MODEL), tok(D_MODEL), tok(ROUTER_LANES)],
        out_shape=[jax.ShapeDtypeStruct((ntok, D_MODEL), F32),
                   jax.ShapeDtypeStruct((ntok, D_MODEL), BF16),
                   jax.ShapeDtypeStruct((ntok, ROUTER_LANES), F32)],
        compiler_params=_cparams(("parallel",)),
        name="merge_router",
    )(x, mod, y, *attn_outs, p['g_mix'], p['g_ffn'], p['w_mgate'], p['b_mgate'], p['w_glu'],
      p['w_br_ssm'], p['w_br_attn'], p['w_out'], p['w_router_hi'], p['w_router_lo'], p['b_router'])


MOE_EXPERT_BLOCK = 4
MOE_ROW_CHUNK = 128
GROUP_ID_LANE = 64
_TN = (((0,), (0,)), ((), ()))


def _split3(x):
    a = x.astype(BF16)
    r = x - a.astype(F32)
    b = r.astype(BF16)
    return a, b, (r - b.astype(F32)).astype(BF16)


def _moe_kernel(h2_ref, comb_ref, x1_ref, mod_ref, wg_ref, wu_ref, wd_ref, gfin_ref, out_ref,
                perm, earlier, hs, combs, ys, bounds, *, final):
    s = pl.program_id(1)
    tm = h2_ref.shape[0]
    ch = MOE_ROW_CHUNK

    @pl.when(s == 0)
    def _():
        comb = comb_ref[...]
        lane = lax.broadcasted_iota(jnp.int32, (tm, ROUTER_LANES), 1)
        gid = jnp.sum(jnp.where(lane == GROUP_ID_LANE, comb, 0.0), axis=-1, keepdims=True)
        onehot = jnp.where(lane.astype(F32) == gid, 1.0, 0.0)
        col = lax.broadcasted_iota(jnp.int32, (tm, tm), 1)

        @pl.when(pl.program_id(0) == 0)
        def _():
            row = lax.broadcasted_iota(jnp.int32, (tm, tm), 0)
            earlier[...] = jnp.where(col < row, 1.0, 0.0).astype(BF16)

        rank = _dot(earlier[...], onehot.astype(BF16))
        tot = jnp.sum(onehot, axis=0, keepdims=True)
        lane1 = lax.broadcasted_iota(jnp.int32, (1, ROUTER_LANES), 1)
        start = jnp.zeros((1, ROUTER_LANES), F32)
        run = jnp.zeros((1, 1), F32)
        for g in range(N_EXPERT_GROUPS):
            bounds[2 * g] = run[0, 0].astype(jnp.int32)
            start = start + jnp.where(lane1 == g, run, 0.0)
            run = run + jnp.sum(jnp.where(lane1 == g, tot, 0.0), axis=-1, keepdims=True)
            bounds[2 * g + 1] = run[0, 0].astype(jnp.int32)
        pos = jnp.sum(onehot * (rank + start), axis=-1, keepdims=True)
        perm[...] = jnp.where(col.astype(F32) == pos, 1.0, 0.0).astype(BF16)
        hs[...] = lax.dot_general(perm[...], h2_ref[...], _TN, preferred_element_type=F32).astype(BF16)
        parts = lax.dot_general(perm[...], jnp.concatenate(_split3(comb), axis=-1), _TN,
                                preferred_element_type=F32)
        combs[...] = (parts[:, :ROUTER_LANES] + parts[:, ROUTER_LANES:2 * ROUTER_LANES]) + parts[:, 2 * ROUTER_LANES:]
        ys[...] = jnp.zeros_like(ys)

    g = s // (EXPERTS_PER_GROUP // MOE_EXPERT_BLOCK)
    lo = bounds[2 * g] // ch
    hi = (bounds[2 * g + 1] + (ch - 1)) // ch
    lane = lax.broadcasted_iota(jnp.int32, (ch, ROUTER_LANES), 1)

    def chunk(c):
        r0 = pl.multiple_of(c * ch, ch)
        h = hs[pl.ds(r0, ch), :]
        comb = combs[pl.ds(r0, ch), :]
        hids = []
        for j in range(MOE_EXPERT_BLOCK):
            gt = _dot(h, wg_ref[j])
            w = jnp.sum(jnp.where(lane == s * MOE_EXPERT_BLOCK + j, comb, 0.0), axis=-1, keepdims=True)
            hids.append((gt * _sigmoid(gt) * _dot(h, wu_ref[j]) * w).astype(BF16))
        ys[pl.ds(r0, ch), :] += _dot(jnp.concatenate(hids, axis=-1), wd_ref[...])

    def pair(i, carry):
        chunk(lo + 2 * i)
        chunk(lo + 2 * i + 1)
        return carry

    n_chunks = hi - lo
    lax.fori_loop(0, n_chunks // 2, pair, 0)

    @pl.when(n_chunks % 2 == 1)
    def _():
        chunk(hi - 1)

    @pl.when(s == pl.num_programs(1) - 1)
    def _():
        x2 = x1_ref[...] + mod_ref[...] * _dot(perm[...], ys[...].astype(BF16))
        out_ref[...] = _rmsnorm(x2, gfin_ref[...]) if final else x2


def _moe_call(h2, comb, x1, mod, p, l, tm, tiles_per_seq, final):
    ntok = h2.shape[0]
    eb = MOE_EXPERT_BLOCK
    tok = lambda w: pl.BlockSpec((tm, w), lambda i, s: (i, 0))
    gate_piece = 5
    if mod.ndim == 4:
        mod_spec = pl.BlockSpec((None, None, 1, D_MODEL), lambda i, s: (gate_piece, i // tiles_per_seq, 0, 0))
    else:
        mod_spec = pl.BlockSpec((None, tm, D_MODEL), lambda i, s: (gate_piece, i, 0))
    w_down = p['w_exp_down'].reshape(DEPTH, N_EXPERTS // eb, eb * EXPERT_FF, D_MODEL)
    return pl.pallas_call(
        functools.partial(_moe_kernel, final=final),
        grid=(ntok // tm, N_EXPERTS // eb),
        in_specs=[tok(D_MODEL), tok(ROUTER_LANES), tok(D_MODEL), mod_spec,
                  pl.BlockSpec((None, eb, D_MODEL, EXPERT_FF), lambda i, s: (l, s, 0, 0)),
                  pl.BlockSpec((None, eb, D_MODEL, EXPERT_FF), lambda i, s: (l, s, 0, 0)),
                  pl.BlockSpec((None, None, eb * EXPERT_FF, D_MODEL), lambda i, s: (l, s, 0, 0)),
                  pl.BlockSpec((1, D_MODEL), lambda i, s: (0, 0))],
        out_specs=tok(D_MODEL),
        out_shape=jax.ShapeDtypeStruct((ntok, D_MODEL), F32),
        scratch_shapes=[pltpu.VMEM((tm, tm), BF16), pltpu.VMEM((tm, tm), BF16), pltpu.VMEM((tm, D_MODEL), BF16),
                        pltpu.VMEM((tm, ROUTER_LANES), F32), pltpu.VMEM((tm, D_MODEL), F32),
                        pltpu.SMEM((2 * N_EXPERT_GROUPS,), jnp.int32)],
        compiler_params=_cparams(("arbitrary", "arbitrary")),
        name="moe_experts",
    )(h2, comb, x1, mod, p['w_exp_gate'], p['w_exp_up'], w_down, p['g_final'])


def _kv_state(qkv, n_seq, seq, prompt):
    kv = []
    for g in range(N_GROUPS):
        keep = min(WINDOWS[g], seq)
        if prompt:
            tails = _kv_tail_call(qkv[g], n_seq, seq, keep).reshape(2, n_seq, HEADS, HEAD_DIM, keep)
            kv += [tails[0].transpose(0, 3, 1, 2), tails[1].transpose(0, 3, 1, 2)]
        else:
            for a in (1, 2):
                new = qkv[g][a].reshape(2, n_seq, seq, 2, HEAD_DIM)
                kv.append(new.transpose(1, 2, 0, 3, 4).reshape(n_seq, seq, HEADS, HEAD_DIM))
    return kv


def _run_trunks(xp, xs, mod_p, mod_s, p, ssm, batch, seq, dec_batch, dec_seq, caches, h0):
    tm_p, tm_s = 512, 256
    tm_moe_p, tm_moe_s = 1024, 512
    wide = N_GROUPS - 1
    states_p, states_s = [], []
    for l in range(DEPTH):
        ab, bre, bim, cre, cim, d_skip = ssm[l]
        up, *qkv_p = _inproj_call(xp, mod_p[l], p['g_mix'], p['w_in'], l, tm_p, seq // tm_p)
        us, *qkv_s = _inproj_call(xs, mod_s[l], p['g_mix'], p['w_in'], l, tm_s, 1)

        (yp, hre_p, him_p), attn_wide = _scan_attn_call(
            up, ab, bre, bim, cre, cim, d_skip, batch, seq,
            qkv_s[wide], caches[2 * wide], caches[2 * wide + 1], l, dec_batch, dec_seq, DILATIONS[wide])
        attn_p, attn_s = [], []
        for g in range(N_GROUPS - 1):
            o_p, o_s = _attn_prompt_call(qkv_p[g], batch, seq, DILATIONS[g],
                                         sample=(qkv_s[g], caches[2 * g], caches[2 * g + 1], l, dec_batch, dec_seq))
            attn_p += list(o_p)
            attn_s += list(o_s)
        attn_p += list(_attn_prompt_call(qkv_p[wide], batch, seq, DILATIONS[wide]))
        attn_s += list(attn_wide)
        x1, h2, comb = _merge_call(xp, mod_p[l], yp, attn_p, p, l, tm_p, seq // tm_p)
        xp = _moe_call(h2, comb, x1, mod_p[l], p, l, tm_moe_p, seq // tm_moe_p, final=(l == DEPTH - 1))
        states_p.append(_kv_state(qkv_p, batch, seq, True)
                        + [hre_p[:, 0].reshape(batch, N_SSM_GROUPS, SSM_STATE),
                           him_p[:, 0].reshape(batch, N_SSM_GROUPS, SSM_STATE)])

        ys, hre_s, him_s = _ssm_sample_call(us.reshape(dec_batch, dec_seq * SSM_WIDTH), ab, bre, bim, cre, cim,
                                            d_skip, h0[0][l], h0[1][l], dec_seq)
        x1, h2, comb = _merge_call(xs, mod_s[l], ys.reshape(dec_batch * dec_seq, SSM_WIDTH), attn_s, p, l, tm_s, 1)
        xs = _moe_call(h2, comb, x1, mod_s[l], p, l, tm_moe_s, 1, final=(l == DEPTH - 1))
        states_s.append(_kv_state(qkv_s, dec_batch, dec_seq, False)
                        + [hre_s.reshape(dec_batch, N_SSM_GROUPS, SSM_STATE),
                           him_s.reshape(dec_batch, N_SSM_GROUPS, SSM_STATE)])
    stack = lambda states: [jnp.stack([states[l][i] for l in range(DEPTH)], axis=0)
                            for i in range(len(states[0]))]
    return xp, xs, stack(states_p), stack(states_s)


def kernel(x_prompt, x_sample, cache_k_w128, cache_v_w128, cache_k_w512, cache_v_w512,
           cache_k_w2048, cache_v_w2048, state_ssm_re, state_ssm_im, c_prompt, c_sample,
           w_ada, b_ada, g_norm_mix, g_norm_ffn, g_final, w_in, ssm_a_re, ssm_a_im, ssm_b_re,
           ssm_b_im, ssm_c_re, ssm_c_im, ssm_log_step, ssm_d, w_glu, w_br_ssm, w_br_attn,
           w_mgate, b_mgate, w_out, w_router_grp, b_router_grp, w_router_exp, b_router_exp,
           w_exp_gate, w_exp_up, w_exp_down):
    batch, seq, _ = x_prompt.shape
    dec_batch, dec_seq, _ = x_sample.shape

    pad = -batch % SUBLANES
    c_p = jnp.concatenate([c_prompt, jnp.zeros((pad, D_MODEL), F32)], axis=0)
    mod_p, mod_s = _ada_call(c_p, jnp.repeat(c_sample, dec_seq, axis=0), w_ada, b_ada)
    mod_prompt = [mod_p[l, :, :batch].reshape(6, batch, 1, D_MODEL) for l in range(DEPTH)]
    mod_sample = [mod_s[l] for l in range(DEPTH)]

    abr, abi, bbr, bbi = _ssm_param_call(ssm_a_re, ssm_a_im, ssm_log_step, ssm_b_re, ssm_b_im)
    ssm = []
    for l in range(DEPTH):
        pick = lambda a: a[l].reshape(N_SSM_GROUPS, SSM_STATE, SSM_GROUP_CH)
        ab = jnp.stack([pick(abr)[:, :, 0].reshape(SSM_LANES), pick(abi)[:, :, 0].reshape(SSM_LANES)])
        bre = _block_diag_halves(pick(bbr).transpose(0, 2, 1), BF16)
        bim = _block_diag_halves(pick(bbi).transpose(0, 2, 1), BF16)
        cre = _block_diag_halves(ssm_c_re[l].transpose(0, 2, 1), BF16)
        cim = _block_diag_halves(ssm_c_im[l].transpose(0, 2, 1), BF16)
        ssm.append((ab, bre, bim, cre, cim, ssm_d[l].reshape(1, SSM_WIDTH)))

    w_router = jnp.concatenate(
        [w_router_exp, w_router_grp,
         jnp.zeros((DEPTH, D_MODEL, ROUTER_LANES - N_EXPERTS - N_EXPERT_GROUPS), F32)], axis=-1)
    b_router = jnp.concatenate(
        [b_router_exp, b_router_grp,
         jnp.zeros((DEPTH, ROUTER_LANES - N_EXPERTS - N_EXPERT_GROUPS), F32)], axis=-1)
    w_router_hi = w_router.astype(BF16)
    p = dict(
        g_mix=g_norm_mix.reshape(DEPTH, 1, D_MODEL), g_ffn=g_norm_ffn.reshape(DEPTH, 1, D_MODEL),
        g_final=g_final.reshape(1, D_MODEL), w_in=w_in.astype(BF16), w_glu=w_glu.astype(BF16),
        w_br_ssm=w_br_ssm.astype(BF16), w_br_attn=w_br_attn.astype(BF16),
        w_mgate=w_mgate.astype(BF16), b_mgate=b_mgate.reshape(DEPTH, 1, 2 * D_MODEL),
        w_out=w_out.astype(BF16), w_router_hi=w_router_hi,
        w_router_lo=(w_router - w_router_hi.astype(F32)).astype(BF16),
        b_router=b_router.reshape(DEPTH, 1, ROUTER_LANES),
        w_exp_gate=w_exp_gate.astype(BF16), w_exp_up=w_exp_up.astype(BF16),
        w_exp_down=w_exp_down.astype(BF16))

    caches = [cache_k_w128, cache_v_w128, cache_k_w512, cache_v_w512, cache_k_w2048, cache_v_w2048]
    h0 = (state_ssm_re.reshape(DEPTH, dec_batch, SSM_LANES), state_ssm_im.reshape(DEPTH, dec_batch, SSM_LANES))
    y_prompt, y_sample, pstate, sstate = _run_trunks(
        x_prompt.reshape(batch * seq, D_MODEL), x_sample.reshape(dec_batch * dec_seq, D_MODEL),
        mod_prompt, mod_sample, p, ssm, batch, seq, dec_batch, dec_seq, caches, h0)
    return (y_prompt.reshape(batch, seq, D_MODEL), y_sample.reshape(dec_batch, dec_seq, D_MODEL),
            *pstate, *sstate)
```

```python
import functools
import math

import jax
import jax.numpy as jnp
from jax import lax
from jax.experimental import pallas as pl
from jax.experimental.pallas import tpu as pltpu

F32 = jnp.float32
BF16 = jnp.bfloat16

D_MODEL = 1024
DEPTH = 2
SSM_WIDTH = 512
SSM_GROUP_CH = 16
N_SSM_GROUPS = 32
SSM_STATE = 64
SSM_LANES = N_SSM_GROUPS * SSM_STATE
HEAD_DIM = 64
HEADS = 4
GROUP_WIDTH = HEADS * HEAD_DIM
PAIR_WIDTH = 2 * HEAD_DIM
WINDOWS = (128, 512, 2048)
DILATIONS = (1, 4, 16)
N_BACK = 128
N_GROUPS = 3
QKV_WIDTH = N_GROUPS * GROUP_WIDTH
IN_PROJ_WIDTH = SSM_WIDTH + 3 * QKV_WIDTH
N_EXPERT_GROUPS = 4
EXPERTS_PER_GROUP = 8
N_EXPERTS = 32
EXPERT_FF = 256
RMS_EPS = 1e-6
NEG_INF = -1e30
ROUTER_LANES = 128
GROUP_LOGIT_LANE0 = N_EXPERTS

SUBLANES = 8
VMEM_LIMIT = 56 * 1024 * 1024


def _cparams(sem):
    return pltpu.CompilerParams(dimension_semantics=sem, vmem_limit_bytes=VMEM_LIMIT)


def _sigmoid(x):
    return 1.0 / (1.0 + jnp.exp(-x))


def _gelu_tanh(x):
    return 0.5 * x * (1.0 + jnp.tanh(math.sqrt(2.0 / math.pi) * (x + 0.044715 * (x * x * x))))


def _dot(a, b):
    return jnp.dot(a, b, preferred_element_type=F32)


def _rmsnorm(x, g):
    ms = jnp.mean(x * x, axis=-1, keepdims=True)
    return x * lax.rsqrt(ms + RMS_EPS) * g


def _ada_kernel(cp_ref, cs_ref, w_ref, b_ref, op_ref, os_ref):
    w = w_ref[...].astype(BF16)
    for c_ref, o_ref in ((cp_ref, op_ref), (cs_ref, os_ref)):
        c = c_ref[...]
        o_ref[...] = _dot((c * _sigmoid(c)).astype(BF16), w) + b_ref[...]


def _ada_call(c_prompt_rows, c_sample_rows, w_ada, b_ada):
    rp, rs = c_prompt_rows.shape[0], c_sample_rows.shape[0]
    out = lambda r: pl.BlockSpec((None, None, r, D_MODEL), lambda l, j: (l, j, 0, 0))
    return pl.pallas_call(
        _ada_kernel,
        grid=(DEPTH, 6),
        in_specs=[
            pl.BlockSpec((rp, D_MODEL), lambda l, j: (0, 0)),
            pl.BlockSpec((rs, D_MODEL), lambda l, j: (0, 0)),
            pl.BlockSpec((None, D_MODEL, D_MODEL), lambda l, j: (l, 0, j)),
            pl.BlockSpec((None, 1, D_MODEL), lambda l, j: (l, 0, j)),
        ],
        out_specs=[out(rp), out(rs)],
        out_shape=[jax.ShapeDtypeStruct((DEPTH, 6, rp, D_MODEL), F32),
                   jax.ShapeDtypeStruct((DEPTH, 6, rs, D_MODEL), F32)],
        compiler_params=_cparams(("parallel", "parallel")),
        name="ada_mod",
    )(c_prompt_rows, c_sample_rows, w_ada, b_ada.reshape(DEPTH, 1, 6 * D_MODEL))


def _ssm_param_kernel(ar_ref, ai_ref, ls_ref, br_ref, bi_ref, abr_ref, abi_ref, bbr_ref, bbi_ref):
    lr, li = ar_ref[...], ai_ref[...]
    dt = jnp.exp(ls_ref[...])
    mag = jnp.exp(lr * dt)
    abr = mag * jnp.cos(li * dt)
    abi = mag * jnp.sin(li * dt)
    den = lr * lr + li * li
    er, ei = abr - 1.0, abi
    fr = (er * lr + ei * li) / den
    fi = (ei * lr - er * li) / den
    br, bi = br_ref[...], bi_ref[...]
    abr_ref[...] = abr
    abi_ref[...] = abi
    bbr_ref[...] = fr * br - fi * bi
    bbi_ref[...] = fr * bi + fi * br


def _ssm_param_call(a_re, a_im, log_step, b_re, b_im):
    w = SSM_STATE * SSM_GROUP_CH
    rep = lambda a: jnp.repeat(a, SSM_GROUP_CH, axis=-1)
    big = pl.BlockSpec((None, N_SSM_GROUPS, w), lambda l: (l, 0, 0))
    shp = jax.ShapeDtypeStruct((DEPTH, N_SSM_GROUPS, w), F32)
    return pl.pallas_call(
        _ssm_param_kernel,
        grid=(DEPTH,),
        in_specs=[big, big, pl.BlockSpec((None, N_SSM_GROUPS, 1), lambda l: (l, 0, 0)), big, big],
        out_specs=[big, big, big, big],
        out_shape=[shp, shp, shp, shp],
        compiler_params=_cparams(("parallel",)),
        name="ssm_params",
    )(rep(a_re), rep(a_im), log_step.reshape(DEPTH, N_SSM_GROUPS, 1),
      b_re.reshape(DEPTH, N_SSM_GROUPS, w), b_im.reshape(DEPTH, N_SSM_GROUPS, w))


def _block_diag_halves(blocks, dtype):
    g, r, c = blocks.shape
    half = g // 2
    eye = jnp.eye(half, dtype=bool)

    def one(b):
        m = jnp.where(eye[:, None, :, None], b[:, :, None, :], 0.0)
        return m.reshape(half * r, half * c)

    return jnp.stack([one(blocks[:half]), one(blocks[half:])]).astype(dtype)


def _inproj_kernel(x_ref, mod_ref, g_ref, w_ref, u_ref, a0_ref, a1_ref, a2_ref):
    h = _rmsnorm(x_ref[...], g_ref[...]) * (1.0 + mod_ref[1]) + mod_ref[0]
    hb = h.astype(BF16)
    u_ref[...] = _dot(hb, w_ref[:, :SSM_WIDTH])
    for g, ref in enumerate((a0_ref, a1_ref, a2_ref)):
        for a in range(3):
            c0 = SSM_WIDTH + a * QKV_WIDTH + g * GROUP_WIDTH
            val = _dot(hb, w_ref[:, c0:c0 + GROUP_WIDTH])
            ref[a, 0] = val[:, :PAIR_WIDTH]
            ref[a, 1] = val[:, PAIR_WIDTH:]


def _mod_spec(mod, l, tm, tiles_per_seq):
    if mod.ndim == 5:
        return pl.BlockSpec((None, 6, None, 1, D_MODEL), lambda i: (l, 0, i // tiles_per_seq, 0, 0))
    return pl.BlockSpec((None, 6, tm, D_MODEL), lambda i: (l, 0, i, 0))


def _inproj_call(x, mod, g_mix, w_in, l, tm, tiles_per_seq):
    ntok = x.shape[0]
    grp = jax.ShapeDtypeStruct((3, 2, ntok, PAIR_WIDTH), F32)
    grp_spec = pl.BlockSpec((3, 2, tm, PAIR_WIDTH), lambda i: (0, 0, i, 0))
    return pl.pallas_call(
        _inproj_kernel,
        grid=(ntok // tm,),
        in_specs=[
            pl.BlockSpec((tm, D_MODEL), lambda i: (i, 0)),
            _mod_spec(mod, l, tm, tiles_per_seq),
            pl.BlockSpec((None, 1, D_MODEL), lambda i: (l, 0, 0)),
            pl.BlockSpec((None, D_MODEL, IN_PROJ_WIDTH), lambda i: (l, 0, 0)),
        ],
        out_specs=[pl.BlockSpec((tm, SSM_WIDTH), lambda i: (i, 0)), grp_spec, grp_spec, grp_spec],
        out_shape=[jax.ShapeDtypeStruct((ntok, SSM_WIDTH), F32), grp, grp, grp],
        compiler_params=_cparams(("parallel",)),
        name="in_proj",
    )(x, mod, g_mix, w_in)


LANE_TILE = 128
SCAN_CHUNK = 256
SEG_LEN = SCAN_CHUNK // SUBLANES
SCAN_TILE_GROUP = 4


def _cmul(ar, ai, br, bi):
    return ar * br - ai * bi, ar * bi + ai * br


def _ssm_prompt_kernel(u_ref, pm_ref, pmt_ref, ab_ref, bre_ref, bim_ref, cre_ref, cim_ref, d_ref,
                       y_ref, hre_ref, him_ref, bur, bui, tab, carry):
    c = pl.program_id(1)
    half = SSM_LANES // 2
    tiles_per_half = half // LANE_TILE

    @pl.when(c == 0)
    def _():
        shp = (SUBLANES, SSM_LANES)
        sub = lax.broadcasted_iota(jnp.int32, shp, 0)
        b_r = jnp.broadcast_to(ab_ref[0:1, :], shp)
        b_i = jnp.broadcast_to(ab_ref[1:2, :], shp)
        for _ in range(SEG_LEN.bit_length() - 1):
            b_r, b_i = _cmul(b_r, b_i, b_r, b_i)
        p_r, p_i = b_r, b_i
        ps_r, ps_i = jnp.ones(shp, F32), jnp.zeros(shp, F32)
        for s in range(1, SUBLANES):
            if s in (1, 2, 4):
                k = {1: 0, 2: 2, 4: 4}[s]
                tab[k] = jnp.where(sub >= s, p_r, 0.0)
                tab[k + 1] = jnp.where(sub >= s, p_i, 0.0)
            ps_r = jnp.where(sub >= s, p_r, ps_r)
            ps_i = jnp.where(sub >= s, p_i, ps_i)
            p_r, p_i = _cmul(p_r, p_i, b_r, b_i)
        tab[6] = ps_r
        tab[7] = ps_i
        carry[...] = jnp.zeros_like(carry)

    u = u_ref[...]
    ub = _dot(pm_ref[...], u.astype(BF16)).astype(BF16)
    for h in range(2):
        uh = ub[:, h * 256:(h + 1) * 256]
        res_r = _dot(uh, bre_ref[h])
        res_i = _dot(uh, bim_ref[h])
        for j in range(tiles_per_half):
            bur[h * tiles_per_half + j] = res_r[:, j * LANE_TILE:(j + 1) * LANE_TILE]
            bui[h * tiles_per_half + j] = res_i[:, j * LANE_TILE:(j + 1) * LANE_TILE]

    vshape = (SUBLANES, LANE_TILE)
    sub1 = lax.broadcasted_iota(jnp.int32, vshape, 0) >= 1
    for g0 in range(0, SSM_LANES // LANE_TILE, SCAN_TILE_GROUP):
        tiles = list(range(g0, g0 + SCAN_TILE_GROUP))
        lanes = [slice(lt * LANE_TILE, (lt + 1) * LANE_TILE) for lt in tiles]
        a_r = [jnp.broadcast_to(ab_ref[0:1, sl], vshape) for sl in lanes]
        a_i = [jnp.broadcast_to(ab_ref[1:2, sl], vshape) for sl in lanes]

        def step(t, hs, store):
            out = []
            for j, lt in enumerate(tiles):
                rows = pl.ds(pl.multiple_of(t * SUBLANES, SUBLANES), SUBLANES)
                hr, hi = hs[2 * j], hs[2 * j + 1]
                nr = (a_r[j] * hr - a_i[j] * hi) + bur[lt, rows, :]
                ni = (a_r[j] * hi + a_i[j] * hr) + bui[lt, rows, :]
                if store:
                    bur[lt, rows, :] = nr
                    bui[lt, rows, :] = ni
                out += [nr, ni]
            return tuple(out)

        zeros = tuple(jnp.zeros(vshape, F32) for _ in range(2 * SCAN_TILE_GROUP))
        ends = lax.fori_loop(0, SEG_LEN, lambda t, hs: step(t, hs, False), zeros, unroll=True)

        starts = []
        for j, sl in enumerate(lanes):
            fr, fi = ends[2 * j], ends[2 * j + 1]
            for k in (1, 2, 4):
                ti = {1: 0, 2: 2, 4: 4}[k]
                mr, mi = tab[ti, :, sl], tab[ti + 1, :, sl]
                rr = pltpu.roll(fr, k, axis=0)
                ri = pltpu.roll(fi, k, axis=0)
                fr, fi = fr + (mr * rr - mi * ri), fi + (mr * ri + mi * rr)
            cr, ci = carry[0, :, sl], carry[1, :, sl]
            ps_r, ps_i = tab[6, :, sl], tab[7, :, sl]
            sr = (ps_r * cr - ps_i * ci) + jnp.where(sub1, pltpu.roll(fr, 1, axis=0), 0.0)
            si = (ps_r * ci + ps_i * cr) + jnp.where(sub1, pltpu.roll(fi, 1, axis=0), 0.0)
            starts += [sr, si]

        last = lax.fori_loop(0, SEG_LEN, lambda t, hs: step(t, hs, True), tuple(starts), unroll=True)
        for j, sl in enumerate(lanes):
            carry[0, :, sl] = jnp.broadcast_to(last[2 * j][SUBLANES - 1:SUBLANES, :], vshape)
            carry[1, :, sl] = jnp.broadcast_to(last[2 * j + 1][SUBLANES - 1:SUBLANES, :], vshape)

    for h in range(2):
        cs = slice(h * 256, (h + 1) * 256)
        tl = range(h * tiles_per_half, (h + 1) * tiles_per_half)
        hr = jnp.concatenate([bur[lt] for lt in tl], axis=-1).astype(BF16)
        hi = jnp.concatenate([bui[lt] for lt in tl], axis=-1).astype(BF16)
        y = _dot(hr, cre_ref[h]) - _dot(hi, cim_ref[h])
        y_hi = y.astype(BF16)
        y_lo = (y - y_hi.astype(F32)).astype(BF16)
        y = _dot(pmt_ref[...], y_hi) + _dot(pmt_ref[...], y_lo)
        y_ref[:, cs] = y + d_ref[:, cs] * u[:, cs]

    @pl.when(c == pl.num_programs(1) - 1)
    def _():
        hre_ref[...] = carry[0]
        him_ref[...] = carry[1]


def _ssm_sample_kernel(u_ref, ab_ref, bre_ref, bim_ref, cre_ref, cim_ref, d_ref, h0r_ref, h0i_ref,
                       y_ref, hr_ref, hi_ref, *, steps):
    half = SSM_LANES // 2
    ar, ai = ab_ref[0:1, :], ab_ref[1:2, :]
    hr_ref[...] = h0r_ref[...]
    hi_ref[...] = h0i_ref[...]
    for t in range(steps):
        u = u_ref[:, t * SSM_WIDTH:(t + 1) * SSM_WIDTH]
        ub = u.astype(BF16)
        for h in range(2):
            hs = slice(h * half, (h + 1) * half)
            cs = slice(h * 256, (h + 1) * 256)
            uh = ub[:, cs]
            hr, hi = hr_ref[:, hs], hi_ref[:, hs]
            a_r, a_i = ar[:, hs], ai[:, hs]
            nr = (a_r * hr - a_i * hi) + _dot(uh, bre_ref[h])
            ni = (a_r * hi + a_i * hr) + _dot(uh, bim_ref[h])
            hr_ref[:, hs] = nr
            hi_ref[:, hs] = ni
            y = _dot(nr.astype(BF16), cre_ref[h]) - _dot(ni.astype(BF16), cim_ref[h])
            y_ref[:, t * SSM_WIDTH + h * 256:t * SSM_WIDTH + (h + 1) * 256] = y + d_ref[:, cs] * u[:, cs]


def _ssm_sample_call(u, ab, bre, bim, cre, cim, d_skip, h0r, h0i, steps):
    n = u.shape[0]
    st = jax.ShapeDtypeStruct((n, SSM_LANES), F32)
    return pl.pallas_call(
        functools.partial(_ssm_sample_kernel, steps=steps),
        out_shape=[jax.ShapeDtypeStruct((n, steps * SSM_WIDTH), F32), st, st],
        compiler_params=pltpu.CompilerParams(vmem_limit_bytes=VMEM_LIMIT),
        name="ssm_sample",
    )(u, ab, bre, bim, cre, cim, d_skip, h0r, h0i)


_NT = (((1,), (1,)), ((), ()))


def _attn_prompt_kernel(q_ref, kp_ref, kc_ref, vp_ref, vc_ref, o_ref, lse_ref, *, dil, q_blocks):
    b = pl.program_id(2)
    qi = lax.broadcasted_iota(jnp.int32, (N_BACK, 2 * N_BACK), 0)
    kj = lax.broadcasted_iota(jnp.int32, (N_BACK, 2 * N_BACK), 1)
    band = (kj >= qi) & (kj <= qi + N_BACK)
    band_first = band & (kj >= jnp.where(b > 0, 0, N_BACK))
    lane = lax.broadcasted_iota(jnp.int32, (N_BACK, PAIR_WIDTH), 1)
    head0 = lane < HEAD_DIM

    def rows(j, r):
        start = j * N_BACK * dil + r
        return pl.ds(start, N_BACK, stride=dil) if dil > 1 else pl.ds(start, N_BACK)

    def one(j, r):
        cur = rows(j, r)
        q = (q_ref[cur, :] * (HEAD_DIM ** -0.5)).astype(BF16)
        k_prev = kp_ref[rows(0, r), :] if j == 0 else kc_ref[rows(j - 1, r), :]
        v_prev = vp_ref[rows(0, r), :] if j == 0 else vc_ref[rows(j - 1, r), :]
        kk = jnp.concatenate([k_prev, kc_ref[cur, :]], axis=0).astype(BF16)
        vv = jnp.concatenate([v_prev, vc_ref[cur, :]], axis=0).astype(BF16)
        valid = band_first if j == 0 else band
        outs, lses = [], []
        for sub in range(2):
            in_head = head0 if sub == 0 else jnp.logical_not(head0)
            qm = jnp.where(in_head, q, jnp.zeros_like(q))
            s = lax.dot_general(qm, kk, _NT, preferred_element_type=F32)
            s = jnp.where(valid, s, NEG_INF)
            m = jnp.max(s, axis=-1, keepdims=True)
            p = jnp.exp(s - m)
            den = jnp.sum(p, axis=-1, keepdims=True)
            outs.append(_dot(p.astype(BF16), vv) / den)
            lses.append(jnp.broadcast_to(m + jnp.log(den), (N_BACK, PAIR_WIDTH)))
        o_ref[cur, :] = jnp.where(head0, outs[0], outs[1])
        lse_ref[cur, :] = jnp.where(head0, lses[0], lses[1])

    def residue(r, carry):
        for j in range(q_blocks):
            one(j, r)
        return carry

    if dil <= ATTN_UNROLL:
        for r in range(dil):
            residue(r, 0)
    else:
        lax.fori_loop(0, dil, residue, 0, unroll=ATTN_UNROLL)


ATTN_UNROLL = 4
ATTN_Q_BLOCKS = {1: 8, 4: 2, 16: 1}


def _attn_pair_kernel(q_ref, kp_ref, kc_ref, vp_ref, vc_ref, qkv_s_ref, kt_ref, vt_ref,
                      o_ref, lse_ref, o_s_ref, lse_s_ref, *, dil, q_blocks, steps):
    _attn_sample_kernel(qkv_s_ref, kt_ref, vt_ref, o_s_ref, lse_s_ref, dil=dil, steps=steps)
    _attn_prompt_kernel(q_ref, kp_ref, kc_ref, vp_ref, vc_ref, o_ref, lse_ref, dil=dil, q_blocks=q_blocks)


def _attn_prompt_call(qkv, n_seq, seq, dil, sample=None):
    q_blocks = ATTN_Q_BLOCKS[dil]
    prev_rows = N_BACK * dil
    rows = prev_rows * q_blocks
    nbk = seq // rows
    cur = lambda a: pl.BlockSpec((None, None, rows, PAIR_WIDTH), lambda n, hp, b: (a, hp, n * nbk + b, 0))
    prev = lambda a: pl.BlockSpec(
        (None, None, prev_rows, PAIR_WIDTH),
        lambda n, hp, b: (a, hp, jnp.maximum((n * nbk + b) * q_blocks - 1, n * nbk * q_blocks), 0))
    out_spec = pl.BlockSpec((None, rows, PAIR_WIDTH), lambda n, hp, b: (hp, n * nbk + b, 0))
    out_shape = jax.ShapeDtypeStruct((2, n_seq * seq, PAIR_WIDTH), F32)
    in_specs = [cur(0), prev(1), cur(1), prev(2), cur(2)]
    operands = [qkv, qkv, qkv, qkv, qkv]
    out_specs, out_shapes = [out_spec, out_spec], [out_shape, out_shape]
    if sample is None:
        body = functools.partial(_attn_prompt_kernel, dil=dil, q_blocks=q_blocks)
    else:
        qkv_s, k_cache, v_cache, l, n_dec, steps = sample
        w = k_cache.shape[2]
        nb = SAMPLE_SEQ_BLOCK[w]
        assert n_seq * 2 * nbk * nb == n_dec, "sample blocks must match the prompt attention grid"
        step = lambda n, hp, b: (n * 2 + hp) * nbk + b
        cache_spec = pl.BlockSpec((None, nb, HEADS, HEAD_DIM, w), lambda n, hp, b: (l, step(n, hp, b), 0, 0, 0))
        s_out = pl.BlockSpec((2, nb * steps, PAIR_WIDTH), lambda n, hp, b: (0, step(n, hp, b), 0))
        s_shape = jax.ShapeDtypeStruct((2, n_dec * steps, PAIR_WIDTH), F32)
        in_specs += [pl.BlockSpec((3, 2, nb * steps, PAIR_WIDTH), lambda n, hp, b: (0, 0, step(n, hp, b), 0)),
                     cache_spec, cache_spec]
        operands += [qkv_s, k_cache.transpose(0, 1, 3, 4, 2), v_cache.transpose(0, 1, 3, 4, 2)]
        out_specs += [s_out, s_out]
        out_shapes += [s_shape, s_shape]
        body = functools.partial(_attn_pair_kernel, dil=dil, q_blocks=q_blocks, steps=steps)
    outs = pl.pallas_call(
        body,
        grid=(n_seq, 2, nbk),
        in_specs=in_specs,
        out_specs=out_specs,
        out_shape=out_shapes,
        compiler_params=_cparams(("parallel", "parallel", "parallel")),
        name=f"attn_prompt_d{dil}",
    )(*operands)
    return outs if sample is None else (outs[:2], outs[2:])


def _kv_tail_kernel(x_ref, o_ref):
    o_ref[...] = x_ref[...].T


def _kv_tail_call(qkv, n_seq, seq, keep):
    blocks_per_seq = seq // keep
    return pl.pallas_call(
        _kv_tail_kernel,
        grid=(2, n_seq, 2),
        in_specs=[pl.BlockSpec((None, None, keep, PAIR_WIDTH),
                               lambda a, n, hp: (a + 1, hp, (n + 1) * blocks_per_seq - 1, 0))],
        out_specs=pl.BlockSpec((None, None, None, PAIR_WIDTH, keep), lambda a, n, hp: (a, n, hp, 0, 0)),
        out_shape=jax.ShapeDtypeStruct((2, n_seq, 2, PAIR_WIDTH, keep), F32),
        compiler_params=_cparams(("parallel", "parallel", "parallel")),
        name="kv_tail",
    )(qkv)


SAMPLE_SEQ_BLOCK = {128: 4, 512: 4, 2048: 2}


def _attn_sample_kernel(qkv_ref, kt_ref, vt_ref, o_ref, lse_ref, *, dil, steps):
    nb, _, _, w = kt_ref.shape
    rows = nb * steps
    tok = lax.broadcasted_iota(jnp.int32, (2 * rows, w), 0) % rows
    pos = lax.broadcasted_iota(jnp.int32, (2 * rows, w), 1)
    tokn = lax.broadcasted_iota(jnp.int32, (2 * rows, rows), 0) % rows
    coln = lax.broadcasted_iota(jnp.int32, (2 * rows, rows), 1)
    rowo = lax.broadcasted_iota(jnp.int32, (rows, PAIR_WIDTH), 0)
    head0 = lax.broadcasted_iota(jnp.int32, (rows, PAIR_WIDTH), 1) < HEAD_DIM

    vns, s_buf, s_new = [], {}, []
    for hp in range(2):
        q2 = qkv_ref[0, hp] * (HEAD_DIM ** -0.5)
        q_st = jnp.concatenate([jnp.where(head0, q2, 0.0), jnp.where(head0, 0.0, q2)], axis=0).astype(BF16)
        kn = qkv_ref[1, hp].astype(BF16)
        s_new.append(lax.dot_general(q_st, kn, _NT, preferred_element_type=F32))
        vns.append(qkv_ref[2, hp].astype(BF16))
        for n in range(nb):
            kt = kt_ref[n, 2 * hp:2 * hp + 2].reshape(PAIR_WIDTH, w).astype(BF16)
            s_buf[hp, n] = _dot(q_st, kt)

    probs = {}
    for hp in range(2):
        for n in range(nb):
            t = tok - n * steps
            in_buf = (((pos - t) & (dil - 1)) == 0) & (pos >= t)
            tn = tokn - n * steps
            sn = coln - n * steps
            in_new = ((sn >= 0) & (sn <= tn)) if dil == 1 else (sn == tn)
            s = jnp.where(in_buf, s_buf[hp, n], NEG_INF)
            s2 = jnp.where(in_new, s_new[hp], NEG_INF)
            m = jnp.maximum(jnp.max(s, axis=-1, keepdims=True), jnp.max(s2, axis=-1, keepdims=True))
            p = jnp.exp(s - m)
            p2 = jnp.exp(s2 - m)
            den = jnp.sum(p, axis=-1, keepdims=True) + jnp.sum(p2, axis=-1, keepdims=True)
            probs[hp, n] = (p.astype(BF16), p2.astype(BF16), den, m + jnp.log(den))

    for hp in range(2):
        o_pair = jnp.zeros((rows, PAIR_WIDTH), F32)
        lse_pair = jnp.zeros((rows, PAIR_WIDTH), F32)
        for n in range(nb):
            p, p2, den, lse = probs[hp, n]
            vt = vt_ref[n, 2 * hp:2 * hp + 2].reshape(PAIR_WIDTH, w).astype(BF16)
            o = (lax.dot_general(p, vt, _NT, preferred_element_type=F32) + _dot(p2, vns[hp])) / den
            o = jnp.where(head0, o[:rows], o[rows:])
            lse = jnp.where(head0, jnp.broadcast_to(lse[:rows], (rows, PAIR_WIDTH)),
                            jnp.broadcast_to(lse[rows:], (rows, PAIR_WIDTH)))
            mine = (rowo >= n * steps) & (rowo < (n + 1) * steps)
            o_pair = jnp.where(mine, o, o_pair)
            lse_pair = jnp.where(mine, lse, lse_pair)
        o_ref[hp] = o_pair
        lse_ref[hp] = lse_pair


def _scan_attn_kernel(u_ref, pm_ref, pmt_ref, ab_ref, bre_ref, bim_ref, cre_ref, cim_ref, d_ref,
                      qkv_ref, kt_ref, vt_ref, y_ref, hre_ref, him_ref, o_ref, lse_ref,
                      bur, bui, tab, carry, *, dil, steps):
    _attn_sample_kernel(qkv_ref, kt_ref, vt_ref, o_ref, lse_ref, dil=dil, steps=steps)
    _ssm_prompt_kernel(u_ref, pm_ref, pmt_ref, ab_ref, bre_ref, bim_ref, cre_ref, cim_ref, d_ref,
                       y_ref, hre_ref, him_ref, bur, bui, tab, carry)


def _scan_attn_call(u, ab, bre, bim, cre, cim, d_skip, n_seq, seq,
                    qkv_s, k_cache, v_cache, l, n_dec, steps, dil):
    chunks = seq // SCAN_CHUNK
    w = k_cache.shape[2]
    nb = SAMPLE_SEQ_BLOCK[w]
    rows = nb * steps
    assert n_seq * chunks * nb == n_dec, "sample blocks must match the scan grid"
    kt = k_cache.transpose(0, 1, 3, 4, 2)
    vt = v_cache.transpose(0, 1, 3, 4, 2)
    half = SSM_LANES // 2
    wspec_b = pl.BlockSpec((2, 256, half), lambda n, c: (0, 0, 0))
    wspec_c = pl.BlockSpec((2, half, 256), lambda n, c: (0, 0, 0))
    st_spec = pl.BlockSpec((None, SUBLANES, SSM_LANES), lambda n, c: (n, 0, 0))
    st_shape = jax.ShapeDtypeStruct((n_seq, SUBLANES, SSM_LANES), F32)
    r = jnp.arange(SCAN_CHUNK)
    pm = (r[None, :] == ((r % SUBLANES) * SEG_LEN + r // SUBLANES)[:, None]).astype(BF16)
    perm_spec = pl.BlockSpec((SCAN_CHUNK, SCAN_CHUNK), lambda n, c: (0, 0))
    cache_spec = pl.BlockSpec((None, nb, HEADS, HEAD_DIM, w), lambda n, c: (l, n * chunks + c, 0, 0, 0))
    attn_out = pl.BlockSpec((2, rows, PAIR_WIDTH), lambda n, c: (0, n * chunks + c, 0))
    attn_shape = jax.ShapeDtypeStruct((2, n_dec * steps, PAIR_WIDTH), F32)
    y, hre, him, o, lse = pl.pallas_call(
        functools.partial(_scan_attn_kernel, dil=dil, steps=steps),
        grid=(n_seq, chunks),
        in_specs=[
            pl.BlockSpec((SCAN_CHUNK, SSM_WIDTH), lambda n, c: (n * chunks + c, 0)),
            perm_spec, perm_spec,
            pl.BlockSpec((2, SSM_LANES), lambda n, c: (0, 0)),
            wspec_b, wspec_b, wspec_c, wspec_c,
            pl.BlockSpec((1, SSM_WIDTH), lambda n, c: (0, 0)),
            pl.BlockSpec((3, 2, rows, PAIR_WIDTH), lambda n, c: (0, 0, n * chunks + c, 0)),
            cache_spec, cache_spec,
        ],
        out_specs=[pl.BlockSpec((SCAN_CHUNK, SSM_WIDTH), lambda n, c: (n * chunks + c, 0)),
                   st_spec, st_spec, attn_out, attn_out],
        out_shape=[jax.ShapeDtypeStruct((n_seq * seq, SSM_WIDTH), F32), st_shape, st_shape,
                   attn_shape, attn_shape],
        scratch_shapes=[pltpu.VMEM((SSM_LANES // LANE_TILE, SCAN_CHUNK, LANE_TILE), F32),
                        pltpu.VMEM((SSM_LANES // LANE_TILE, SCAN_CHUNK, LANE_TILE), F32),
                        pltpu.VMEM((8, SUBLANES, SSM_LANES), F32),
                        pltpu.VMEM((2, SUBLANES, SSM_LANES), F32)],
        compiler_params=_cparams(("parallel", "arbitrary")),
        name="scan_prompt_attn_sample",
    )(u, pm, pm.T, ab, bre, bim, cre, cim, d_skip, qkv_s, kt, vt)
    return (y, hre, him), (o, lse)


def _merge_kernel(x_ref, mod_ref, y_ref, o0_ref, l0_ref, o1_ref, l1_ref, o2_ref, l2_ref,
                  gmix_ref, gffn_ref, wmg_ref, bmg_ref, wglu_ref, wbs_ref, wba_ref, wout_ref,
                  wrh_ref, wrl_ref, br_ref, x1_ref, h2_ref, comb_ref):
    x = x_ref[...]
    tm = x.shape[0]
    hb = (_rmsnorm(x, gmix_ref[...]) * (1.0 + mod_ref[1]) + mod_ref[0]).astype(BF16)

    y = _gelu_tanh(y_ref[...])
    yg = y * _sigmoid(_dot(y.astype(BF16), wglu_ref[...]))
    ssm_branch = _dot(yg.astype(BF16), wbs_ref[...])

    pairs = []
    for hp in range(2):
        l0, l1, l2 = l0_ref[hp], l1_ref[hp], l2_ref[hp]
        lm = jnp.maximum(jnp.maximum(l0, l1), l2)
        e0, e1, e2 = jnp.exp(l0 - lm), jnp.exp(l1 - lm), jnp.exp(l2 - lm)
        pairs.append((e0 * o0_ref[hp] + e1 * o1_ref[hp] + e2 * o2_ref[hp]) / (e0 + e1 + e2))
    attn = jnp.concatenate(pairs, axis=-1)
    attn_branch = _dot(attn.astype(BF16), wba_ref[...])

    g_ssm = _sigmoid(_dot(hb, wmg_ref[:, :D_MODEL]) + bmg_ref[:, :D_MODEL])
    g_attn = _sigmoid(_dot(hb, wmg_ref[:, D_MODEL:]) + bmg_ref[:, D_MODEL:])
    merged = g_ssm * ssm_branch + g_attn * attn_branch
    x1 = x + mod_ref[2] * _dot(merged.astype(BF16), wout_ref[...])
    x1_ref[...] = x1

    h2 = _rmsnorm(x1, gffn_ref[...]) * (1.0 + mod_ref[4]) + mod_ref[3]
    h2_hi = h2.astype(BF16)
    h2_ref[...] = h2_hi
    h2_lo = (h2 - h2_hi.astype(F32)).astype(BF16)
    logits = (_dot(h2_hi, wrh_ref[...]) + _dot(h2_hi, wrl_ref[...]) + _dot(h2_lo, wrh_ref[...])
              + br_ref[...])

    lane = lax.broadcasted_iota(jnp.int32, (tm, ROUTER_LANES), 1)
    lanef = lane.astype(F32)
    low = jnp.float32(-3.0e38)
    is_grp = (lane >= GROUP_LOGIT_LANE0) & (lane < GROUP_LOGIT_LANE0 + N_EXPERT_GROUPS)
    lg = jnp.where(is_grp, logits, low)
    mg = jnp.max(lg, axis=-1, keepdims=True)
    gsel = jnp.min(jnp.where(lg == mg, lanef - GROUP_LOGIT_LANE0, 1e9), axis=-1, keepdims=True)
    pg_sel = 1.0 / jnp.sum(jnp.exp(lg - mg), axis=-1, keepdims=True)
    in_grp = (lane < N_EXPERTS) & ((lane // EXPERTS_PER_GROUP).astype(F32) == gsel)
    le = jnp.where(in_grp, logits, low)
    m1 = jnp.max(le, axis=-1, keepdims=True)
    i1 = jnp.min(jnp.where(le == m1, lanef, 1e9), axis=-1, keepdims=True)
    le2 = jnp.where(lanef == i1, low, le)
    m2 = jnp.max(le2, axis=-1, keepdims=True)
    i2 = jnp.min(jnp.where(le2 == m2, lanef, 1e9), axis=-1, keepdims=True)
    r = jnp.exp(m2 - m1)
    w1 = 1.0 / (1.0 + r)
    w2 = r / (1.0 + r)
    comb = pg_sel * (jnp.where(lanef == i1, w1, 0.0) + jnp.where(lanef == i2, w2, 0.0))
    comb_ref[...] = comb + jnp.where(lane == GROUP_ID_LANE, gsel, 0.0)


def _merge_call(x, mod, y, attn_outs, p, l, tm, tiles_per_seq):
    ntok = x.shape[0]
    tok = lambda w: pl.BlockSpec((tm, w), lambda i: (i, 0))
    lay = lambda *shape: pl.BlockSpec((None,) + shape, lambda i: (l,) + (0,) * len(shape))
    attn_specs = [pl.BlockSpec((2, tm, PAIR_WIDTH), lambda i: (0, i, 0))] * 6
    return pl.pallas_call(
        _merge_kernel,
        grid=(ntok // tm,),
        in_specs=[tok(D_MODEL), _mod_spec(mod, l, tm, tiles_per_seq), tok(SSM_WIDTH)] + attn_specs + [
            lay(1, D_MODEL), lay(1, D_MODEL),
            lay(D_MODEL, 2 * D_MODEL), lay(1, 2 * D_MODEL),
            lay(SSM_WIDTH, SSM_WIDTH), lay(SSM_WIDTH, D_MODEL), lay(GROUP_WIDTH, D_MODEL),
            lay(D_MODEL, D_MODEL),
            lay(D_MODEL, ROUTER_LANES), lay(D_MODEL, ROUTER_LANES), lay(1, ROUTER_LANES),
        ],
        out_specs=[tok(D_MODEL), tok(D_MODEL), tok(ROUTER_LANES)],
        out_shape=[jax.ShapeDtypeStruct((ntok, D_MODEL), F32),
                   jax.ShapeDtypeStruct((ntok, D_MODEL), BF16),
                   jax.ShapeDtypeStruct((ntok, ROUTER_LANES), F32)],
        compiler_params=_cparams(("parallel",)),
        name="merge_router",
    )(x, mod, y, *attn_outs, p['g_mix'], p['g_ffn'], p['w_mgate'], p['b_mgate'], p['w_glu'],
      p['w_br_ssm'], p['w_br_attn'], p['w_out'], p['w_router_hi'], p['w_router_lo'], p['b_router'])


MOE_EXPERT_BLOCK = 4
MOE_ROW_CHUNK = 128
GROUP_ID_LANE = 64
_TN = (((0,), (0,)), ((), ()))


def _moe_kernel(h2_ref, comb_ref, x1_ref, mod_ref, wg_ref, wu_ref, wd_ref, gfin_ref, out_ref,
                perm, earlier, hs, combs, ys, bounds, *, final):
    s = pl.program_id(1)
    tm = h2_ref.shape[0]
    ch = MOE_ROW_CHUNK

    @pl.when(s == 0)
    def _():
        comb = comb_ref[...]
        lane = lax.broadcasted_iota(jnp.int32, (tm, ROUTER_LANES), 1)
        gid = jnp.sum(jnp.where(lane == GROUP_ID_LANE, comb, 0.0), axis=-1, keepdims=True)
        onehot = jnp.where(lane.astype(F32) == gid, 1.0, 0.0)
        col = lax.broadcasted_iota(jnp.int32, (tm, tm), 1)

        @pl.when(pl.program_id(0) == 0)
        def _():
            row = lax.broadcasted_iota(jnp.int32, (tm, tm), 0)
            earlier[...] = jnp.where(col < row, 1.0, 0.0).astype(BF16)

        rank = _dot(earlier[...], onehot.astype(BF16))
        tot = jnp.sum(onehot, axis=0, keepdims=True)
        lane1 = lax.broadcasted_iota(jnp.int32, (1, ROUTER_LANES), 1)
        start = jnp.zeros((1, ROUTER_LANES), F32)
        run = jnp.zeros((1, 1), F32)
        for g in range(N_EXPERT_GROUPS):
            bounds[2 * g] = run[0, 0].astype(jnp.int32)
            start = start + jnp.where(lane1 == g, run, 0.0)
            run = run + jnp.sum(jnp.where(lane1 == g, tot, 0.0), axis=-1, keepdims=True)
            bounds[2 * g + 1] = run[0, 0].astype(jnp.int32)
        pos = jnp.sum(onehot * (rank + start), axis=-1, keepdims=True)
        perm[...] = jnp.where(col.astype(F32) == pos, 1.0, 0.0).astype(BF16)
        hs[...] = lax.dot_general(perm[...], h2_ref[...], _TN, preferred_element_type=F32).astype(BF16)
        c_hi = comb.astype(BF16).astype(F32)
        rest = comb - c_hi
        c_mid = rest.astype(BF16).astype(F32)
        c_lo = rest - c_mid
        packed = jnp.where(lane < 2 * N_EXPERTS, c_hi + pltpu.roll(c_mid, N_EXPERTS, axis=1),
                           jnp.where(lane >= 3 * N_EXPERTS, pltpu.roll(c_lo, 3 * N_EXPERTS, axis=1), c_hi))
        got = lax.dot_general(perm[...], packed.astype(BF16), _TN, preferred_element_type=F32)
        mid = pltpu.roll(jnp.where((lane >= N_EXPERTS) & (lane < 2 * N_EXPERTS), got, 0.0),
                         ROUTER_LANES - N_EXPERTS, axis=1)
        low = pltpu.roll(jnp.where(lane >= 3 * N_EXPERTS, got, 0.0), ROUTER_LANES - 3 * N_EXPERTS, axis=1)
        keep = (lane < N_EXPERTS) | (lane == GROUP_ID_LANE)
        combs[...] = (jnp.where(keep, got, 0.0) + mid) + low
        ys[...] = jnp.zeros_like(ys)

    g = s // (EXPERTS_PER_GROUP // MOE_EXPERT_BLOCK)
    lo = bounds[2 * g] // ch
    hi = (bounds[2 * g + 1] + (ch - 1)) // ch
    lane = lax.broadcasted_iota(jnp.int32, (ch, ROUTER_LANES), 1)

    def chunk(c):
        r0 = pl.multiple_of(c * ch, ch)
        h = hs[pl.ds(r0, ch), :]
        comb = combs[pl.ds(r0, ch), :]
        hids = []
        for j in range(MOE_EXPERT_BLOCK):
            gt = _dot(h, wg_ref[j])
            w = jnp.sum(jnp.where(lane == s * MOE_EXPERT_BLOCK + j, comb, 0.0), axis=-1, keepdims=True)
            hids.append((gt * _sigmoid(gt) * _dot(h, wu_ref[j]) * w).astype(BF16))
        ys[pl.ds(r0, ch), :] += _dot(jnp.concatenate(hids, axis=-1), wd_ref[...])

    def pair(i, carry):
        chunk(lo + 2 * i)
        chunk(lo + 2 * i + 1)
        return carry

    n_chunks = hi - lo
    lax.fori_loop(0, n_chunks // 2, pair, 0)

    @pl.when(n_chunks % 2 == 1)
    def _():
        chunk(hi - 1)

    @pl.when(s == pl.num_programs(1) - 1)
    def _():
        x2 = x1_ref[...] + mod_ref[...] * _dot(perm[...], ys[...].astype(BF16))
        out_ref[...] = _rmsnorm(x2, gfin_ref[...]) if final else x2


def _moe_call(h2, comb, x1, mod, p, l, tm, tiles_per_seq, final):
    ntok = h2.shape[0]
    eb = MOE_EXPERT_BLOCK
    tok = lambda w: pl.BlockSpec((tm, w), lambda i, s: (i, 0))
    gate_piece = 5
    if mod.ndim == 5:
        mod_spec = pl.BlockSpec((None, None, None, 1, D_MODEL),
                                lambda i, s: (l, gate_piece, i // tiles_per_seq, 0, 0))
    else:
        mod_spec = pl.BlockSpec((None, None, tm, D_MODEL), lambda i, s: (l, gate_piece, i, 0))
    w_down = p['w_exp_down'].reshape(DEPTH, N_EXPERTS // eb, eb * EXPERT_FF, D_MODEL)
    return pl.pallas_call(
        functools.partial(_moe_kernel, final=final),
        grid=(ntok // tm, N_EXPERTS // eb),
        in_specs=[tok(D_MODEL), tok(ROUTER_LANES), tok(D_MODEL), mod_spec,
                  pl.BlockSpec((None, eb, D_MODEL, EXPERT_FF), lambda i, s: (l, s, 0, 0)),
                  pl.BlockSpec((None, eb, D_MODEL, EXPERT_FF), lambda i, s: (l, s, 0, 0)),
                  pl.BlockSpec((None, None, eb * EXPERT_FF, D_MODEL), lambda i, s: (l, s, 0, 0)),
                  pl.BlockSpec((1, D_MODEL), lambda i, s: (0, 0))],
        out_specs=tok(D_MODEL),
        out_shape=jax.ShapeDtypeStruct((ntok, D_MODEL), F32),
        scratch_shapes=[pltpu.VMEM((tm, tm), BF16), pltpu.VMEM((tm, tm), BF16), pltpu.VMEM((tm, D_MODEL), BF16),
                        pltpu.VMEM((tm, ROUTER_LANES), F32), pltpu.VMEM((tm, D_MODEL), F32),
                        pltpu.SMEM((2 * N_EXPERT_GROUPS,), jnp.int32)],
        compiler_params=_cparams(("arbitrary", "arbitrary")),
        name="moe_experts",
    )(h2, comb, x1, mod, p['w_exp_gate'], p['w_exp_up'], w_down, p['g_final'])


def _kv_state(qkv, n_seq, seq, prompt):
    kv = []
    for g in range(N_GROUPS):
        keep = min(WINDOWS[g], seq)
        if prompt:
            tails = _kv_tail_call(qkv[g], n_seq, seq, keep).reshape(2, n_seq, HEADS, HEAD_DIM, keep)
            kv += [tails[0].transpose(0, 3, 1, 2), tails[1].transpose(0, 3, 1, 2)]
        else:
            for a in (1, 2):
                new = qkv[g][a].reshape(2, n_seq, seq, 2, HEAD_DIM)
                kv.append(new.transpose(1, 2, 0, 3, 4).reshape(n_seq, seq, HEADS, HEAD_DIM))
    return kv


def _run_trunks(xp, xs, mod_p, mod_s, p, ssm, batch, seq, dec_batch, dec_seq, caches, h0):
    tm_p, tm_s = 512, 256
    tm_moe_p, tm_moe_s = 1024, 512
    wide = N_GROUPS - 1
    states_p, states_s = [], []
    for l in range(DEPTH):
        ab, bre, bim, cre, cim, d_skip = ssm[l]
        up, *qkv_p = _inproj_call(xp, mod_p, p['g_mix'], p['w_in'], l, tm_p, seq // tm_p)
        us, *qkv_s = _inproj_call(xs, mod_s, p['g_mix'], p['w_in'], l, tm_s, 1)

        (yp, hre_p, him_p), attn_wide = _scan_attn_call(
            up, ab, bre, bim, cre, cim, d_skip, batch, seq,
            qkv_s[wide], caches[2 * wide], caches[2 * wide + 1], l, dec_batch, dec_seq, DILATIONS[wide])
        attn_p, attn_s = [], []
        for g in range(N_GROUPS - 1):
            o_p, o_s = _attn_prompt_call(qkv_p[g], batch, seq, DILATIONS[g],
                                         sample=(qkv_s[g], caches[2 * g], caches[2 * g + 1], l, dec_batch, dec_seq))
            attn_p += list(o_p)
            attn_s += list(o_s)
        attn_p += list(_attn_prompt_call(qkv_p[wide], batch, seq, DILATIONS[wide]))
        attn_s += list(attn_wide)
        x1, h2, comb = _merge_call(xp, mod_p, yp, attn_p, p, l, tm_p, seq // tm_p)
        xp = _moe_call(h2, comb, x1, mod_p, p, l, tm_moe_p, seq // tm_moe_p, final=(l == DEPTH - 1))
        states_p.append(_kv_state(qkv_p, batch, seq, True)
                        + [hre_p[:, 0].reshape(batch, N_SSM_GROUPS, SSM_STATE),
                           him_p[:, 0].reshape(batch, N_SSM_GROUPS, SSM_STATE)])

        ys, hre_s, him_s = _ssm_sample_call(us.reshape(dec_batch, dec_seq * SSM_WIDTH), ab, bre, bim, cre, cim,
                                            d_skip, h0[0][l], h0[1][l], dec_seq)
        x1, h2, comb = _merge_call(xs, mod_s, ys.reshape(dec_batch * dec_seq, SSM_WIDTH), attn_s, p, l, tm_s, 1)
        xs = _moe_call(h2, comb, x1, mod_s, p, l, tm_moe_s, 1, final=(l == DEPTH - 1))
        states_s.append(_kv_state(qkv_s, dec_batch, dec_seq, False)
                        + [hre_s.reshape(dec_batch, N_SSM_GROUPS, SSM_STATE),
                           him_s.reshape(dec_batch, N_SSM_GROUPS, SSM_STATE)])
    stack = lambda states: [jnp.stack([states[l][i] for l in range(DEPTH)], axis=0)
                            for i in range(len(states[0]))]
    return xp, xs, stack(states_p), stack(states_s)


def kernel(x_prompt, x_sample, cache_k_w128, cache_v_w128, cache_k_w512, cache_v_w512,
           cache_k_w2048, cache_v_w2048, state_ssm_re, state_ssm_im, c_prompt, c_sample,
           w_ada, b_ada, g_norm_mix, g_norm_ffn, g_final, w_in, ssm_a_re, ssm_a_im, ssm_b_re,
           ssm_b_im, ssm_c_re, ssm_c_im, ssm_log_step, ssm_d, w_glu, w_br_ssm, w_br_attn,
           w_mgate, b_mgate, w_out, w_router_grp, b_router_grp, w_router_exp, b_router_exp,
           w_exp_gate, w_exp_up, w_exp_down):
    batch, seq, _ = x_prompt.shape
    dec_batch, dec_seq, _ = x_sample.shape

    pad = -batch % SUBLANES
    c_p = jnp.concatenate([c_prompt, jnp.zeros((pad, D_MODEL), F32)], axis=0)
    mod_p, mod_s = _ada_call(c_p, jnp.repeat(c_sample, dec_seq, axis=0), w_ada, b_ada)
    mod_prompt = mod_p.reshape(DEPTH, 6, batch + pad, 1, D_MODEL)
    mod_sample = mod_s

    abr, abi, bbr, bbi = _ssm_param_call(ssm_a_re, ssm_a_im, ssm_log_step, ssm_b_re, ssm_b_im)
    ssm = []
    for l in range(DEPTH):
        pick = lambda a: a[l].reshape(N_SSM_GROUPS, SSM_STATE, SSM_GROUP_CH)
        ab = jnp.stack([pick(abr)[:, :, 0].reshape(SSM_LANES), pick(abi)[:, :, 0].reshape(SSM_LANES)])
        bre = _block_diag_halves(pick(bbr).transpose(0, 2, 1), BF16)
        bim = _block_diag_halves(pick(bbi).transpose(0, 2, 1), BF16)
        cre = _block_diag_halves(ssm_c_re[l].transpose(0, 2, 1), BF16)
        cim = _block_diag_halves(ssm_c_im[l].transpose(0, 2, 1), BF16)
        ssm.append((ab, bre, bim, cre, cim, ssm_d[l].reshape(1, SSM_WIDTH)))

    w_router = jnp.concatenate(
        [w_router_exp, w_router_grp,
         jnp.zeros((DEPTH, D_MODEL, ROUTER_LANES - N_EXPERTS - N_EXPERT_GROUPS), F32)], axis=-1)
    b_router = jnp.concatenate(
        [b_router_exp, b_router_grp,
         jnp.zeros((DEPTH, ROUTER_LANES - N_EXPERTS - N_EXPERT_GROUPS), F32)], axis=-1)
    w_router_hi = w_router.astype(BF16)
    p = dict(
        g_mix=g_norm_mix.reshape(DEPTH, 1, D_MODEL), g_ffn=g_norm_ffn.reshape(DEPTH, 1, D_MODEL),
        g_final=g_final.reshape(1, D_MODEL), w_in=w_in.astype(BF16), w_glu=w_glu.astype(BF16),
        w_br_ssm=w_br_ssm.astype(BF16), w_br_attn=w_br_attn.astype(BF16),
        w_mgate=w_mgate.astype(BF16), b_mgate=b_mgate.reshape(DEPTH, 1, 2 * D_MODEL),
        w_out=w_out.astype(BF16), w_router_hi=w_router_hi,
        w_router_lo=(w_router - w_router_hi.astype(F32)).astype(BF16),
        b_router=b_router.reshape(DEPTH, 1, ROUTER_LANES),
        w_exp_gate=w_exp_gate.astype(BF16), w_exp_up=w_exp_up.astype(BF16),
        w_exp_down=w_exp_down.astype(BF16))

    caches = [cache_k_w128, cache_v_w128, cache_k_w512, cache_v_w512, cache_k_w2048, cache_v_w2048]
    h0 = (state_ssm_re.reshape(DEPTH, dec_batch, SSM_LANES), state_ssm_im.reshape(DEPTH, dec_batch, SSM_LANES))
    y_prompt, y_sample, pstate, sstate = _run_trunks(
        x_prompt.reshape(batch * seq, D_MODEL), x_sample.reshape(dec_batch * dec_seq, D_MODEL),
        mod_prompt, mod_sample, p, ssm, batch, seq, dec_batch, dec_seq, caches, h0)
    return (y_prompt.reshape(batch, seq, D_MODEL), y_sample.reshape(dec_batch, dec_seq, D_MODEL),
            *pstate, *sstate)
```

```python
import functools
import math

import jax
import jax.numpy as jnp
from jax import lax
from jax.experimental import pallas as pl
from jax.experimental.pallas import tpu as pltpu

F32 = jnp.float32
BF16 = jnp.bfloat16

D_MODEL = 1024
DEPTH = 2
SSM_WIDTH = 512
SSM_GROUP_CH = 16
N_SSM_GROUPS = 32
SSM_STATE = 64
SSM_LANES = N_SSM_GROUPS * SSM_STATE
HEAD_DIM = 64
HEADS = 4
GROUP_WIDTH = HEADS * HEAD_DIM
PAIR_WIDTH = 2 * HEAD_DIM
WINDOWS = (128, 512, 2048)
DILATIONS = (1, 4, 16)
N_BACK = 128
N_GROUPS = 3
QKV_WIDTH = N_GROUPS * GROUP_WIDTH
IN_PROJ_WIDTH = SSM_WIDTH + 3 * QKV_WIDTH
N_EXPERT_GROUPS = 4
EXPERTS_PER_GROUP = 8
N_EXPERTS = 32
EXPERT_FF = 256
RMS_EPS = 1e-6
NEG_INF = -1e30
ROUTER_LANES = 128
GROUP_LOGIT_LANE0 = N_EXPERTS

SUBLANES = 8
VMEM_LIMIT = 56 * 1024 * 1024


def _cparams(sem):
    return pltpu.CompilerParams(dimension_semantics=sem, vmem_limit_bytes=VMEM_LIMIT)


def _sigmoid(x):
    return 1.0 / (1.0 + jnp.exp(-x))


def _gelu_tanh(x):
    return 0.5 * x * (1.0 + jnp.tanh(math.sqrt(2.0 / math.pi) * (x + 0.044715 * (x * x * x))))


def _dot(a, b):
    return jnp.dot(a, b, preferred_element_type=F32)


def _rmsnorm(x, g):
    ms = jnp.mean(x * x, axis=-1, keepdims=True)
    return x * lax.rsqrt(ms + RMS_EPS) * g


def _ada_kernel(cp_ref, cs_ref, w_ref, b_ref, op_ref, os_ref):
    w = w_ref[...].astype(BF16)
    for c_ref, o_ref in ((cp_ref, op_ref), (cs_ref, os_ref)):
        c = c_ref[...]
        o_ref[...] = _dot((c * _sigmoid(c)).astype(BF16), w) + b_ref[...]


def _ada_call(c_prompt_rows, c_sample_rows, w_ada, b_ada):
    rp, rs = c_prompt_rows.shape[0], c_sample_rows.shape[0]
    out = lambda r: pl.BlockSpec((None, None, r, D_MODEL), lambda l, j: (l, j, 0, 0))
    return pl.pallas_call(
        _ada_kernel,
        grid=(DEPTH, 6),
        in_specs=[
            pl.BlockSpec((rp, D_MODEL), lambda l, j: (0, 0)),
            pl.BlockSpec((rs, D_MODEL), lambda l, j: (0, 0)),
            pl.BlockSpec((None, D_MODEL, D_MODEL), lambda l, j: (l, 0, j)),
            pl.BlockSpec((None, 1, D_MODEL), lambda l, j: (l, 0, j)),
        ],
        out_specs=[out(rp), out(rs)],
        out_shape=[jax.ShapeDtypeStruct((DEPTH, 6, rp, D_MODEL), F32),
                   jax.ShapeDtypeStruct((DEPTH, 6, rs, D_MODEL), F32)],
        compiler_params=_cparams(("parallel", "parallel")),
        name="ada_mod",
    )(c_prompt_rows, c_sample_rows, w_ada, b_ada.reshape(DEPTH, 1, 6 * D_MODEL))


def _ssm_param_kernel(ar_ref, ai_ref, ls_ref, br_ref, bi_ref, abr_ref, abi_ref, bbr_ref, bbi_ref):
    lr, li = ar_ref[...], ai_ref[...]
    dt = jnp.exp(ls_ref[...])
    mag = jnp.exp(lr * dt)
    abr = mag * jnp.cos(li * dt)
    abi = mag * jnp.sin(li * dt)
    den = lr * lr + li * li
    er, ei = abr - 1.0, abi
    fr = (er * lr + ei * li) / den
    fi = (ei * lr - er * li) / den
    br, bi = br_ref[...], bi_ref[...]
    abr_ref[...] = abr
    abi_ref[...] = abi
    bbr_ref[...] = fr * br - fi * bi
    bbi_ref[...] = fr * bi + fi * br


def _ssm_param_call(a_re, a_im, log_step, b_re, b_im):
    w = SSM_STATE * SSM_GROUP_CH
    rep = lambda a: jnp.repeat(a, SSM_GROUP_CH, axis=-1)
    big = pl.BlockSpec((None, N_SSM_GROUPS, w), lambda l: (l, 0, 0))
    shp = jax.ShapeDtypeStruct((DEPTH, N_SSM_GROUPS, w), F32)
    return pl.pallas_call(
        _ssm_param_kernel,
        grid=(DEPTH,),
        in_specs=[big, big, pl.BlockSpec((None, N_SSM_GROUPS, 1), lambda l: (l, 0, 0)), big, big],
        out_specs=[big, big, big, big],
        out_shape=[shp, shp, shp, shp],
        compiler_params=_cparams(("parallel",)),
        name="ssm_params",
    )(rep(a_re), rep(a_im), log_step.reshape(DEPTH, N_SSM_GROUPS, 1),
      b_re.reshape(DEPTH, N_SSM_GROUPS, w), b_im.reshape(DEPTH, N_SSM_GROUPS, w))


def _block_diag_halves(blocks, dtype):
    g, r, c = blocks.shape
    half = g // 2
    eye = jnp.eye(half, dtype=bool)

    def one(b):
        m = jnp.where(eye[:, None, :, None], b[:, :, None, :], 0.0)
        return m.reshape(half * r, half * c)

    return jnp.stack([one(blocks[:half]), one(blocks[half:])]).astype(dtype)


def _inproj_kernel(x_ref, mod_ref, g_ref, w_ref, u_ref, a0_ref, a1_ref, a2_ref):
    h = _rmsnorm(x_ref[...], g_ref[...]) * (1.0 + mod_ref[1]) + mod_ref[0]
    hb = h.astype(BF16)
    u_ref[...] = _dot(hb, w_ref[:, :SSM_WIDTH])
    for g, ref in enumerate((a0_ref, a1_ref, a2_ref)):
        for a in range(3):
            c0 = SSM_WIDTH + a * QKV_WIDTH + g * GROUP_WIDTH
            val = _dot(hb, w_ref[:, c0:c0 + GROUP_WIDTH])
            ref[a, 0] = val[:, :PAIR_WIDTH]
            ref[a, 1] = val[:, PAIR_WIDTH:]


def _mod_spec(mod, l, tm, tiles_per_seq):
    if mod.ndim == 5:
        return pl.BlockSpec((None, 6, None, 1, D_MODEL), lambda i: (l, 0, i // tiles_per_seq, 0, 0))
    return pl.BlockSpec((None, 6, tm, D_MODEL), lambda i: (l, 0, i, 0))


def _inproj_call(x, mod, g_mix, w_in, l, tm, tiles_per_seq):
    ntok = x.shape[0]
    grp = jax.ShapeDtypeStruct((3, 2, ntok, PAIR_WIDTH), F32)
    grp_spec = pl.BlockSpec((3, 2, tm, PAIR_WIDTH), lambda i: (0, 0, i, 0))
    return pl.pallas_call(
        _inproj_kernel,
        grid=(ntok // tm,),
        in_specs=[
            pl.BlockSpec((tm, D_MODEL), lambda i: (i, 0)),
            _mod_spec(mod, l, tm, tiles_per_seq),
            pl.BlockSpec((None, 1, D_MODEL), lambda i: (l, 0, 0)),
            pl.BlockSpec((None, D_MODEL, IN_PROJ_WIDTH), lambda i: (l, 0, 0)),
        ],
        out_specs=[pl.BlockSpec((tm, SSM_WIDTH), lambda i: (i, 0)), grp_spec, grp_spec, grp_spec],
        out_shape=[jax.ShapeDtypeStruct((ntok, SSM_WIDTH), F32), grp, grp, grp],
        compiler_params=_cparams(("parallel",)),
        name="in_proj",
    )(x, mod, g_mix, w_in)


LANE_TILE = 128
SCAN_CHUNK = 256
SEG_LEN = SCAN_CHUNK // SUBLANES
SCAN_TILE_GROUP = 4


def _cmul(ar, ai, br, bi):
    return ar * br - ai * bi, ar * bi + ai * br


def _ssm_prompt_kernel(u_ref, pm_ref, pmt_ref, ab_ref, bre_ref, bim_ref, cre_ref, cim_ref, d_ref,
                       y_ref, hre_ref, him_ref, bur, bui, tab, carry):
    c = pl.program_id(1)
    half = SSM_LANES // 2
    tiles_per_half = half // LANE_TILE

    @pl.when(c == 0)
    def _():
        shp = (SUBLANES, SSM_LANES)
        sub = lax.broadcasted_iota(jnp.int32, shp, 0)
        b_r = jnp.broadcast_to(ab_ref[0:1, :], shp)
        b_i = jnp.broadcast_to(ab_ref[1:2, :], shp)
        for _ in range(SEG_LEN.bit_length() - 1):
            b_r, b_i = _cmul(b_r, b_i, b_r, b_i)
        p_r, p_i = b_r, b_i
        ps_r, ps_i = jnp.ones(shp, F32), jnp.zeros(shp, F32)
        for s in range(1, SUBLANES):
            if s in (1, 2, 4):
                k = {1: 0, 2: 2, 4: 4}[s]
                tab[k] = jnp.where(sub >= s, p_r, 0.0)
                tab[k + 1] = jnp.where(sub >= s, p_i, 0.0)
            ps_r = jnp.where(sub >= s, p_r, ps_r)
            ps_i = jnp.where(sub >= s, p_i, ps_i)
            p_r, p_i = _cmul(p_r, p_i, b_r, b_i)
        tab[6] = ps_r
        tab[7] = ps_i
        carry[...] = jnp.zeros_like(carry)

    u = u_ref[...]
    ub = _dot(pm_ref[...], u.astype(BF16)).astype(BF16)
    for h in range(2):
        uh = ub[:, h * 256:(h + 1) * 256]
        res_r = _dot(uh, bre_ref[h])
        res_i = _dot(uh, bim_ref[h])
        for j in range(tiles_per_half):
            bur[h * tiles_per_half + j] = res_r[:, j * LANE_TILE:(j + 1) * LANE_TILE]
            bui[h * tiles_per_half + j] = res_i[:, j * LANE_TILE:(j + 1) * LANE_TILE]

    vshape = (SUBLANES, LANE_TILE)
    sub1 = lax.broadcasted_iota(jnp.int32, vshape, 0) >= 1
    for g0 in range(0, SSM_LANES // LANE_TILE, SCAN_TILE_GROUP):
        tiles = list(range(g0, g0 + SCAN_TILE_GROUP))
        lanes = [slice(lt * LANE_TILE, (lt + 1) * LANE_TILE) for lt in tiles]
        a_r = [jnp.broadcast_to(ab_ref[0:1, sl], vshape) for sl in lanes]
        a_i = [jnp.broadcast_to(ab_ref[1:2, sl], vshape) for sl in lanes]

        def step(t, hs, store):
            out = []
            for j, lt in enumerate(tiles):
                rows = pl.ds(pl.multiple_of(t * SUBLANES, SUBLANES), SUBLANES)
                hr, hi = hs[2 * j], hs[2 * j + 1]
                nr = (a_r[j] * hr - a_i[j] * hi) + bur[lt, rows, :]
                ni = (a_r[j] * hi + a_i[j] * hr) + bui[lt, rows, :]
                if store:
                    bur[lt, rows, :] = nr
                    bui[lt, rows, :] = ni
                out += [nr, ni]
            return tuple(out)

        zeros = tuple(jnp.zeros(vshape, F32) for _ in range(2 * SCAN_TILE_GROUP))
        ends = lax.fori_loop(0, SEG_LEN, lambda t, hs: step(t, hs, False), zeros, unroll=True)

        starts = []
        for j, sl in enumerate(lanes):
            fr, fi = ends[2 * j], ends[2 * j + 1]
            for k in (1, 2, 4):
                ti = {1: 0, 2: 2, 4: 4}[k]
                mr, mi = tab[ti, :, sl], tab[ti + 1, :, sl]
                rr = pltpu.roll(fr, k, axis=0)
                ri = pltpu.roll(fi, k, axis=0)
                fr, fi = fr + (mr * rr - mi * ri), fi + (mr * ri + mi * rr)
            cr, ci = carry[0, :, sl], carry[1, :, sl]
            ps_r, ps_i = tab[6, :, sl], tab[7, :, sl]
            sr = (ps_r * cr - ps_i * ci) + jnp.where(sub1, pltpu.roll(fr, 1, axis=0), 0.0)
            si = (ps_r * ci + ps_i * cr) + jnp.where(sub1, pltpu.roll(fi, 1, axis=0), 0.0)
            starts += [sr, si]

        last = lax.fori_loop(0, SEG_LEN, lambda t, hs: step(t, hs, True), tuple(starts), unroll=True)
        for j, sl in enumerate(lanes):
            carry[0, :, sl] = jnp.broadcast_to(last[2 * j][SUBLANES - 1:SUBLANES, :], vshape)
            carry[1, :, sl] = jnp.broadcast_to(last[2 * j + 1][SUBLANES - 1:SUBLANES, :], vshape)

    for h in range(2):
        cs = slice(h * 256, (h + 1) * 256)
        tl = range(h * tiles_per_half, (h + 1) * tiles_per_half)
        hr = jnp.concatenate([bur[lt] for lt in tl], axis=-1).astype(BF16)
        hi = jnp.concatenate([bui[lt] for lt in tl], axis=-1).astype(BF16)
        y = _dot(hr, cre_ref[h]) - _dot(hi, cim_ref[h])
        y_hi = y.astype(BF16)
        y_lo = (y - y_hi.astype(F32)).astype(BF16)
        y = _dot(pmt_ref[...], y_hi) + _dot(pmt_ref[...], y_lo)
        y_ref[:, cs] = y + d_ref[:, cs] * u[:, cs]

    @pl.when(c == pl.num_programs(1) - 1)
    def _():
        hre_ref[...] = carry[0]
        him_ref[...] = carry[1]


def _ssm_sample_kernel(u_ref, ab_ref, bre_ref, bim_ref, cre_ref, cim_ref, d_ref, h0r_ref, h0i_ref,
                       y_ref, hr_ref, hi_ref, *, steps):
    half = SSM_LANES // 2
    ar, ai = ab_ref[0:1, :], ab_ref[1:2, :]
    hr_ref[...] = h0r_ref[...]
    hi_ref[...] = h0i_ref[...]
    for t in range(steps):
        u = u_ref[:, t * SSM_WIDTH:(t + 1) * SSM_WIDTH]
        ub = u.astype(BF16)
        for h in range(2):
            hs = slice(h * half, (h + 1) * half)
            cs = slice(h * 256, (h + 1) * 256)
            uh = ub[:, cs]
            hr, hi = hr_ref[:, hs], hi_ref[:, hs]
            a_r, a_i = ar[:, hs], ai[:, hs]
            nr = (a_r * hr - a_i * hi) + _dot(uh, bre_ref[h])
            ni = (a_r * hi + a_i * hr) + _dot(uh, bim_ref[h])
            hr_ref[:, hs] = nr
            hi_ref[:, hs] = ni
            y = _dot(nr.astype(BF16), cre_ref[h]) - _dot(ni.astype(BF16), cim_ref[h])
            y_ref[:, t * SSM_WIDTH + h * 256:t * SSM_WIDTH + (h + 1) * 256] = y + d_ref[:, cs] * u[:, cs]


def _ssm_sample_call(u, ab, bre, bim, cre, cim, d_skip, h0r, h0i, steps):
    n = u.shape[0]
    st = jax.ShapeDtypeStruct((n, SSM_LANES), F32)
    return pl.pallas_call(
        functools.partial(_ssm_sample_kernel, steps=steps),
        out_shape=[jax.ShapeDtypeStruct((n, steps * SSM_WIDTH), F32), st, st],
        compiler_params=pltpu.CompilerParams(vmem_limit_bytes=VMEM_LIMIT),
        name="ssm_sample",
    )(u, ab, bre, bim, cre, cim, d_skip, h0r, h0i)


_NT = (((1,), (1,)), ((), ()))


def _attn_prompt_kernel(q_ref, kp_ref, kc_ref, vp_ref, vc_ref, o_ref, lse_ref, *, dil, q_blocks):
    b = pl.program_id(2)
    qi = lax.broadcasted_iota(jnp.int32, (N_BACK, 2 * N_BACK), 0)
    kj = lax.broadcasted_iota(jnp.int32, (N_BACK, 2 * N_BACK), 1)
    band = (kj >= qi) & (kj <= qi + N_BACK)
    band_first = band & (kj >= jnp.where(b > 0, 0, N_BACK))
    lane = lax.broadcasted_iota(jnp.int32, (N_BACK, PAIR_WIDTH), 1)
    head0 = lane < HEAD_DIM

    def rows(j, r):
        start = j * N_BACK * dil + r
        return pl.ds(start, N_BACK, stride=dil) if dil > 1 else pl.ds(start, N_BACK)

    def one(j, r):
        cur = rows(j, r)
        q = (q_ref[cur, :] * (HEAD_DIM ** -0.5)).astype(BF16)
        k_prev = kp_ref[rows(0, r), :] if j == 0 else kc_ref[rows(j - 1, r), :]
        v_prev = vp_ref[rows(0, r), :] if j == 0 else vc_ref[rows(j - 1, r), :]
        kk = jnp.concatenate([k_prev, kc_ref[cur, :]], axis=0).astype(BF16)
        vv = jnp.concatenate([v_prev, vc_ref[cur, :]], axis=0).astype(BF16)
        valid = band_first if j == 0 else band
        outs, lses = [], []
        for sub in range(2):
            in_head = head0 if sub == 0 else jnp.logical_not(head0)
            qm = jnp.where(in_head, q, jnp.zeros_like(q))
            s = lax.dot_general(qm, kk, _NT, preferred_element_type=F32)
            s = jnp.where(valid, s, NEG_INF)
            m = jnp.max(s, axis=-1, keepdims=True)
            p = jnp.exp(s - m)
            den = jnp.sum(p, axis=-1, keepdims=True)
            outs.append(_dot(p.astype(BF16), vv) / den)
            lses.append(jnp.broadcast_to(m + jnp.log(den), (N_BACK, PAIR_WIDTH)))
        o_ref[cur, :] = jnp.where(head0, outs[0], outs[1])
        lse_ref[cur, :] = jnp.where(head0, lses[0], lses[1])

    def residue(r, carry):
        for j in range(q_blocks):
            one(j, r)
        return carry

    if dil <= ATTN_UNROLL:
        for r in range(dil):
            residue(r, 0)
    else:
        lax.fori_loop(0, dil, residue, 0, unroll=2 * ATTN_UNROLL)


ATTN_UNROLL = 4
ATTN_Q_BLOCKS = {1: 8, 4: 2, 16: 1}


def _attn_pair_kernel(q_ref, kp_ref, kc_ref, vp_ref, vc_ref, qkv_s_ref, kt_ref, vt_ref,
                      o_ref, lse_ref, o_s_ref, lse_s_ref, *, dil, q_blocks, steps):
    _attn_sample_kernel(qkv_s_ref, kt_ref, vt_ref, o_s_ref, lse_s_ref, dil=dil, steps=steps)
    _attn_prompt_kernel(q_ref, kp_ref, kc_ref, vp_ref, vc_ref, o_ref, lse_ref, dil=dil, q_blocks=q_blocks)


def _attn_prompt_call(qkv, n_seq, seq, dil, sample=None):
    q_blocks = ATTN_Q_BLOCKS[dil]
    prev_rows = N_BACK * dil
    rows = prev_rows * q_blocks
    nbk = seq // rows
    cur = lambda a: pl.BlockSpec((None, None, rows, PAIR_WIDTH), lambda n, hp, b: (a, hp, n * nbk + b, 0))
    prev = lambda a: pl.BlockSpec(
        (None, None, prev_rows, PAIR_WIDTH),
        lambda n, hp, b: (a, hp, jnp.maximum((n * nbk + b) * q_blocks - 1, n * nbk * q_blocks), 0))
    out_spec = pl.BlockSpec((None, rows, PAIR_WIDTH), lambda n, hp, b: (hp, n * nbk + b, 0))
    out_shape = jax.ShapeDtypeStruct((2, n_seq * seq, PAIR_WIDTH), F32)
    in_specs = [cur(0), prev(1), cur(1), prev(2), cur(2)]
    operands = [qkv, qkv, qkv, qkv, qkv]
    out_specs, out_shapes = [out_spec, out_spec], [out_shape, out_shape]
    if sample is None:
        body = functools.partial(_attn_prompt_kernel, dil=dil, q_blocks=q_blocks)
    else:
        qkv_s, k_cache, v_cache, l, n_dec, steps = sample
        w = k_cache.shape[2]
        nb = SAMPLE_SEQ_BLOCK[w]
        assert n_seq * 2 * nbk * nb == n_dec, "sample blocks must match the prompt attention grid"
        step = lambda n, hp, b: (n * 2 + hp) * nbk + b
        cache_spec = pl.BlockSpec((None, nb, HEADS, HEAD_DIM, w), lambda n, hp, b: (l, step(n, hp, b), 0, 0, 0))
        s_out = pl.BlockSpec((2, nb * steps, PAIR_WIDTH), lambda n, hp, b: (0, step(n, hp, b), 0))
        s_shape = jax.ShapeDtypeStruct((2, n_dec * steps, PAIR_WIDTH), F32)
        in_specs += [pl.BlockSpec((3, 2, nb * steps, PAIR_WIDTH), lambda n, hp, b: (0, 0, step(n, hp, b), 0)),
                     cache_spec, cache_spec]
        operands += [qkv_s, k_cache.transpose(0, 1, 3, 4, 2), v_cache.transpose(0, 1, 3, 4, 2)]
        out_specs += [s_out, s_out]
        out_shapes += [s_shape, s_shape]
        body = functools.partial(_attn_pair_kernel, dil=dil, q_blocks=q_blocks, steps=steps)
    outs = pl.pallas_call(
        body,
        grid=(n_seq, 2, nbk),
        in_specs=in_specs,
        out_specs=out_specs,
        out_shape=out_shapes,
        compiler_params=_cparams(("parallel", "parallel", "parallel")),
        name=f"attn_prompt_d{dil}",
    )(*operands)
    return outs if sample is None else (outs[:2], outs[2:])


def _kv_tail_kernel(x_ref, o_ref):
    o_ref[...] = x_ref[...].T


def _kv_tail_call(qkv, n_seq, seq, keep):
    blocks_per_seq = seq // keep
    return pl.pallas_call(
        _kv_tail_kernel,
        grid=(2, n_seq, 2),
        in_specs=[pl.BlockSpec((None, None, keep, PAIR_WIDTH),
                               lambda a, n, hp: (a + 1, hp, (n + 1) * blocks_per_seq - 1, 0))],
        out_specs=pl.BlockSpec((None, None, None, PAIR_WIDTH, keep), lambda a, n, hp: (a, n, hp, 0, 0)),
        out_shape=jax.ShapeDtypeStruct((2, n_seq, 2, PAIR_WIDTH, keep), F32),
        compiler_params=_cparams(("parallel", "parallel", "parallel")),
        name="kv_tail",
    )(qkv)


SAMPLE_SEQ_BLOCK = {128: 4, 512: 4, 2048: 2}


def _attn_sample_kernel(qkv_ref, kt_ref, vt_ref, o_ref, lse_ref, *, dil, steps):
    nb, _, _, w = kt_ref.shape
    rows = nb * steps
    tok = lax.broadcasted_iota(jnp.int32, (2 * rows, w), 0) % rows
    pos = lax.broadcasted_iota(jnp.int32, (2 * rows, w), 1)
    tokn = lax.broadcasted_iota(jnp.int32, (2 * rows, rows), 0) % rows
    coln = lax.broadcasted_iota(jnp.int32, (2 * rows, rows), 1)
    rowo = lax.broadcasted_iota(jnp.int32, (rows, PAIR_WIDTH), 0)
    head0 = lax.broadcasted_iota(jnp.int32, (rows, PAIR_WIDTH), 1) < HEAD_DIM

    vns, s_buf, s_new = [], {}, []
    for hp in range(2):
        q2 = qkv_ref[0, hp] * (HEAD_DIM ** -0.5)
        q_st = jnp.concatenate([jnp.where(head0, q2, 0.0), jnp.where(head0, 0.0, q2)], axis=0).astype(BF16)
        kn = qkv_ref[1, hp].astype(BF16)
        s_new.append(lax.dot_general(q_st, kn, _NT, preferred_element_type=F32))
        vns.append(qkv_ref[2, hp].astype(BF16))
        for n in range(nb):
            kt = kt_ref[n, 2 * hp:2 * hp + 2].reshape(PAIR_WIDTH, w).astype(BF16)
            s_buf[hp, n] = _dot(q_st, kt)

    probs = {}
    for hp in range(2):
        for n in range(nb):
            t = tok - n * steps
            in_buf = (((pos - t) & (dil - 1)) == 0) & (pos >= t)
            tn = tokn - n * steps
            sn = coln - n * steps
            in_new = ((sn >= 0) & (sn <= tn)) if dil == 1 else (sn == tn)
            s = jnp.where(in_buf, s_buf[hp, n], NEG_INF)
            s2 = jnp.where(in_new, s_new[hp], NEG_INF)
            m = jnp.maximum(jnp.max(s, axis=-1, keepdims=True), jnp.max(s2, axis=-1, keepdims=True))
            p = jnp.exp(s - m)
            p2 = jnp.exp(s2 - m)
            den = jnp.sum(p, axis=-1, keepdims=True) + jnp.sum(p2, axis=-1, keepdims=True)
            probs[hp, n] = (p.astype(BF16), p2.astype(BF16), den, m + jnp.log(den))

    for hp in range(2):
        o_pair = jnp.zeros((rows, PAIR_WIDTH), F32)
        lse_pair = jnp.zeros((rows, PAIR_WIDTH), F32)
        for n in range(nb):
            p, p2, den, lse = probs[hp, n]
            vt = vt_ref[n, 2 * hp:2 * hp + 2].reshape(PAIR_WIDTH, w).astype(BF16)
            o = (lax.dot_general(p, vt, _NT, preferred_element_type=F32) + _dot(p2, vns[hp])) / den
            o = jnp.where(head0, o[:rows], o[rows:])
            lse = jnp.where(head0, jnp.broadcast_to(lse[:rows], (rows, PAIR_WIDTH)),
                            jnp.broadcast_to(lse[rows:], (rows, PAIR_WIDTH)))
            mine = (rowo >= n * steps) & (rowo < (n + 1) * steps)
            o_pair = jnp.where(mine, o, o_pair)
            lse_pair = jnp.where(mine, lse, lse_pair)
        o_ref[hp] = o_pair
        lse_ref[hp] = lse_pair


def _scan_attn_kernel(u_ref, pm_ref, pmt_ref, ab_ref, bre_ref, bim_ref, cre_ref, cim_ref, d_ref,
                      qkv_ref, kt_ref, vt_ref, y_ref, hre_ref, him_ref, o_ref, lse_ref,
                      bur, bui, tab, carry, *, dil, steps):
    _attn_sample_kernel(qkv_ref, kt_ref, vt_ref, o_ref, lse_ref, dil=dil, steps=steps)
    _ssm_prompt_kernel(u_ref, pm_ref, pmt_ref, ab_ref, bre_ref, bim_ref, cre_ref, cim_ref, d_ref,
                       y_ref, hre_ref, him_ref, bur, bui, tab, carry)


def _scan_attn_call(u, ab, bre, bim, cre, cim, d_skip, n_seq, seq,
                    qkv_s, k_cache, v_cache, l, n_dec, steps, dil):
    chunks = seq // SCAN_CHUNK
    w = k_cache.shape[2]
    nb = SAMPLE_SEQ_BLOCK[w]
    rows = nb * steps
    assert n_seq * chunks * nb == n_dec, "sample blocks must match the scan grid"
    kt = k_cache.transpose(0, 1, 3, 4, 2)
    vt = v_cache.transpose(0, 1, 3, 4, 2)
    half = SSM_LANES // 2
    wspec_b = pl.BlockSpec((2, 256, half), lambda n, c: (0, 0, 0))
    wspec_c = pl.BlockSpec((2, half, 256), lambda n, c: (0, 0, 0))
    st_spec = pl.BlockSpec((None, SUBLANES, SSM_LANES), lambda n, c: (n, 0, 0))
    st_shape = jax.ShapeDtypeStruct((n_seq, SUBLANES, SSM_LANES), F32)
    r = jnp.arange(SCAN_CHUNK)
    pm = (r[None, :] == ((r % SUBLANES) * SEG_LEN + r // SUBLANES)[:, None]).astype(BF16)
    perm_spec = pl.BlockSpec((SCAN_CHUNK, SCAN_CHUNK), lambda n, c: (0, 0))
    cache_spec = pl.BlockSpec((None, nb, HEADS, HEAD_DIM, w), lambda n, c: (l, n * chunks + c, 0, 0, 0))
    attn_out = pl.BlockSpec((2, rows, PAIR_WIDTH), lambda n, c: (0, n * chunks + c, 0))
    attn_shape = jax.ShapeDtypeStruct((2, n_dec * steps, PAIR_WIDTH), F32)
    y, hre, him, o, lse = pl.pallas_call(
        functools.partial(_scan_attn_kernel, dil=dil, steps=steps),
        grid=(n_seq, chunks),
        in_specs=[
            pl.BlockSpec((SCAN_CHUNK, SSM_WIDTH), lambda n, c: (n * chunks + c, 0)),
            perm_spec, perm_spec,
            pl.BlockSpec((2, SSM_LANES), lambda n, c: (0, 0)),
            wspec_b, wspec_b, wspec_c, wspec_c,
            pl.BlockSpec((1, SSM_WIDTH), lambda n, c: (0, 0)),
            pl.BlockSpec((3, 2, rows, PAIR_WIDTH), lambda n, c: (0, 0, n * chunks + c, 0)),
            cache_spec, cache_spec,
        ],
        out_specs=[pl.BlockSpec((SCAN_CHUNK, SSM_WIDTH), lambda n, c: (n * chunks + c, 0)),
                   st_spec, st_spec, attn_out, attn_out],
        out_shape=[jax.ShapeDtypeStruct((n_seq * seq, SSM_WIDTH), F32), st_shape, st_shape,
                   attn_shape, attn_shape],
        scratch_shapes=[pltpu.VMEM((SSM_LANES // LANE_TILE, SCAN_CHUNK, LANE_TILE), F32),
                        pltpu.VMEM((SSM_LANES // LANE_TILE, SCAN_CHUNK, LANE_TILE), F32),
                        pltpu.VMEM((8, SUBLANES, SSM_LANES), F32),
                        pltpu.VMEM((2, SUBLANES, SSM_LANES), F32)],
        compiler_params=_cparams(("parallel", "arbitrary")),
        name="scan_prompt_attn_sample",
    )(u, pm, pm.T, ab, bre, bim, cre, cim, d_skip, qkv_s, kt, vt)
    return (y, hre, him), (o, lse)


def _merge_kernel(x_ref, mod_ref, y_ref, o0_ref, l0_ref, o1_ref, l1_ref, o2_ref, l2_ref,
                  gmix_ref, gffn_ref, wmg_ref, bmg_ref, wglu_ref, wbs_ref, wba_ref, wout_ref,
                  wrh_ref, wrl_ref, br_ref, x1_ref, h2_ref, comb_ref):
    x = x_ref[...]
    tm = x.shape[0]
    hb = (_rmsnorm(x, gmix_ref[...]) * (1.0 + mod_ref[1]) + mod_ref[0]).astype(BF16)

    y = _gelu_tanh(y_ref[...])
    yg = y * _sigmoid(_dot(y.astype(BF16), wglu_ref[...]))
    ssm_branch = _dot(yg.astype(BF16), wbs_ref[...])

    pairs = []
    for hp in range(2):
        l0, l1, l2 = l0_ref[hp], l1_ref[hp], l2_ref[hp]
        lm = jnp.maximum(jnp.maximum(l0, l1), l2)
        e0, e1, e2 = jnp.exp(l0 - lm), jnp.exp(l1 - lm), jnp.exp(l2 - lm)
        pairs.append((e0 * o0_ref[hp] + e1 * o1_ref[hp] + e2 * o2_ref[hp]) / (e0 + e1 + e2))
    attn = jnp.concatenate(pairs, axis=-1)
    attn_branch = _dot(attn.astype(BF16), wba_ref[...])

    g_ssm = _sigmoid(_dot(hb, wmg_ref[:, :D_MODEL]) + bmg_ref[:, :D_MODEL])
    g_attn = _sigmoid(_dot(hb, wmg_ref[:, D_MODEL:]) + bmg_ref[:, D_MODEL:])
    merged = g_ssm * ssm_branch + g_attn * attn_branch
    x1 = x + mod_ref[2] * _dot(merged.astype(BF16), wout_ref[...])
    x1_ref[...] = x1

    h2 = _rmsnorm(x1, gffn_ref[...]) * (1.0 + mod_ref[4]) + mod_ref[3]
    h2_hi = h2.astype(BF16)
    h2_ref[...] = h2_hi
    h2_lo = (h2 - h2_hi.astype(F32)).astype(BF16)
    both = _dot(h2_hi, wrl_ref[...])
    logits = (both[:, :ROUTER_LANES] + both[:, ROUTER_LANES:]) + _dot(h2_lo, wrh_ref[...]) + br_ref[...]

    lane = lax.broadcasted_iota(jnp.int32, (tm, ROUTER_LANES), 1)
    lanef = lane.astype(F32)
    low = jnp.float32(-3.0e38)
    is_grp = (lane >= GROUP_LOGIT_LANE0) & (lane < GROUP_LOGIT_LANE0 + N_EXPERT_GROUPS)
    lg = jnp.where(is_grp, logits, low)
    mg = jnp.max(lg, axis=-1, keepdims=True)
    gsel = jnp.min(jnp.where(lg == mg, lanef - GROUP_LOGIT_LANE0, 1e9), axis=-1, keepdims=True)
    pg_sel = 1.0 / jnp.sum(jnp.exp(lg - mg), axis=-1, keepdims=True)
    in_grp = (lane < N_EXPERTS) & ((lane // EXPERTS_PER_GROUP).astype(F32) == gsel)
    le = jnp.where(in_grp, logits, low)
    m1 = jnp.max(le, axis=-1, keepdims=True)
    i1 = jnp.min(jnp.where(le == m1, lanef, 1e9), axis=-1, keepdims=True)
    le2 = jnp.where(lanef == i1, low, le)
    m2 = jnp.max(le2, axis=-1, keepdims=True)
    i2 = jnp.min(jnp.where(le2 == m2, lanef, 1e9), axis=-1, keepdims=True)
    r = jnp.exp(m2 - m1)
    w1 = 1.0 / (1.0 + r)
    w2 = r / (1.0 + r)
    comb = pg_sel * (jnp.where(lanef == i1, w1, 0.0) + jnp.where(lanef == i2, w2, 0.0))
    comb_ref[...] = comb + jnp.where(lane == GROUP_ID_LANE, gsel, 0.0)


def _merge_call(x, mod, y, attn_outs, p, l, tm, tiles_per_seq):
    ntok = x.shape[0]
    tok = lambda w: pl.BlockSpec((tm, w), lambda i: (i, 0))
    lay = lambda *shape: pl.BlockSpec((None,) + shape, lambda i: (l,) + (0,) * len(shape))
    attn_specs = [pl.BlockSpec((2, tm, PAIR_WIDTH), lambda i: (0, i, 0))] * 6
    return pl.pallas_call(
        _merge_kernel,
        grid=(ntok // tm,),
        in_specs=[tok(D_MODEL), _mod_spec(mod, l, tm, tiles_per_seq), tok(SSM_WIDTH)] + attn_specs + [
            lay(1, D_MODEL), lay(1, D_MODEL),
            lay(D_MODEL, 2 * D_MODEL), lay(1, 2 * D_MODEL),
            lay(SSM_WIDTH, SSM_WIDTH), lay(SSM_WIDTH, D_MODEL), lay(GROUP_WIDTH, D_MODEL),
            lay(D_MODEL, D_MODEL),
            lay(D_MODEL, ROUTER_LANES), lay(D_MODEL, 2 * ROUTER_LANES), lay(1, ROUTER_LANES),
        ],
        out_specs=[tok(D_MODEL), tok(D_MODEL), tok(ROUTER_LANES)],
        out_shape=[jax.ShapeDtypeStruct((ntok, D_MODEL), F32),
                   jax.ShapeDtypeStruct((ntok, D_MODEL), BF16),
                   jax.ShapeDtypeStruct((ntok, ROUTER_LANES), F32)],
        compiler_params=_cparams(("parallel",)),
        name="merge_router",
    )(x, mod, y, *attn_outs, p['g_mix'], p['g_ffn'], p['w_mgate'], p['b_mgate'], p['w_glu'],
      p['w_br_ssm'], p['w_br_attn'], p['w_out'], p['w_router_hi'], p['w_router_lo'], p['b_router'])


MOE_EXPERT_BLOCK = 4
MOE_ROW_CHUNK = 128
GROUP_ID_LANE = 64
_TN = (((0,), (0,)), ((), ()))


def _moe_kernel(h2_ref, comb_ref, x1_ref, mod_ref, wg_ref, wu_ref, wd_ref, gfin_ref, out_ref,
                perm, earlier, hs, combs, ys, bounds, *, final):
    s = pl.program_id(1)
    tm = h2_ref.shape[0]
    ch = MOE_ROW_CHUNK

    @pl.when(s == 0)
    def _():
        comb = comb_ref[...]
        lane = lax.broadcasted_iota(jnp.int32, (tm, ROUTER_LANES), 1)
        gid = jnp.sum(jnp.where(lane == GROUP_ID_LANE, comb, 0.0), axis=-1, keepdims=True)
        onehot = jnp.where(lane.astype(F32) == gid, 1.0, 0.0)
        col = lax.broadcasted_iota(jnp.int32, (tm, tm), 1)

        @pl.when(pl.program_id(0) == 0)
        def _():
            row = lax.broadcasted_iota(jnp.int32, (tm, tm), 0)
            earlier[...] = jnp.where(col < row, 1.0, 0.0).astype(BF16)

        rank = _dot(earlier[...], onehot.astype(BF16))
        tot = jnp.sum(onehot, axis=0, keepdims=True)
        lane1 = lax.broadcasted_iota(jnp.int32, (1, ROUTER_LANES), 1)
        start = jnp.zeros((1, ROUTER_LANES), F32)
        run = jnp.zeros((1, 1), F32)
        for g in range(N_EXPERT_GROUPS):
            bounds[2 * g] = run[0, 0].astype(jnp.int32)
            start = start + jnp.where(lane1 == g, run, 0.0)
            run = run + jnp.sum(jnp.where(lane1 == g, tot, 0.0), axis=-1, keepdims=True)
            bounds[2 * g + 1] = run[0, 0].astype(jnp.int32)
        pos = jnp.sum(onehot * (rank + start), axis=-1, keepdims=True)
        perm[...] = jnp.where(col.astype(F32) == pos, 1.0, 0.0).astype(BF16)
        hs[...] = lax.dot_general(perm[...], h2_ref[...], _TN, preferred_element_type=F32).astype(BF16)
        c_hi = comb.astype(BF16).astype(F32)
        rest = comb - c_hi
        c_mid = rest.astype(BF16).astype(F32)
        c_lo = rest - c_mid
        packed = jnp.where(lane < 2 * N_EXPERTS, c_hi + pltpu.roll(c_mid, N_EXPERTS, axis=1),
                           jnp.where(lane >= 3 * N_EXPERTS, pltpu.roll(c_lo, 3 * N_EXPERTS, axis=1), c_hi))
        got = lax.dot_general(perm[...], packed.astype(BF16), _TN, preferred_element_type=F32)
        mid = pltpu.roll(jnp.where((lane >= N_EXPERTS) & (lane < 2 * N_EXPERTS), got, 0.0),
                         ROUTER_LANES - N_EXPERTS, axis=1)
        low = pltpu.roll(jnp.where(lane >= 3 * N_EXPERTS, got, 0.0), ROUTER_LANES - 3 * N_EXPERTS, axis=1)
        keep = (lane < N_EXPERTS) | (lane == GROUP_ID_LANE)
        combs[...] = (jnp.where(keep, got, 0.0) + mid) + low
        ys[...] = jnp.zeros_like(ys)

    g = s // (EXPERTS_PER_GROUP // MOE_EXPERT_BLOCK)
    lo = bounds[2 * g] // ch
    hi = (bounds[2 * g + 1] + (ch - 1)) // ch
    lane = lax.broadcasted_iota(jnp.int32, (ch, ROUTER_LANES), 1)

    def chunk(c):
        r0 = pl.multiple_of(c * ch, ch)
        h = hs[pl.ds(r0, ch), :]
        comb = combs[pl.ds(r0, ch), :]
        hids = []
        for j in range(MOE_EXPERT_BLOCK):
            gt = _dot(h, wg_ref[j])
            w = jnp.sum(jnp.where(lane == s * MOE_EXPERT_BLOCK + j, comb, 0.0), axis=-1, keepdims=True)
            hids.append((gt * _sigmoid(gt) * _dot(h, wu_ref[j]) * w).astype(BF16))
        ys[pl.ds(r0, ch), :] += _dot(jnp.concatenate(hids, axis=-1), wd_ref[...])

    def pair(i, carry):
        chunk(lo + 2 * i)
        chunk(lo + 2 * i + 1)
        return carry

    n_chunks = hi - lo
    lax.fori_loop(0, n_chunks // 2, pair, 0)

    @pl.when(n_chunks % 2 == 1)
    def _():
        chunk(hi - 1)

    @pl.when(s == pl.num_programs(1) - 1)
    def _():
        x2 = x1_ref[...] + mod_ref[...] * _dot(perm[...], ys[...].astype(BF16))
        out_ref[...] = _rmsnorm(x2, gfin_ref[...]) if final else x2


def _moe_call(h2, comb, x1, mod, p, l, tm, tiles_per_seq, final):
    ntok = h2.shape[0]
    eb = MOE_EXPERT_BLOCK
    tok = lambda w: pl.BlockSpec((tm, w), lambda i, s: (i, 0))
    gate_piece = 5
    if mod.ndim == 5:
        mod_spec = pl.BlockSpec((None, None, None, 1, D_MODEL),
                                lambda i, s: (l, gate_piece, i // tiles_per_seq, 0, 0))
    else:
        mod_spec = pl.BlockSpec((None, None, tm, D_MODEL), lambda i, s: (l, gate_piece, i, 0))
    w_down = p['w_exp_down'].reshape(DEPTH, N_EXPERTS // eb, eb * EXPERT_FF, D_MODEL)
    return pl.pallas_call(
        functools.partial(_moe_kernel, final=final),
        grid=(ntok // tm, N_EXPERTS // eb),
        in_specs=[tok(D_MODEL), tok(ROUTER_LANES), tok(D_MODEL), mod_spec,
                  pl.BlockSpec((None, eb, D_MODEL, EXPERT_FF), lambda i, s: (l, s, 0, 0)),
                  pl.BlockSpec((None, eb, D_MODEL, EXPERT_FF), lambda i, s: (l, s, 0, 0)),
                  pl.BlockSpec((None, None, eb * EXPERT_FF, D_MODEL), lambda i, s: (l, s, 0, 0)),
                  pl.BlockSpec((1, D_MODEL), lambda i, s: (0, 0))],
        out_specs=tok(D_MODEL),
        out_shape=jax.ShapeDtypeStruct((ntok, D_MODEL), F32),
        scratch_shapes=[pltpu.VMEM((tm, tm), BF16), pltpu.VMEM((tm, tm), BF16), pltpu.VMEM((tm, D_MODEL), BF16),
                        pltpu.VMEM((tm, ROUTER_LANES), F32), pltpu.VMEM((tm, D_MODEL), F32),
                        pltpu.SMEM((2 * N_EXPERT_GROUPS,), jnp.int32)],
        compiler_params=_cparams(("arbitrary", "arbitrary")),
        name="moe_experts",
    )(h2, comb, x1, mod, p['w_exp_gate'], p['w_exp_up'], w_down, p['g_final'])


def _kv_state(qkv, n_seq, seq, prompt):
    kv = []
    for g in range(N_GROUPS):
        keep = min(WINDOWS[g], seq)
        if prompt:
            tails = _kv_tail_call(qkv[g], n_seq, seq, keep).reshape(2, n_seq, HEADS, HEAD_DIM, keep)
            kv += [tails[0].transpose(0, 3, 1, 2), tails[1].transpose(0, 3, 1, 2)]
        else:
            for a in (1, 2):
                new = qkv[g][a].reshape(2, n_seq, seq, 2, HEAD_DIM)
                kv.append(new.transpose(1, 2, 0, 3, 4).reshape(n_seq, seq, HEADS, HEAD_DIM))
    return kv


def _run_trunks(xp, xs, mod_p, mod_s, p, ssm, batch, seq, dec_batch, dec_seq, caches, h0):
    tm_p, tm_s = 512, 256
    tm_moe_p, tm_moe_s = 1024, 512
    wide = N_GROUPS - 1
    states_p, states_s = [], []
    for l in range(DEPTH):
        ab, bre, bim, cre, cim, d_skip = ssm[l]
        up, *qkv_p = _inproj_call(xp, mod_p, p['g_mix'], p['w_in'], l, tm_p, seq // tm_p)
        us, *qkv_s = _inproj_call(xs, mod_s, p['g_mix'], p['w_in'], l, tm_s, 1)

        (yp, hre_p, him_p), attn_wide = _scan_attn_call(
            up, ab, bre, bim, cre, cim, d_skip, batch, seq,
            qkv_s[wide], caches[2 * wide], caches[2 * wide + 1], l, dec_batch, dec_seq, DILATIONS[wide])
        attn_p, attn_s = [], []
        for g in range(N_GROUPS - 1):
            o_p, o_s = _attn_prompt_call(qkv_p[g], batch, seq, DILATIONS[g],
                                         sample=(qkv_s[g], caches[2 * g], caches[2 * g + 1], l, dec_batch, dec_seq))
            attn_p += list(o_p)
            attn_s += list(o_s)
        attn_p += list(_attn_prompt_call(qkv_p[wide], batch, seq, DILATIONS[wide]))
        attn_s += list(attn_wide)
        x1, h2, comb = _merge_call(xp, mod_p, yp, attn_p, p, l, tm_p, seq // tm_p)
        xp = _moe_call(h2, comb, x1, mod_p, p, l, tm_moe_p, seq // tm_moe_p, final=(l == DEPTH - 1))
        states_p.append(_kv_state(qkv_p, batch, seq, True)
                        + [hre_p[:, 0].reshape(batch, N_SSM_GROUPS, SSM_STATE),
                           him_p[:, 0].reshape(batch, N_SSM_GROUPS, SSM_STATE)])

        ys, hre_s, him_s = _ssm_sample_call(us.reshape(dec_batch, dec_seq * SSM_WIDTH), ab, bre, bim, cre, cim,
                                            d_skip, h0[0][l], h0[1][l], dec_seq)
        x1, h2, comb = _merge_call(xs, mod_s, ys.reshape(dec_batch * dec_seq, SSM_WIDTH), attn_s, p, l, tm_s, 1)
        xs = _moe_call(h2, comb, x1, mod_s, p, l, tm_moe_s, 1, final=(l == DEPTH - 1))
        states_s.append(_kv_state(qkv_s, dec_batch, dec_seq, False)
                        + [hre_s.reshape(dec_batch, N_SSM_GROUPS, SSM_STATE),
                           him_s.reshape(dec_batch, N_SSM_GROUPS, SSM_STATE)])
    stack = lambda states: [jnp.stack([states[l][i] for l in range(DEPTH)], axis=0)
                            for i in range(len(states[0]))]
    return xp, xs, stack(states_p), stack(states_s)


def kernel(x_prompt, x_sample, cache_k_w128, cache_v_w128, cache_k_w512, cache_v_w512,
           cache_k_w2048, cache_v_w2048, state_ssm_re, state_ssm_im, c_prompt, c_sample,
           w_ada, b_ada, g_norm_mix, g_norm_ffn, g_final, w_in, ssm_a_re, ssm_a_im, ssm_b_re,
           ssm_b_im, ssm_c_re, ssm_c_im, ssm_log_step, ssm_d, w_glu, w_br_ssm, w_br_attn,
           w_mgate, b_mgate, w_out, w_router_grp, b_router_grp, w_router_exp, b_router_exp,
           w_exp_gate, w_exp_up, w_exp_down):
    batch, seq, _ = x_prompt.shape
    dec_batch, dec_seq, _ = x_sample.shape

    pad = -batch % SUBLANES
    c_p = jnp.concatenate([c_prompt, jnp.zeros((pad, D_MODEL), F32)], axis=0)
    mod_p, mod_s = _ada_call(c_p, jnp.repeat(c_sample, dec_seq, axis=0), w_ada, b_ada)
    mod_prompt = mod_p.reshape(DEPTH, 6, batch + pad, 1, D_MODEL)
    mod_sample = mod_s

    abr, abi, bbr, bbi = _ssm_param_call(ssm_a_re, ssm_a_im, ssm_log_step, ssm_b_re, ssm_b_im)
    ssm = []
    for l in range(DEPTH):
        pick = lambda a: a[l].reshape(N_SSM_GROUPS, SSM_STATE, SSM_GROUP_CH)
        ab = jnp.stack([pick(abr)[:, :, 0].reshape(SSM_LANES), pick(abi)[:, :, 0].reshape(SSM_LANES)])
        bre = _block_diag_halves(pick(bbr).transpose(0, 2, 1), BF16)
        bim = _block_diag_halves(pick(bbi).transpose(0, 2, 1), BF16)
        cre = _block_diag_halves(ssm_c_re[l].transpose(0, 2, 1), BF16)
        cim = _block_diag_halves(ssm_c_im[l].transpose(0, 2, 1), BF16)
        ssm.append((ab, bre, bim, cre, cim, ssm_d[l].reshape(1, SSM_WIDTH)))

    w_router = jnp.concatenate(
        [w_router_exp, w_router_grp,
         jnp.zeros((DEPTH, D_MODEL, ROUTER_LANES - N_EXPERTS - N_EXPERT_GROUPS), F32)], axis=-1)
    b_router = jnp.concatenate(
        [b_router_exp, b_router_grp,
         jnp.zeros((DEPTH, ROUTER_LANES - N_EXPERTS - N_EXPERT_GROUPS), F32)], axis=-1)
    w_router_hi = w_router.astype(BF16)
    p = dict(
        g_mix=g_norm_mix.reshape(DEPTH, 1, D_MODEL), g_ffn=g_norm_ffn.reshape(DEPTH, 1, D_MODEL),
        g_final=g_final.reshape(1, D_MODEL), w_in=w_in.astype(BF16), w_glu=w_glu.astype(BF16),
        w_br_ssm=w_br_ssm.astype(BF16), w_br_attn=w_br_attn.astype(BF16),
        w_mgate=w_mgate.astype(BF16), b_mgate=b_mgate.reshape(DEPTH, 1, 2 * D_MODEL),
        w_out=w_out.astype(BF16), w_router_hi=w_router_hi,
        w_router_lo=jnp.concatenate([w_router_hi, (w_router - w_router_hi.astype(F32)).astype(BF16)], axis=-1),
        b_router=b_router.reshape(DEPTH, 1, ROUTER_LANES),
        w_exp_gate=w_exp_gate.astype(BF16), w_exp_up=w_exp_up.astype(BF16),
        w_exp_down=w_exp_down.astype(BF16))

    caches = [cache_k_w128, cache_v_w128, cache_k_w512, cache_v_w512, cache_k_w2048, cache_v_w2048]
    h0 = (state_ssm_re.reshape(DEPTH, dec_batch, SSM_LANES), state_ssm_im.reshape(DEPTH, dec_batch, SSM_LANES))
    y_prompt, y_sample, pstate, sstate = _run_trunks(
        x_prompt.reshape(batch * seq, D_MODEL), x_sample.reshape(dec_batch * dec_seq, D_MODEL),
        mod_prompt, mod_sample, p, ssm, batch, seq, dec_batch, dec_seq, caches, h0)
    return (y_prompt.reshape(batch, seq, D_MODEL), y_sample.reshape(dec_batch, dec_seq, D_MODEL),
            *pstate, *sstate)
```

```python
import functools
import math

import jax
import jax.numpy as jnp
from jax import lax
from jax.experimental import pallas as pl
from jax.experimental.pallas import tpu as pltpu

F32 = jnp.float32
BF16 = jnp.bfloat16

D_MODEL = 1024
DEPTH = 2
SSM_WIDTH = 512
SSM_GROUP_CH = 16
N_SSM_GROUPS = 32
SSM_STATE = 64
SSM_LANES = N_SSM_GROUPS * SSM_STATE
HEAD_DIM = 64
HEADS = 4
GROUP_WIDTH = HEADS * HEAD_DIM
PAIR_WIDTH = 2 * HEAD_DIM
WINDOWS = (128, 512, 2048)
DILATIONS = (1, 4, 16)
N_BACK = 128
N_GROUPS = 3
QKV_WIDTH = N_GROUPS * GROUP_WIDTH
IN_PROJ_WIDTH = SSM_WIDTH + 3 * QKV_WIDTH
N_EXPERT_GROUPS = 4
EXPERTS_PER_GROUP = 8
N_EXPERTS = 32
EXPERT_FF = 256
RMS_EPS = 1e-6
NEG_INF = -1e30
ROUTER_LANES = 128
GROUP_LOGIT_LANE0 = N_EXPERTS

SUBLANES = 8
VMEM_LIMIT = 56 * 1024 * 1024


def _cparams(sem):
    return pltpu.CompilerParams(dimension_semantics=sem, vmem_limit_bytes=VMEM_LIMIT)


def _sigmoid(x):
    return 1.0 / (1.0 + jnp.exp(-x))


def _gelu_tanh(x):
    return 0.5 * x * (1.0 + jnp.tanh(math.sqrt(2.0 / math.pi) * (x + 0.044715 * (x * x * x))))


def _dot(a, b):
    return jnp.dot(a, b, preferred_element_type=F32)


def _rmsnorm(x, g):
    ms = jnp.mean(x * x, axis=-1, keepdims=True)
    return x * lax.rsqrt(ms + RMS_EPS) * g


def _ada_kernel(cp_ref, cs_ref, w_ref, b_ref, op_ref, os_ref):
    w = w_ref[...].astype(BF16)
    for c_ref, o_ref in ((cp_ref, op_ref), (cs_ref, os_ref)):
        c = c_ref[...]
        o_ref[...] = _dot((c * _sigmoid(c)).astype(BF16), w) + b_ref[...]


def _ada_call(c_prompt_rows, c_sample_rows, w_ada, b_ada):
    rp, rs = c_prompt_rows.shape[0], c_sample_rows.shape[0]
    out = lambda r: pl.BlockSpec((None, None, r, D_MODEL), lambda l, j: (l, j, 0, 0))
    return pl.pallas_call(
        _ada_kernel,
        grid=(DEPTH, 6),
        in_specs=[
            pl.BlockSpec((rp, D_MODEL), lambda l, j: (0, 0)),
            pl.BlockSpec((rs, D_MODEL), lambda l, j: (0, 0)),
            pl.BlockSpec((None, D_MODEL, D_MODEL), lambda l, j: (l, 0, j)),
            pl.BlockSpec((None, 1, D_MODEL), lambda l, j: (l, 0, j)),
        ],
        out_specs=[out(rp), out(rs)],
        out_shape=[jax.ShapeDtypeStruct((DEPTH, 6, rp, D_MODEL), F32),
                   jax.ShapeDtypeStruct((DEPTH, 6, rs, D_MODEL), F32)],
        compiler_params=_cparams(("parallel", "parallel")),
        name="ada_mod",
    )(c_prompt_rows, c_sample_rows, w_ada, b_ada.reshape(DEPTH, 1, 6 * D_MODEL))


def _ssm_param_kernel(ar_ref, ai_ref, ls_ref, br_ref, bi_ref, abr_ref, abi_ref, bbr_ref, bbi_ref):
    lr, li = ar_ref[...], ai_ref[...]
    dt = jnp.exp(ls_ref[...])
    mag = jnp.exp(lr * dt)
    abr = mag * jnp.cos(li * dt)
    abi = mag * jnp.sin(li * dt)
    den = lr * lr + li * li
    er, ei = abr - 1.0, abi
    fr = (er * lr + ei * li) / den
    fi = (ei * lr - er * li) / den
    br, bi = br_ref[...], bi_ref[...]
    abr_ref[...] = abr
    abi_ref[...] = abi
    bbr_ref[...] = fr * br - fi * bi
    bbi_ref[...] = fr * bi + fi * br


def _ssm_param_call(a_re, a_im, log_step, b_re, b_im):
    w = SSM_STATE * SSM_GROUP_CH
    rep = lambda a: jnp.repeat(a, SSM_GROUP_CH, axis=-1)
    big = pl.BlockSpec((None, N_SSM_GROUPS, w), lambda l: (l, 0, 0))
    shp = jax.ShapeDtypeStruct((DEPTH, N_SSM_GROUPS, w), F32)
    return pl.pallas_call(
        _ssm_param_kernel,
        grid=(DEPTH,),
        in_specs=[big, big, pl.BlockSpec((None, N_SSM_GROUPS, 1), lambda l: (l, 0, 0)), big, big],
        out_specs=[big, big, big, big],
        out_shape=[shp, shp, shp, shp],
        compiler_params=_cparams(("parallel",)),
        name="ssm_params",
    )(rep(a_re), rep(a_im), log_step.reshape(DEPTH, N_SSM_GROUPS, 1),
      b_re.reshape(DEPTH, N_SSM_GROUPS, w), b_im.reshape(DEPTH, N_SSM_GROUPS, w))


def _block_diag_halves(blocks, dtype):
    g, r, c = blocks.shape
    half = g // 2
    eye = jnp.eye(half, dtype=bool)

    def one(b):
        m = jnp.where(eye[:, None, :, None], b[:, :, None, :], 0.0)
        return m.reshape(half * r, half * c)

    return jnp.stack([one(blocks[:half]), one(blocks[half:])]).astype(dtype)


def _inproj_kernel(x_ref, mod_ref, g_ref, w_ref, u_ref, a0_ref, a1_ref, a2_ref):
    h = _rmsnorm(x_ref[...], g_ref[...]) * (1.0 + mod_ref[1]) + mod_ref[0]
    hb = h.astype(BF16)
    u_ref[...] = _dot(hb, w_ref[:, :SSM_WIDTH])
    for g, ref in enumerate((a0_ref, a1_ref, a2_ref)):
        for a in range(3):
            c0 = SSM_WIDTH + a * QKV_WIDTH + g * GROUP_WIDTH
            val = _dot(hb, w_ref[:, c0:c0 + GROUP_WIDTH])
            ref[a, 0] = val[:, :PAIR_WIDTH]
            ref[a, 1] = val[:, PAIR_WIDTH:]


def _mod_spec(mod, l, tm, tiles_per_seq):
    if mod.ndim == 5:
        return pl.BlockSpec((None, 6, None, 1, D_MODEL), lambda i: (l, 0, i // tiles_per_seq, 0, 0))
    return pl.BlockSpec((None, 6, tm, D_MODEL), lambda i: (l, 0, i, 0))


def _inproj_call(x, mod, g_mix, w_in, l, tm, tiles_per_seq):
    ntok = x.shape[0]
    grp = jax.ShapeDtypeStruct((3, 2, ntok, PAIR_WIDTH), F32)
    grp_spec = pl.BlockSpec((3, 2, tm, PAIR_WIDTH), lambda i: (0, 0, i, 0))
    return pl.pallas_call(
        _inproj_kernel,
        grid=(ntok // tm,),
        in_specs=[
            pl.BlockSpec((tm, D_MODEL), lambda i: (i, 0)),
            _mod_spec(mod, l, tm, tiles_per_seq),
            pl.BlockSpec((None, 1, D_MODEL), lambda i: (l, 0, 0)),
            pl.BlockSpec((None, D_MODEL, IN_PROJ_WIDTH), lambda i: (l, 0, 0), pipeline_mode=pl.Buffered(1)),
        ],
        out_specs=[pl.BlockSpec((tm, SSM_WIDTH), lambda i: (i, 0)), grp_spec, grp_spec, grp_spec],
        out_shape=[jax.ShapeDtypeStruct((ntok, SSM_WIDTH), F32), grp, grp, grp],
        compiler_params=_cparams(("parallel",)),
        name="in_proj",
    )(x, mod, g_mix, w_in)


LANE_TILE = 128
SCAN_CHUNK = 256
SEG_LEN = SCAN_CHUNK // SUBLANES
SCAN_TILE_GROUP = 4


def _cmul(ar, ai, br, bi):
    return ar * br - ai * bi, ar * bi + ai * br


def _ssm_prompt_kernel(u_ref, pm_ref, pmt_ref, ab_ref, bre_ref, bim_ref, cre_ref, cim_ref, d_ref,
                       y_ref, hre_ref, him_ref, bur, bui, tab, carry):
    c = pl.program_id(1)
    half = SSM_LANES // 2
    tiles_per_half = half // LANE_TILE

    @pl.when(c == 0)
    def _():
        shp = (SUBLANES, SSM_LANES)
        sub = lax.broadcasted_iota(jnp.int32, shp, 0)
        b_r = jnp.broadcast_to(ab_ref[0:1, :], shp)
        b_i = jnp.broadcast_to(ab_ref[1:2, :], shp)
        for _ in range(SEG_LEN.bit_length() - 1):
            b_r, b_i = _cmul(b_r, b_i, b_r, b_i)
        p_r, p_i = b_r, b_i
        ps_r, ps_i = jnp.ones(shp, F32), jnp.zeros(shp, F32)
        for s in range(1, SUBLANES):
            if s in (1, 2, 4):
                k = {1: 0, 2: 2, 4: 4}[s]
                tab[k] = jnp.where(sub >= s, p_r, 0.0)
                tab[k + 1] = jnp.where(sub >= s, p_i, 0.0)
            ps_r = jnp.where(sub >= s, p_r, ps_r)
            ps_i = jnp.where(sub >= s, p_i, ps_i)
            p_r, p_i = _cmul(p_r, p_i, b_r, b_i)
        tab[6] = ps_r
        tab[7] = ps_i
        carry[...] = jnp.zeros_like(carry)

    u = u_ref[...]
    ub = _dot(pm_ref[...], u.astype(BF16)).astype(BF16)
    for h in range(2):
        uh = ub[:, h * 256:(h + 1) * 256]
        res_r = _dot(uh, bre_ref[h])
        res_i = _dot(uh, bim_ref[h])
        for j in range(tiles_per_half):
            bur[h * tiles_per_half + j] = res_r[:, j * LANE_TILE:(j + 1) * LANE_TILE]
            bui[h * tiles_per_half + j] = res_i[:, j * LANE_TILE:(j + 1) * LANE_TILE]

    vshape = (SUBLANES, LANE_TILE)
    sub1 = lax.broadcasted_iota(jnp.int32, vshape, 0) >= 1
    for g0 in range(0, SSM_LANES // LANE_TILE, SCAN_TILE_GROUP):
        tiles = list(range(g0, g0 + SCAN_TILE_GROUP))
        lanes = [slice(lt * LANE_TILE, (lt + 1) * LANE_TILE) for lt in tiles]
        a_r = [jnp.broadcast_to(ab_ref[0:1, sl], vshape) for sl in lanes]
        a_i = [jnp.broadcast_to(ab_ref[1:2, sl], vshape) for sl in lanes]

        def step(t, hs, store):
            out = []
            for j, lt in enumerate(tiles):
                rows = pl.ds(pl.multiple_of(t * SUBLANES, SUBLANES), SUBLANES)
                hr, hi = hs[2 * j], hs[2 * j + 1]
                nr = (a_r[j] * hr - a_i[j] * hi) + bur[lt, rows, :]
                ni = (a_r[j] * hi + a_i[j] * hr) + bui[lt, rows, :]
                if store:
                    bur[lt, rows, :] = nr
                    bui[lt, rows, :] = ni
                out += [nr, ni]
            return tuple(out)

        zeros = tuple(jnp.zeros(vshape, F32) for _ in range(2 * SCAN_TILE_GROUP))
        ends = lax.fori_loop(0, SEG_LEN, lambda t, hs: step(t, hs, False), zeros, unroll=True)

        starts = []
        for j, sl in enumerate(lanes):
            fr, fi = ends[2 * j], ends[2 * j + 1]
            for k in (1, 2, 4):
                ti = {1: 0, 2: 2, 4: 4}[k]
                mr, mi = tab[ti, :, sl], tab[ti + 1, :, sl]
                rr = pltpu.roll(fr, k, axis=0)
                ri = pltpu.roll(fi, k, axis=0)
                fr, fi = fr + (mr * rr - mi * ri), fi + (mr * ri + mi * rr)
            cr, ci = carry[0, :, sl], carry[1, :, sl]
            ps_r, ps_i = tab[6, :, sl], tab[7, :, sl]
            sr = (ps_r * cr - ps_i * ci) + jnp.where(sub1, pltpu.roll(fr, 1, axis=0), 0.0)
            si = (ps_r * ci + ps_i * cr) + jnp.where(sub1, pltpu.roll(fi, 1, axis=0), 0.0)
            starts += [sr, si]

        last = lax.fori_loop(0, SEG_LEN, lambda t, hs: step(t, hs, True), tuple(starts), unroll=True)
        for j, sl in enumerate(lanes):
            carry[0, :, sl] = jnp.broadcast_to(last[2 * j][SUBLANES - 1:SUBLANES, :], vshape)
            carry[1, :, sl] = jnp.broadcast_to(last[2 * j + 1][SUBLANES - 1:SUBLANES, :], vshape)

    for h in range(2):
        cs = slice(h * 256, (h + 1) * 256)
        tl = range(h * tiles_per_half, (h + 1) * tiles_per_half)
        hr = jnp.concatenate([bur[lt] for lt in tl], axis=-1).astype(BF16)
        hi = jnp.concatenate([bui[lt] for lt in tl], axis=-1).astype(BF16)
        y = _dot(hr, cre_ref[h]) - _dot(hi, cim_ref[h])
        y_hi = y.astype(BF16)
        y_lo = (y - y_hi.astype(F32)).astype(BF16)
        y = _dot(pmt_ref[...], y_hi) + _dot(pmt_ref[...], y_lo)
        y_ref[:, cs] = y + d_ref[:, cs] * u[:, cs]

    @pl.when(c == pl.num_programs(1) - 1)
    def _():
        hre_ref[...] = carry[0]
        him_ref[...] = carry[1]


def _ssm_sample_kernel(u_ref, ab_ref, bre_ref, bim_ref, cre_ref, cim_ref, d_ref, h0r_ref, h0i_ref,
                       y_ref, hr_ref, hi_ref, *, steps):
    half = SSM_LANES // 2
    ar, ai = ab_ref[0:1, :], ab_ref[1:2, :]
    hr_ref[...] = h0r_ref[...]
    hi_ref[...] = h0i_ref[...]
    for t in range(steps):
        u = u_ref[:, t * SSM_WIDTH:(t + 1) * SSM_WIDTH]
        ub = u.astype(BF16)
        for h in range(2):
            hs = slice(h * half, (h + 1) * half)
            cs = slice(h * 256, (h + 1) * 256)
            uh = ub[:, cs]
            hr, hi = hr_ref[:, hs], hi_ref[:, hs]
            a_r, a_i = ar[:, hs], ai[:, hs]
            nr = (a_r * hr - a_i * hi) + _dot(uh, bre_ref[h])
            ni = (a_r * hi + a_i * hr) + _dot(uh, bim_ref[h])
            hr_ref[:, hs] = nr
            hi_ref[:, hs] = ni
            y = _dot(nr.astype(BF16), cre_ref[h]) - _dot(ni.astype(BF16), cim_ref[h])
            y_ref[:, t * SSM_WIDTH + h * 256:t * SSM_WIDTH + (h + 1) * 256] = y + d_ref[:, cs] * u[:, cs]


def _ssm_sample_call(u, ab, bre, bim, cre, cim, d_skip, h0r, h0i, steps):
    n = u.shape[0]
    st = jax.ShapeDtypeStruct((n, SSM_LANES), F32)
    return pl.pallas_call(
        functools.partial(_ssm_sample_kernel, steps=steps),
        out_shape=[jax.ShapeDtypeStruct((n, steps * SSM_WIDTH), F32), st, st],
        compiler_params=pltpu.CompilerParams(vmem_limit_bytes=VMEM_LIMIT),
        name="ssm_sample",
    )(u, ab, bre, bim, cre, cim, d_skip, h0r, h0i)


_NT = (((1,), (1,)), ((), ()))


def _attn_prompt_kernel(q_ref, kp_ref, kc_ref, vp_ref, vc_ref, o_ref, lse_ref, *, dil, q_blocks):
    b = pl.program_id(2)
    qi = lax.broadcasted_iota(jnp.int32, (N_BACK, 2 * N_BACK), 0)
    kj = lax.broadcasted_iota(jnp.int32, (N_BACK, 2 * N_BACK), 1)
    band = (kj >= qi) & (kj <= qi + N_BACK)
    band_first = band & (kj >= jnp.where(b > 0, 0, N_BACK))
    lane = lax.broadcasted_iota(jnp.int32, (N_BACK, PAIR_WIDTH), 1)
    head0 = lane < HEAD_DIM

    def rows(j, r):
        start = j * N_BACK * dil + r
        return pl.ds(start, N_BACK, stride=dil) if dil > 1 else pl.ds(start, N_BACK)

    def one(j, r):
        cur = rows(j, r)
        q = (q_ref[cur, :] * (HEAD_DIM ** -0.5)).astype(BF16)
        k_prev = kp_ref[rows(0, r), :] if j == 0 else kc_ref[rows(j - 1, r), :]
        v_prev = vp_ref[rows(0, r), :] if j == 0 else vc_ref[rows(j - 1, r), :]
        kk = jnp.concatenate([k_prev, kc_ref[cur, :]], axis=0).astype(BF16)
        vv = jnp.concatenate([v_prev, vc_ref[cur, :]], axis=0).astype(BF16)
        valid = band_first if j == 0 else band
        outs, lses = [], []
        for sub in range(2):
            in_head = head0 if sub == 0 else jnp.logical_not(head0)
            qm = jnp.where(in_head, q, jnp.zeros_like(q))
            s = lax.dot_general(qm, kk, _NT, preferred_element_type=F32)
            s = jnp.where(valid, s, NEG_INF)
            m = jnp.max(s, axis=-1, keepdims=True)
            p = jnp.exp(s - m)
            den = jnp.sum(p, axis=-1, keepdims=True)
            outs.append(_dot(p.astype(BF16), vv) / den)
            lses.append(jnp.broadcast_to(m + jnp.log(den), (N_BACK, PAIR_WIDTH)))
        o_ref[cur, :] = jnp.where(head0, outs[0], outs[1])
        lse_ref[cur, :] = jnp.where(head0, lses[0], lses[1])

    def residue(r, carry):
        for j in range(q_blocks):
            one(j, r)
        return carry

    if dil <= ATTN_UNROLL:
        for r in range(dil):
            residue(r, 0)
    else:
        lax.fori_loop(0, dil, residue, 0, unroll=2 * ATTN_UNROLL)


ATTN_UNROLL = 4
ATTN_Q_BLOCKS = {1: 8, 4: 2, 16: 1}


def _attn_pair_kernel(q_ref, kp_ref, kc_ref, vp_ref, vc_ref, qkv_s_ref, kt_ref, vt_ref,
                      o_ref, lse_ref, o_s_ref, lse_s_ref, *, dil, q_blocks, steps):
    _attn_sample_kernel(qkv_s_ref, kt_ref, vt_ref, o_s_ref, lse_s_ref, dil=dil, steps=steps)
    _attn_prompt_kernel(q_ref, kp_ref, kc_ref, vp_ref, vc_ref, o_ref, lse_ref, dil=dil, q_blocks=q_blocks)


def _attn_prompt_call(qkv, n_seq, seq, dil, sample=None):
    q_blocks = ATTN_Q_BLOCKS[dil]
    prev_rows = N_BACK * dil
    rows = prev_rows * q_blocks
    nbk = seq // rows
    cur = lambda a: pl.BlockSpec((None, None, rows, PAIR_WIDTH), lambda n, hp, b: (a, hp, n * nbk + b, 0))
    prev = lambda a: pl.BlockSpec(
        (None, None, prev_rows, PAIR_WIDTH),
        lambda n, hp, b: (a, hp, jnp.maximum((n * nbk + b) * q_blocks - 1, n * nbk * q_blocks), 0))
    out_spec = pl.BlockSpec((None, rows, PAIR_WIDTH), lambda n, hp, b: (hp, n * nbk + b, 0))
    out_shape = jax.ShapeDtypeStruct((2, n_seq * seq, PAIR_WIDTH), F32)
    in_specs = [cur(0), prev(1), cur(1), prev(2), cur(2)]
    operands = [qkv, qkv, qkv, qkv, qkv]
    out_specs, out_shapes = [out_spec, out_spec], [out_shape, out_shape]
    if sample is None:
        body = functools.partial(_attn_prompt_kernel, dil=dil, q_blocks=q_blocks)
    else:
        qkv_s, k_cache, v_cache, l, n_dec, steps = sample
        w = k_cache.shape[2]
        nb = SAMPLE_SEQ_BLOCK[w]
        assert n_seq * 2 * nbk * nb == n_dec, "sample blocks must match the prompt attention grid"
        step = lambda n, hp, b: (n * 2 + hp) * nbk + b
        cache_spec = pl.BlockSpec((None, nb, HEADS, HEAD_DIM, w), lambda n, hp, b: (l, step(n, hp, b), 0, 0, 0))
        s_out = pl.BlockSpec((2, nb * steps, PAIR_WIDTH), lambda n, hp, b: (0, step(n, hp, b), 0))
        s_shape = jax.ShapeDtypeStruct((2, n_dec * steps, PAIR_WIDTH), F32)
        in_specs += [pl.BlockSpec((3, 2, nb * steps, PAIR_WIDTH), lambda n, hp, b: (0, 0, step(n, hp, b), 0)),
                     cache_spec, cache_spec]
        operands += [qkv_s, k_cache.transpose(0, 1, 3, 4, 2), v_cache.transpose(0, 1, 3, 4, 2)]
        out_specs += [s_out, s_out]
        out_shapes += [s_shape, s_shape]
        body = functools.partial(_attn_pair_kernel, dil=dil, q_blocks=q_blocks, steps=steps)
    outs = pl.pallas_call(
        body,
        grid=(n_seq, 2, nbk),
        in_specs=in_specs,
        out_specs=out_specs,
        out_shape=out_shapes,
        compiler_params=_cparams(("parallel", "parallel", "parallel")),
        name=f"attn_prompt_d{dil}",
    )(*operands)
    return outs if sample is None else (outs[:2], outs[2:])


def _kv_tail_kernel(x_ref, o_ref):
    o_ref[...] = x_ref[...].T


def _kv_tail_call(qkv, n_seq, seq, keep):
    blocks_per_seq = seq // keep
    return pl.pallas_call(
        _kv_tail_kernel,
        grid=(2, n_seq, 2),
        in_specs=[pl.BlockSpec((None, None, keep, PAIR_WIDTH),
                               lambda a, n, hp: (a + 1, hp, (n + 1) * blocks_per_seq - 1, 0))],
        out_specs=pl.BlockSpec((None, None, None, PAIR_WIDTH, keep), lambda a, n, hp: (a, n, hp, 0, 0)),
        out_shape=jax.ShapeDtypeStruct((2, n_seq, 2, PAIR_WIDTH, keep), F32),
        compiler_params=_cparams(("parallel", "parallel", "parallel")),
        name="kv_tail",
    )(qkv)


SAMPLE_SEQ_BLOCK = {128: 4, 512: 4, 2048: 2}


def _attn_sample_kernel(qkv_ref, kt_ref, vt_ref, o_ref, lse_ref, *, dil, steps):
    nb, _, _, w = kt_ref.shape
    rows = nb * steps
    tok = lax.broadcasted_iota(jnp.int32, (2 * rows, w), 0) % rows
    pos = lax.broadcasted_iota(jnp.int32, (2 * rows, w), 1)
    tokn = lax.broadcasted_iota(jnp.int32, (2 * rows, rows), 0) % rows
    coln = lax.broadcasted_iota(jnp.int32, (2 * rows, rows), 1)
    rowo = lax.broadcasted_iota(jnp.int32, (rows, PAIR_WIDTH), 0)
    head0 = lax.broadcasted_iota(jnp.int32, (rows, PAIR_WIDTH), 1) < HEAD_DIM

    vns, s_buf, s_new = [], {}, []
    for hp in range(2):
        q2 = qkv_ref[0, hp] * (HEAD_DIM ** -0.5)
        q_st = jnp.concatenate([jnp.where(head0, q2, 0.0), jnp.where(head0, 0.0, q2)], axis=0).astype(BF16)
        kn = qkv_ref[1, hp].astype(BF16)
        s_new.append(lax.dot_general(q_st, kn, _NT, preferred_element_type=F32))
        vns.append(qkv_ref[2, hp].astype(BF16))
        for n in range(nb):
            kt = kt_ref[n, 2 * hp:2 * hp + 2].reshape(PAIR_WIDTH, w).astype(BF16)
            s_buf[hp, n] = _dot(q_st, kt)

    probs = {}
    for hp in range(2):
        for n in range(nb):
            t = tok - n * steps
            in_buf = (((pos - t) & (dil - 1)) == 0) & (pos >= t)
            tn = tokn - n * steps
            sn = coln - n * steps
            in_new = ((sn >= 0) & (sn <= tn)) if dil == 1 else (sn == tn)
            s = jnp.where(in_buf, s_buf[hp, n], NEG_INF)
            s2 = jnp.where(in_new, s_new[hp], NEG_INF)
            m = jnp.maximum(jnp.max(s, axis=-1, keepdims=True), jnp.max(s2, axis=-1, keepdims=True))
            p = jnp.exp(s - m)
            p2 = jnp.exp(s2 - m)
            den = jnp.sum(p, axis=-1, keepdims=True) + jnp.sum(p2, axis=-1, keepdims=True)
            probs[hp, n] = (p.astype(BF16), p2.astype(BF16), den, m + jnp.log(den))

    for hp in range(2):
        o_pair = jnp.zeros((rows, PAIR_WIDTH), F32)
        lse_pair = jnp.zeros((rows, PAIR_WIDTH), F32)
        for n in range(nb):
            p, p2, den, lse = probs[hp, n]
            vt = vt_ref[n, 2 * hp:2 * hp + 2].reshape(PAIR_WIDTH, w).astype(BF16)
            o = (lax.dot_general(p, vt, _NT, preferred_element_type=F32) + _dot(p2, vns[hp])) / den
            o = jnp.where(head0, o[:rows], o[rows:])
            lse = jnp.where(head0, jnp.broadcast_to(lse[:rows], (rows, PAIR_WIDTH)),
                            jnp.broadcast_to(lse[rows:], (rows, PAIR_WIDTH)))
            mine = (rowo >= n * steps) & (rowo < (n + 1) * steps)
            o_pair = jnp.where(mine, o, o_pair)
            lse_pair = jnp.where(mine, lse, lse_pair)
        o_ref[hp] = o_pair
        lse_ref[hp] = lse_pair


def _scan_attn_kernel(u_ref, pm_ref, pmt_ref, ab_ref, bre_ref, bim_ref, cre_ref, cim_ref, d_ref,
                      qkv_ref, kt_ref, vt_ref, y_ref, hre_ref, him_ref, o_ref, lse_ref,
                      bur, bui, tab, carry, *, dil, steps):
    _attn_sample_kernel(qkv_ref, kt_ref, vt_ref, o_ref, lse_ref, dil=dil, steps=steps)
    _ssm_prompt_kernel(u_ref, pm_ref, pmt_ref, ab_ref, bre_ref, bim_ref, cre_ref, cim_ref, d_ref,
                       y_ref, hre_ref, him_ref, bur, bui, tab, carry)


def _scan_attn_call(u, ab, bre, bim, cre, cim, d_skip, n_seq, seq,
                    qkv_s, k_cache, v_cache, l, n_dec, steps, dil):
    chunks = seq // SCAN_CHUNK
    w = k_cache.shape[2]
    nb = SAMPLE_SEQ_BLOCK[w]
    rows = nb * steps
    assert n_seq * chunks * nb == n_dec, "sample blocks must match the scan grid"
    kt = k_cache.transpose(0, 1, 3, 4, 2)
    vt = v_cache.transpose(0, 1, 3, 4, 2)
    half = SSM_LANES // 2
    wspec_b = pl.BlockSpec((2, 256, half), lambda n, c: (0, 0, 0))
    wspec_c = pl.BlockSpec((2, half, 256), lambda n, c: (0, 0, 0))
    st_spec = pl.BlockSpec((None, SUBLANES, SSM_LANES), lambda n, c: (n, 0, 0))
    st_shape = jax.ShapeDtypeStruct((n_seq, SUBLANES, SSM_LANES), F32)
    r = jnp.arange(SCAN_CHUNK)
    pm = (r[None, :] == ((r % SUBLANES) * SEG_LEN + r // SUBLANES)[:, None]).astype(BF16)
    perm_spec = pl.BlockSpec((SCAN_CHUNK, SCAN_CHUNK), lambda n, c: (0, 0))
    cache_spec = pl.BlockSpec((None, nb, HEADS, HEAD_DIM, w), lambda n, c: (l, n * chunks + c, 0, 0, 0))
    attn_out = pl.BlockSpec((2, rows, PAIR_WIDTH), lambda n, c: (0, n * chunks + c, 0))
    attn_shape = jax.ShapeDtypeStruct((2, n_dec * steps, PAIR_WIDTH), F32)
    y, hre, him, o, lse = pl.pallas_call(
        functools.partial(_scan_attn_kernel, dil=dil, steps=steps),
        grid=(n_seq, chunks),
        in_specs=[
            pl.BlockSpec((SCAN_CHUNK, SSM_WIDTH), lambda n, c: (n * chunks + c, 0)),
            perm_spec, perm_spec,
            pl.BlockSpec((2, SSM_LANES), lambda n, c: (0, 0)),
            wspec_b, wspec_b, wspec_c, wspec_c,
            pl.BlockSpec((1, SSM_WIDTH), lambda n, c: (0, 0)),
            pl.BlockSpec((3, 2, rows, PAIR_WIDTH), lambda n, c: (0, 0, n * chunks + c, 0)),
            cache_spec, cache_spec,
        ],
        out_specs=[pl.BlockSpec((SCAN_CHUNK, SSM_WIDTH), lambda n, c: (n * chunks + c, 0)),
                   st_spec, st_spec, attn_out, attn_out],
        out_shape=[jax.ShapeDtypeStruct((n_seq * seq, SSM_WIDTH), F32), st_shape, st_shape,
                   attn_shape, attn_shape],
        scratch_shapes=[pltpu.VMEM((SSM_LANES // LANE_TILE, SCAN_CHUNK, LANE_TILE), F32),
                        pltpu.VMEM((SSM_LANES // LANE_TILE, SCAN_CHUNK, LANE_TILE), F32),
                        pltpu.VMEM((8, SUBLANES, SSM_LANES), F32),
                        pltpu.VMEM((2, SUBLANES, SSM_LANES), F32)],
        compiler_params=_cparams(("parallel", "arbitrary")),
        name="scan_prompt_attn_sample",
    )(u, pm, pm.T, ab, bre, bim, cre, cim, d_skip, qkv_s, kt, vt)
    return (y, hre, him), (o, lse)


def _merge_kernel(x_ref, mod_ref, y_ref, o0_ref, l0_ref, o1_ref, l1_ref, o2_ref, l2_ref,
                  gmix_ref, gffn_ref, wmg_ref, bmg_ref, wglu_ref, wbs_ref, wba_ref, wout_ref,
                  wrh_ref, wrl_ref, br_ref, x1_ref, h2_ref, comb_ref):
    x = x_ref[...]
    tm = x.shape[0]
    hb = (_rmsnorm(x, gmix_ref[...]) * (1.0 + mod_ref[1]) + mod_ref[0]).astype(BF16)

    y = _gelu_tanh(y_ref[...])
    yg = y * _sigmoid(_dot(y.astype(BF16), wglu_ref[...]))
    ssm_branch = _dot(yg.astype(BF16), wbs_ref[...])

    pairs = []
    for hp in range(2):
        l0, l1, l2 = l0_ref[hp], l1_ref[hp], l2_ref[hp]
        lm = jnp.maximum(jnp.maximum(l0, l1), l2)
        e0, e1, e2 = jnp.exp(l0 - lm), jnp.exp(l1 - lm), jnp.exp(l2 - lm)
        pairs.append((e0 * o0_ref[hp] + e1 * o1_ref[hp] + e2 * o2_ref[hp]) / (e0 + e1 + e2))
    attn = jnp.concatenate(pairs, axis=-1)
    attn_branch = _dot(attn.astype(BF16), wba_ref[...])

    g_ssm = _sigmoid(_dot(hb, wmg_ref[:, :D_MODEL]) + bmg_ref[:, :D_MODEL])
    g_attn = _sigmoid(_dot(hb, wmg_ref[:, D_MODEL:]) + bmg_ref[:, D_MODEL:])
    merged = g_ssm * ssm_branch + g_attn * attn_branch
    x1 = x + mod_ref[2] * _dot(merged.astype(BF16), wout_ref[...])
    x1_ref[...] = x1

    h2 = _rmsnorm(x1, gffn_ref[...]) * (1.0 + mod_ref[4]) + mod_ref[3]
    h2_hi = h2.astype(BF16)
    h2_ref[...] = h2_hi
    h2_lo = (h2 - h2_hi.astype(F32)).astype(BF16)
    both = _dot(h2_hi, wrl_ref[...])
    logits = (both[:, :ROUTER_LANES] + both[:, ROUTER_LANES:]) + _dot(h2_lo, wrh_ref[...]) + br_ref[...]

    lane = lax.broadcasted_iota(jnp.int32, (tm, ROUTER_LANES), 1)
    lanef = lane.astype(F32)
    low = jnp.float32(-3.0e38)
    is_grp = (lane >= GROUP_LOGIT_LANE0) & (lane < GROUP_LOGIT_LANE0 + N_EXPERT_GROUPS)
    lg = jnp.where(is_grp, logits, low)
    mg = jnp.max(lg, axis=-1, keepdims=True)
    gsel = jnp.min(jnp.where(lg == mg, lanef - GROUP_LOGIT_LANE0, 1e9), axis=-1, keepdims=True)
    pg_sel = 1.0 / jnp.sum(jnp.exp(lg - mg), axis=-1, keepdims=True)
    in_grp = (lane < N_EXPERTS) & ((lane // EXPERTS_PER_GROUP).astype(F32) == gsel)
    le = jnp.where(in_grp, logits, low)
    m1 = jnp.max(le, axis=-1, keepdims=True)
    i1 = jnp.min(jnp.where(le == m1, lanef, 1e9), axis=-1, keepdims=True)
    le2 = jnp.where(lanef == i1, low, le)
    m2 = jnp.max(le2, axis=-1, keepdims=True)
    i2 = jnp.min(jnp.where(le2 == m2, lanef, 1e9), axis=-1, keepdims=True)
    r = jnp.exp(m2 - m1)
    w1 = 1.0 / (1.0 + r)
    w2 = r / (1.0 + r)
    comb = pg_sel * (jnp.where(lanef == i1, w1, 0.0) + jnp.where(lanef == i2, w2, 0.0))
    comb_ref[...] = comb + jnp.where(lane == GROUP_ID_LANE, gsel, 0.0)


def _merge_call(x, mod, y, attn_outs, p, l, tm, tiles_per_seq):
    ntok = x.shape[0]
    tok = lambda w: pl.BlockSpec((tm, w), lambda i: (i, 0))
    lay = lambda *shape: pl.BlockSpec((None,) + shape, lambda i: (l,) + (0,) * len(shape),
                                      pipeline_mode=pl.Buffered(1))
    attn_specs = [pl.BlockSpec((2, tm, PAIR_WIDTH), lambda i: (0, i, 0))] * 6
    return pl.pallas_call(
        _merge_kernel,
        grid=(ntok // tm,),
        in_specs=[tok(D_MODEL), _mod_spec(mod, l, tm, tiles_per_seq), tok(SSM_WIDTH)] + attn_specs + [
            lay(1, D_MODEL), lay(1, D_MODEL),
            lay(D_MODEL, 2 * D_MODEL), lay(1, 2 * D_MODEL),
            lay(SSM_WIDTH, SSM_WIDTH), lay(SSM_WIDTH, D_MODEL), lay(GROUP_WIDTH, D_MODEL),
            lay(D_MODEL, D_MODEL),
            lay(D_MODEL, ROUTER_LANES), lay(D_MODEL, 2 * ROUTER_LANES), lay(1, ROUTER_LANES),
        ],
        out_specs=[tok(D_MODEL), tok(D_MODEL), tok(ROUTER_LANES)],
        out_shape=[jax.ShapeDtypeStruct((ntok, D_MODEL), F32),
                   jax.ShapeDtypeStruct((ntok, D_MODEL), BF16),
                   jax.ShapeDtypeStruct((ntok, ROUTER_LANES), F32)],
        compiler_params=_cparams(("parallel",)),
        name="merge_router",
    )(x, mod, y, *attn_outs, p['g_mix'], p['g_ffn'], p['w_mgate'], p['b_mgate'], p['w_glu'],
      p['w_br_ssm'], p['w_br_attn'], p['w_out'], p['w_router_hi'], p['w_router_lo'], p['b_router'])


MOE_EXPERT_BLOCK = 4
MOE_ROW_CHUNK = 128
GROUP_ID_LANE = 64
_TN = (((0,), (0,)), ((), ()))


def _moe_kernel(h2_ref, comb_ref, x1_ref, mod_ref, wg_ref, wu_ref, wd_ref, gfin_ref, out_ref,
                perm, earlier, hs, combs, ys, bounds, *, final):
    s = pl.program_id(1)
    tm = h2_ref.shape[0]
    ch = MOE_ROW_CHUNK

    @pl.when(s == 0)
    def _():
        comb = comb_ref[...]
        lane = lax.broadcasted_iota(jnp.int32, (tm, ROUTER_LANES), 1)
        gid = jnp.sum(jnp.where(lane == GROUP_ID_LANE, comb, 0.0), axis=-1, keepdims=True)
        onehot = jnp.where(lane.astype(F32) == gid, 1.0, 0.0)
        col = lax.broadcasted_iota(jnp.int32, (tm, tm), 1)

        @pl.when(pl.program_id(0) == 0)
        def _():
            row = lax.broadcasted_iota(jnp.int32, (tm, tm), 0)
            earlier[...] = jnp.where(col < row, 1.0, 0.0).astype(BF16)

        rank = _dot(earlier[...], onehot.astype(BF16))
        tot = jnp.sum(onehot, axis=0, keepdims=True)
        lane1 = lax.broadcasted_iota(jnp.int32, (1, ROUTER_LANES), 1)
        start = jnp.zeros((1, ROUTER_LANES), F32)
        run = jnp.zeros((1, 1), F32)
        for g in range(N_EXPERT_GROUPS):
            bounds[2 * g] = run[0, 0].astype(jnp.int32)
            start = start + jnp.where(lane1 == g, run, 0.0)
            run = run + jnp.sum(jnp.where(lane1 == g, tot, 0.0), axis=-1, keepdims=True)
            bounds[2 * g + 1] = run[0, 0].astype(jnp.int32)
        pos = jnp.sum(onehot * (rank + start), axis=-1, keepdims=True)
        perm[...] = jnp.where(col.astype(F32) == pos, 1.0, 0.0).astype(BF16)
        hs[...] = lax.dot_general(perm[...], h2_ref[...], _TN, preferred_element_type=F32).astype(BF16)
        c_hi = comb.astype(BF16).astype(F32)
        rest = comb - c_hi
        c_mid = rest.astype(BF16).astype(F32)
        c_lo = rest - c_mid
        packed = jnp.where(lane < 2 * N_EXPERTS, c_hi + pltpu.roll(c_mid, N_EXPERTS, axis=1),
                           jnp.where(lane >= 3 * N_EXPERTS, pltpu.roll(c_lo, 3 * N_EXPERTS, axis=1), c_hi))
        got = lax.dot_general(perm[...], packed.astype(BF16), _TN, preferred_element_type=F32)
        mid = pltpu.roll(jnp.where((lane >= N_EXPERTS) & (lane < 2 * N_EXPERTS), got, 0.0),
                         ROUTER_LANES - N_EXPERTS, axis=1)
        low = pltpu.roll(jnp.where(lane >= 3 * N_EXPERTS, got, 0.0), ROUTER_LANES - 3 * N_EXPERTS, axis=1)
        keep = (lane < N_EXPERTS) | (lane == GROUP_ID_LANE)
        combs[...] = (jnp.where(keep, got, 0.0) + mid) + low
        ys[...] = jnp.zeros_like(ys)

    g = s // (EXPERTS_PER_GROUP // MOE_EXPERT_BLOCK)
    lo = bounds[2 * g] // ch
    hi = (bounds[2 * g + 1] + (ch - 1)) // ch
    lane = lax.broadcasted_iota(jnp.int32, (ch, ROUTER_LANES), 1)

    def chunk(c):
        r0 = pl.multiple_of(c * ch, ch)
        h = hs[pl.ds(r0, ch), :]
        comb = combs[pl.ds(r0, ch), :]
        hids = []
        for j in range(MOE_EXPERT_BLOCK):
            gt = _dot(h, wg_ref[j])
            w = jnp.sum(jnp.where(lane == s * MOE_EXPERT_BLOCK + j, comb, 0.0), axis=-1, keepdims=True)
            hids.append((gt * _sigmoid(gt) * _dot(h, wu_ref[j]) * w).astype(BF16))
        ys[pl.ds(r0, ch), :] += _dot(jnp.concatenate(hids, axis=-1), wd_ref[...])

    def pair(i, carry):
        chunk(lo + 2 * i)
        chunk(lo + 2 * i + 1)
        return carry

    n_chunks = hi - lo
    lax.fori_loop(0, n_chunks // 2, pair, 0)

    @pl.when(n_chunks % 2 == 1)
    def _():
        chunk(hi - 1)

    @pl.when(s == pl.num_programs(1) - 1)
    def _():
        x2 = x1_ref[...] + mod_ref[...] * _dot(perm[...], ys[...].astype(BF16))
        out_ref[...] = _rmsnorm(x2, gfin_ref[...]) if final else x2


def _moe_call(h2, comb, x1, mod, p, l, tm, tiles_per_seq, final):
    ntok = h2.shape[0]
    eb = MOE_EXPERT_BLOCK
    tok = lambda w: pl.BlockSpec((tm, w), lambda i, s: (i, 0))
    gate_piece = 5
    if mod.ndim == 5:
        mod_spec = pl.BlockSpec((None, None, None, 1, D_MODEL),
                                lambda i, s: (l, gate_piece, i // tiles_per_seq, 0, 0))
    else:
        mod_spec = pl.BlockSpec((None, None, tm, D_MODEL), lambda i, s: (l, gate_piece, i, 0))
    w_down = p['w_exp_down'].reshape(DEPTH, N_EXPERTS // eb, eb * EXPERT_FF, D_MODEL)
    return pl.pallas_call(
        functools.partial(_moe_kernel, final=final),
        grid=(ntok // tm, N_EXPERTS // eb),
        in_specs=[tok(D_MODEL), tok(ROUTER_LANES), tok(D_MODEL), mod_spec,
                  pl.BlockSpec((None, eb, D_MODEL, EXPERT_FF), lambda i, s: (l, s, 0, 0)),
                  pl.BlockSpec((None, eb, D_MODEL, EXPERT_FF), lambda i, s: (l, s, 0, 0)),
                  pl.BlockSpec((None, None, eb * EXPERT_FF, D_MODEL), lambda i, s: (l, s, 0, 0)),
                  pl.BlockSpec((1, D_MODEL), lambda i, s: (0, 0))],
        out_specs=tok(D_MODEL),
        out_shape=jax.ShapeDtypeStruct((ntok, D_MODEL), F32),
        scratch_shapes=[pltpu.VMEM((tm, tm), BF16), pltpu.VMEM((tm, tm), BF16), pltpu.VMEM((tm, D_MODEL), BF16),
                        pltpu.VMEM((tm, ROUTER_LANES), F32), pltpu.VMEM((tm, D_MODEL), F32),
                        pltpu.SMEM((2 * N_EXPERT_GROUPS,), jnp.int32)],
        compiler_params=_cparams(("arbitrary", "arbitrary")),
        name="moe_experts",
    )(h2, comb, x1, mod, p['w_exp_gate'], p['w_exp_up'], w_down, p['g_final'])


def _kv_state(qkv, n_seq, seq, prompt):
    kv = []
    for g in range(N_GROUPS):
        keep = min(WINDOWS[g], seq)
        if prompt:
            tails = _kv_tail_call(qkv[g], n_seq, seq, keep).reshape(2, n_seq, HEADS, HEAD_DIM, keep)
            kv += [tails[0].transpose(0, 3, 1, 2), tails[1].transpose(0, 3, 1, 2)]
        else:
            for a in (1, 2):
                new = qkv[g][a].reshape(2, n_seq, seq, 2, HEAD_DIM)
                kv.append(new.transpose(1, 2, 0, 3, 4).reshape(n_seq, seq, HEADS, HEAD_DIM))
    return kv


def _run_trunks(xp, xs, mod_p, mod_s, p, ssm, batch, seq, dec_batch, dec_seq, caches, h0):
    tm_p, tm_s = 512, 256
    tm_proj_p = 1024
    tm_moe_p, tm_moe_s = 1024, 512
    wide = N_GROUPS - 1
    states_p, states_s = [], []
    for l in range(DEPTH):
        ab, bre, bim, cre, cim, d_skip = ssm[l]
        up, *qkv_p = _inproj_call(xp, mod_p, p['g_mix'], p['w_in'], l, tm_proj_p, seq // tm_proj_p)
        us, *qkv_s = _inproj_call(xs, mod_s, p['g_mix'], p['w_in'], l, tm_s, 1)

        (yp, hre_p, him_p), attn_wide = _scan_attn_call(
            up, ab, bre, bim, cre, cim, d_skip, batch, seq,
            qkv_s[wide], caches[2 * wide], caches[2 * wide + 1], l, dec_batch, dec_seq, DILATIONS[wide])
        attn_p, attn_s = [], []
        for g in range(N_GROUPS - 1):
            o_p, o_s = _attn_prompt_call(qkv_p[g], batch, seq, DILATIONS[g],
                                         sample=(qkv_s[g], caches[2 * g], caches[2 * g + 1], l, dec_batch, dec_seq))
            attn_p += list(o_p)
            attn_s += list(o_s)
        attn_p += list(_attn_prompt_call(qkv_p[wide], batch, seq, DILATIONS[wide]))
        attn_s += list(attn_wide)
        x1, h2, comb = _merge_call(xp, mod_p, yp, attn_p, p, l, tm_p, seq // tm_p)
        xp = _moe_call(h2, comb, x1, mod_p, p, l, tm_moe_p, seq // tm_moe_p, final=(l == DEPTH - 1))
        states_p.append(_kv_state(qkv_p, batch, seq, True)
                        + [hre_p[:, 0].reshape(batch, N_SSM_GROUPS, SSM_STATE),
                           him_p[:, 0].reshape(batch, N_SSM_GROUPS, SSM_STATE)])

        ys, hre_s, him_s = _ssm_sample_call(us.reshape(dec_batch, dec_seq * SSM_WIDTH), ab, bre, bim, cre, cim,
                                            d_skip, h0[0][l], h0[1][l], dec_seq)
        x1, h2, comb = _merge_call(xs, mod_s, ys.reshape(dec_batch * dec_seq, SSM_WIDTH), attn_s, p, l, tm_s, 1)
        xs = _moe_call(h2, comb, x1, mod_s, p, l, tm_moe_s, 1, final=(l == DEPTH - 1))
        states_s.append(_kv_state(qkv_s, dec_batch, dec_seq, False)
                        + [hre_s.reshape(dec_batch, N_SSM_GROUPS, SSM_STATE),
                           him_s.reshape(dec_batch, N_SSM_GROUPS, SSM_STATE)])
    stack = lambda states: [jnp.stack([states[l][i] for l in range(DEPTH)], axis=0)
                            for i in range(len(states[0]))]
    return xp, xs, stack(states_p), stack(states_s)


def kernel(x_prompt, x_sample, cache_k_w128, cache_v_w128, cache_k_w512, cache_v_w512,
           cache_k_w2048, cache_v_w2048, state_ssm_re, state_ssm_im, c_prompt, c_sample,
           w_ada, b_ada, g_norm_mix, g_norm_ffn, g_final, w_in, ssm_a_re, ssm_a_im, ssm_b_re,
           ssm_b_im, ssm_c_re, ssm_c_im, ssm_log_step, ssm_d, w_glu, w_br_ssm, w_br_attn,
           w_mgate, b_mgate, w_out, w_router_grp, b_router_grp, w_router_exp, b_router_exp,
           w_exp_gate, w_exp_up, w_exp_down):
    batch, seq, _ = x_prompt.shape
    dec_batch, dec_seq, _ = x_sample.shape

    pad = -batch % SUBLANES
    c_p = jnp.concatenate([c_prompt, jnp.zeros((pad, D_MODEL), F32)], axis=0)
    mod_p, mod_s = _ada_call(c_p, jnp.repeat(c_sample, dec_seq, axis=0), w_ada, b_ada)
    mod_prompt = mod_p.reshape(DEPTH, 6, batch + pad, 1, D_MODEL)
    mod_sample = mod_s

    abr, abi, bbr, bbi = _ssm_param_call(ssm_a_re, ssm_a_im, ssm_log_step, ssm_b_re, ssm_b_im)
    ssm = []
    for l in range(DEPTH):
        pick = lambda a: a[l].reshape(N_SSM_GROUPS, SSM_STATE, SSM_GROUP_CH)
        ab = jnp.stack([pick(abr)[:, :, 0].reshape(SSM_LANES), pick(abi)[:, :, 0].reshape(SSM_LANES)])
        bre = _block_diag_halves(pick(bbr).transpose(0, 2, 1), BF16)
        bim = _block_diag_halves(pick(bbi).transpose(0, 2, 1), BF16)
        cre = _block_diag_halves(ssm_c_re[l].transpose(0, 2, 1), BF16)
        cim = _block_diag_halves(ssm_c_im[l].transpose(0, 2, 1), BF16)
        ssm.append((ab, bre, bim, cre, cim, ssm_d[l].reshape(1, SSM_WIDTH)))

    w_router = jnp.concatenate(
        [w_router_exp, w_router_grp,
         jnp.zeros((DEPTH, D_MODEL, ROUTER_LANES - N_EXPERTS - N_EXPERT_GROUPS), F32)], axis=-1)
    b_router = jnp.concatenate(
        [b_router_exp, b_router_grp,
         jnp.zeros((DEPTH, ROUTER_LANES - N_EXPERTS - N_EXPERT_GROUPS), F32)], axis=-1)
    w_router_hi = w_router.astype(BF16)
    p = dict(
        g_mix=g_norm_mix.reshape(DEPTH, 1, D_MODEL), g_ffn=g_norm_ffn.reshape(DEPTH, 1, D_MODEL),
        g_final=g_final.reshape(1, D_MODEL), w_in=w_in.astype(BF16), w_glu=w_glu.astype(BF16),
        w_br_ssm=w_br_ssm.astype(BF16), w_br_attn=w_br_attn.astype(BF16),
        w_mgate=w_mgate.astype(BF16), b_mgate=b_mgate.reshape(DEPTH, 1, 2 * D_MODEL),
        w_out=w_out.astype(BF16), w_router_hi=w_router_hi,
        w_router_lo=jnp.concatenate([w_router_hi, (w_router - w_router_hi.astype(F32)).astype(BF16)], axis=-1),
        b_router=b_router.reshape(DEPTH, 1, ROUTER_LANES),
        w_exp_gate=w_exp_gate.astype(BF16), w_exp_up=w_exp_up.astype(BF16),
        w_exp_down=w_exp_down.astype(BF16))

    caches = [cache_k_w128, cache_v_w128, cache_k_w512, cache_v_w512, cache_k_w2048, cache_v_w2048]
    h0 = (state_ssm_re.reshape(DEPTH, dec_batch, SSM_LANES), state_ssm_im.reshape(DEPTH, dec_batch, SSM_LANES))
    y_prompt, y_sample, pstate, sstate = _run_trunks(
        x_prompt.reshape(batch * seq, D_MODEL), x_sample.reshape(dec_batch * dec_seq, D_MODEL),
        mod_prompt, mod_sample, p, ssm, batch, seq, dec_batch, dec_seq, caches, h0)
    return (y_prompt.reshape(batch, seq, D_MODEL), y_sample.reshape(dec_batch, dec_seq, D_MODEL),
            *pstate, *sstate)
```
